```python
import jax, jax.numpy as jnp
from jax import lax
import numpy as np

D_MODEL = 4096
BATCH = 2
SEQ = 4096
DEPTH = 2
DEC_BATCH = 8
DEC_SEQ = 32
PAST_LEN = 4096

CHUNK = 64
N_A_LAYERS = DEPTH // 2
N_B_LAYERS = DEPTH - N_A_LAYERS
SGU_CHUNK = 128
SGU_GROUPS = 8
SGU_WIDTH = D_MODEL
SGU_GROUP_WIDTH = SGU_WIDTH // SGU_GROUPS
HEAD_DIM = 64
N_HEADS = D_MODEL // HEAD_DIM
N_KV_HEADS = 8
GQA = N_HEADS // N_KV_HEADS
WINDOW = 128
WINDOW_CHUNKS = WINDOW // CHUNK
D_FF = ((8 * D_MODEL // 3 + 255) // 256) * 256
NEG = -1e30

kernel_name = "yoco_gmlp_swa_sink_stream_step"


def rmsnorm(x, g, eps=1e-6):
    xf = x.astype(jnp.float32)
    y = xf * lax.rsqrt(jnp.mean(xf * xf, axis=-1, keepdims=True) + eps)
    return (y * g.astype(jnp.float32)).astype(x.dtype)


def layernorm(x, g, b, eps=1e-5):
    xf = x.astype(jnp.float32)
    mu = jnp.mean(xf, axis=-1, keepdims=True)
    xc = xf - mu
    y = xc * lax.rsqrt(jnp.mean(xc * xc, axis=-1, keepdims=True) + eps)
    return (y * g.astype(jnp.float32) + b.astype(jnp.float32)).astype(x.dtype)


def swiglu(x, w_gate, w_up, w_down):
    return (jax.nn.silu(x @ w_gate) * (x @ w_up)) @ w_down


def chunk_causal_mask(n):
    idx = jnp.arange(n) // CHUNK
    return idx[:, None] >= idx[None, :]


def sgu_mixer(x, w_in, ln_g, ln_b, w_s, b_s, w_out, prompt):
    B, S, _ = x.shape
    u, v = jnp.split(jax.nn.gelu(x @ w_in, approximate=False), 2, axis=-1)
    v = layernorm(v, ln_g, ln_b)
    ws = jnp.where(chunk_causal_mask(SGU_CHUNK)[None], w_s, 0.0)
    if prompt:
        vb = v.reshape(B, S // SGU_CHUNK, SGU_CHUNK, SGU_GROUPS, SGU_GROUP_WIDTH)
        mixed = jnp.einsum('gij,bnjgc->bnigc', ws, vb) + b_s.T[:, :, None]
    else:
        vb = v.reshape(B, S, SGU_GROUPS, SGU_GROUP_WIDTH)
        mixed = jnp.einsum('gij,bjgc->bigc', ws[:, :S, :S], vb) + b_s[:, :S].T[:, :, None]
    mixed = mixed.reshape(B, S, SGU_WIDTH)
    return (u * mixed) @ w_out, v


def shared_kv(h, norm_kv, w_kv, k_norm):
    B, S, _ = h.shape
    kv = rmsnorm(h, norm_kv) @ w_kv
    k, v = jnp.split(kv, 2, axis=-1)
    k = rmsnorm(k.reshape(B, S, N_KV_HEADS, HEAD_DIM), k_norm)
    v = v.reshape(B, S, N_KV_HEADS, HEAD_DIM)
    return k, v


def sink_softmax(s, sinks):
    col = jnp.broadcast_to(sinks.astype(jnp.float32).reshape(N_KV_HEADS, GQA, 1, 1),
                           s.shape[:-1] + (1,))
    return jax.nn.softmax(jnp.concatenate([s, col], axis=-1), axis=-1)[..., :-1]


def project_q(hn, w_q, q_norm):
    B, S, _ = hn.shape
    q = (hn @ w_q).reshape(B, S, N_KV_HEADS, GQA, HEAD_DIM)
    return rmsnorm(q, q_norm)


def swa_prompt(hn, w_q, q_norm, sinks, w_o, k, v):
    B, S, _ = hn.shape
    nc = S // CHUNK
    nkb = (WINDOW_CHUNKS + 1) * CHUNK
    q = project_q(hn, w_q, q_norm).reshape(B, nc, CHUNK, N_KV_HEADS, GQA, HEAD_DIM)
    pad = ((0, 0), (WINDOW_CHUNKS * CHUNK, 0), (0, 0), (0, 0))
    kp = jnp.pad(k, pad).reshape(B, nc + WINDOW_CHUNKS, CHUNK, N_KV_HEADS, HEAD_DIM)
    vp = jnp.pad(v, pad).reshape(B, nc + WINDOW_CHUNKS, CHUNK, N_KV_HEADS, HEAD_DIM)
    kb = jnp.concatenate([kp[:, j:j + nc] for j in range(WINDOW_CHUNKS + 1)], axis=2)
    vb = jnp.concatenate([vp[:, j:j + nc] for j in range(WINDOW_CHUNKS + 1)], axis=2)
    s = jnp.einsum('bcqkgd,bcskd->bckgqs', q, kb,
                   preferred_element_type=jnp.float32) * (HEAD_DIM ** -0.5)
    valid = (jnp.arange(nc)[:, None] + jnp.arange(nkb)[None, :] // CHUNK - WINDOW_CHUNKS) >= 0
    s = jnp.where(valid[None, :, None, None, None, :], s, NEG)
    p = sink_softmax(s, sinks).astype(v.dtype)
    o = jnp.einsum('bckgqs,bcskd->bcqkgd', p, vb).reshape(B, S, N_HEADS * HEAD_DIM)
    return o @ w_o


def swa_sample(hn, w_q, q_norm, sinks, w_o, k_all, v_all):
    B, T, _ = hn.shape
    q = project_q(hn, w_q, q_norm)
    s = jnp.einsum('btkgd,bskd->bkgts', q, k_all,
                   preferred_element_type=jnp.float32) * (HEAD_DIM ** -0.5)
    p = sink_softmax(s, sinks).astype(v_all.dtype)
    o = jnp.einsum('bkgts,bskd->btkgd', p, v_all).reshape(B, T, N_HEADS * HEAD_DIM)
    return o @ w_o


def trunk(x, cache_k, cache_v, norm_a, w_sgu_in, sgu_ln_g, sgu_ln_b, w_sgu_s, b_sgu_s,
          w_sgu_out, norm_kv, w_kv, k_norm, norm_b, w_q, q_norm, sinks, w_o,
          norm_ffn, w_ffn_gate, w_ffn_up, w_ffn_down):
    prompt = cache_k is None
    h = x
    v_rows = []
    k = v = None
    for l in range(DEPTH):
        if l < N_A_LAYERS:
            a = l
            out, vr = sgu_mixer(rmsnorm(h, norm_a[a]), w_sgu_in[a], sgu_ln_g[a], sgu_ln_b[a],
                                w_sgu_s[a], b_sgu_s[a], w_sgu_out[a], prompt)
            h = h + out
            if not prompt:
                v_rows.append(vr)
        else:
            if l == N_A_LAYERS:
                k, v = shared_kv(h, norm_kv, w_kv, k_norm)
            b = l - N_A_LAYERS
            hn = rmsnorm(h, norm_b[b])
            if prompt:
                out = swa_prompt(hn, w_q[b], q_norm[b], sinks[b], w_o[b], k, v)
            else:
                k_all = jnp.concatenate([cache_k, k], axis=1)
                v_all = jnp.concatenate([cache_v, v], axis=1)
                out = swa_sample(hn, w_q[b], q_norm[b], sinks[b], w_o[b], k_all, v_all)
            h = h + out
        h = h + swiglu(rmsnorm(h, norm_ffn[l]), w_ffn_gate[l], w_ffn_up[l], w_ffn_down[l])
    return h, k, v, v_rows


def setup_inputs(seed: int = 0) -> dict:
    key = jax.random.key(seed)
    ks = jax.random.split(key, 32)
    f32 = jnp.float32

    def nrm(k, shape, scale):
        return jax.random.normal(k, shape, f32) * scale

    def gain(k, shape):
        return 1.0 + 0.02 * jax.random.normal(k, shape, f32)

    return {
        "x_prompt": nrm(ks[0], (BATCH, SEQ, D_MODEL), 1.0),
        "x_sample": nrm(ks[1], (DEC_BATCH, DEC_SEQ, D_MODEL), 1.0),
        "cache_k": nrm(ks[2], (DEC_BATCH, WINDOW, N_KV_HEADS, HEAD_DIM), 1.0),
        "cache_v": nrm(ks[3], (DEC_BATCH, WINDOW, N_KV_HEADS, HEAD_DIM), 1.0),
        "norm_a": gain(ks[4], (N_A_LAYERS, D_MODEL)),
        "w_sgu_in": nrm(ks[5], (N_A_LAYERS, D_MODEL, 2 * SGU_WIDTH), D_MODEL ** -0.5),
        "sgu_ln_g": gain(ks[6], (N_A_LAYERS, SGU_WIDTH)),
        "sgu_ln_b": nrm(ks[7], (N_A_LAYERS, SGU_WIDTH), 0.02),
        "w_sgu_s": nrm(ks[8], (N_A_LAYERS, SGU_GROUPS, SGU_CHUNK, SGU_CHUNK), SGU_CHUNK ** -0.5),
        "b_sgu_s": gain(ks[9], (N_A_LAYERS, SGU_GROUPS, SGU_CHUNK)),
        "w_sgu_out": nrm(ks[10], (N_A_LAYERS, SGU_WIDTH, D_MODEL), SGU_WIDTH ** -0.5),
        "norm_kv": gain(ks[11], (D_MODEL,)),
        "w_kv": nrm(ks[12], (D_MODEL, 2 * N_KV_HEADS * HEAD_DIM), D_MODEL ** -0.5),
        "k_norm": gain(ks[13], (HEAD_DIM,)),
        "norm_b": gain(ks[14], (N_B_LAYERS, D_MODEL)),
        "w_q": nrm(ks[15], (N_B_LAYERS, D_MODEL, N_HEADS * HEAD_DIM), D_MODEL ** -0.5),
        "q_norm": gain(ks[16], (N_B_LAYERS, HEAD_DIM)),
        "sinks": nrm(ks[17], (N_B_LAYERS, N_HEADS), 0.5),
        "w_o": nrm(ks[18], (N_B_LAYERS, N_HEADS * HEAD_DIM, D_MODEL), (N_HEADS * HEAD_DIM) ** -0.5),
        "norm_ffn": gain(ks[19], (DEPTH, D_MODEL)),
        "w_ffn_gate": nrm(ks[20], (DEPTH, D_MODEL, D_FF), D_MODEL ** -0.5),
        "w_ffn_up": nrm(ks[21], (DEPTH, D_MODEL, D_FF), D_MODEL ** -0.5),
        "w_ffn_down": nrm(ks[22], (DEPTH, D_FF, D_MODEL), D_FF ** -0.5),
    }


def reference(x_prompt, x_sample, cache_k, cache_v, norm_a, w_sgu_in, sgu_ln_g, sgu_ln_b,
              w_sgu_s, b_sgu_s, w_sgu_out, norm_kv, w_kv, k_norm, norm_b, w_q, q_norm,
              sinks, w_o, norm_ffn, w_ffn_gate, w_ffn_up, w_ffn_down):
    weights = (norm_a, w_sgu_in, sgu_ln_g, sgu_ln_b, w_sgu_s, b_sgu_s, w_sgu_out,
               norm_kv, w_kv, k_norm, norm_b, w_q, q_norm, sinks, w_o,
               norm_ffn, w_ffn_gate, w_ffn_up, w_ffn_down)
    y_prompt, k_p, v_p, _ = trunk(x_prompt, None, None, *weights)
    new_k_prompt = k_p[:, -WINDOW:]
    new_v_prompt = v_p[:, -WINDOW:]
    y_sample, new_k_sample, new_v_sample, v_rows = trunk(x_sample, cache_k, cache_v, *weights)
    new_sgu_v_sample = jnp.stack(v_rows, axis=0)
    return (y_prompt, y_sample, new_k_prompt, new_v_prompt, new_k_sample, new_v_sample,
            new_sgu_v_sample)
```

```python
import functools

import jax
import jax.numpy as jnp
from jax import lax
from jax.experimental import pallas as pl
from jax.experimental.pallas import tpu as pltpu

F32 = jnp.float32
BF16 = jnp.bfloat16

D_MODEL = 4096
SGU_CHUNK = 128
SGU_GROUPS = 8
SGU_GROUP_WIDTH = D_MODEL // SGU_GROUPS
CHUNK = 64
HEAD_DIM = 64
N_HEADS = 64
N_KV_HEADS = 8
GQA = N_HEADS // N_KV_HEADS
KV_WIDTH = N_KV_HEADS * HEAD_DIM
WINDOW = 128
NEG = -1e30
RMS_EPS = 1e-6
LN_EPS = 1e-5

V7X_VMEM_BYTES = 64 * 1024 * 1024
VMEM_LIMIT_CAP = V7X_VMEM_BYTES - 6 * 1024 * 1024
BF16_SUBLANES = 16

ROW_TILE = 768
ATT_ROWS = 128


def _params(vmem_bytes, n_axes):
    limit = min(int(vmem_bytes) + 8 * 1024 * 1024, VMEM_LIMIT_CAP)
    return pltpu.CompilerParams(
        dimension_semantics=("arbitrary",) * n_axes, vmem_limit_bytes=limit)


def _rms_rows_to_bf16(x_ref, g_ref, xn_ref):
    g = g_ref[...]
    n_steps = x_ref.shape[0] // BF16_SUBLANES

    def body(r, carry):
        rows = pl.ds(pl.multiple_of(r * BF16_SUBLANES, BF16_SUBLANES), BF16_SUBLANES)
        x = x_ref[rows, :]
        ms = jnp.mean(x * x, axis=-1, keepdims=True)
        xn_ref[rows, :] = (x * lax.rsqrt(ms + RMS_EPS) * g).astype(BF16)
        return carry

    lax.fori_loop(0, n_steps, body, 0)


def _head_rms(y, gain_row, scale):
    ms = jnp.mean(y * y, axis=-1, keepdims=True)
    out = y * lax.rsqrt(ms + RMS_EPS) * gain_row
    return out if scale == 1.0 else out * scale


def _sgu_in_kernel(x_ref, g_ref, w_ref, o_ref, xn_ref):
    @pl.when(pl.program_id(1) == 0)
    def _():
        _rms_rows_to_bf16(x_ref, g_ref, xn_ref)

    y = jnp.dot(xn_ref[...], w_ref[...], preferred_element_type=F32)
    o_ref[...] = 0.5 * y * (1.0 + lax.erf(y * (0.5 ** 0.5)))


def _sgu_in(x, gain, w, *, bm, bn):
    m, k = x.shape
    n = w.shape[1]
    vmem = 2 * bm * k * 4 + bm * k * 2 + 2 * k * bn * 2 + 3 * bm * bn * 4
    return pl.pallas_call(
        _sgu_in_kernel,
        grid=(m // bm, n // bn),
        in_specs=[
            pl.BlockSpec((bm, k), lambda i, j: (i, 0)),
            pl.BlockSpec((1, k), lambda i, j: (0, 0)),
            pl.BlockSpec((k, bn), lambda i, j: (0, j)),
        ],
        out_specs=pl.BlockSpec((bm, bn), lambda i, j: (i, j)),
        out_shape=jax.ShapeDtypeStruct((m, n), F32),
        scratch_shapes=[pltpu.VMEM((bm, k), BF16)],
        compiler_params=_params(vmem, 2),
        name="sgu_in",
    )(x, gain, w)


def _sgu_mix_kernel(u_ref, v_ref, lng_ref, lnb_ref, ws_ref, bs_ref, *out_refs, chunk, n_chunks):
    a_ref = out_refs[0]
    vn_ref = out_refs[1] if len(out_refs) > 1 else None
    ln_g = lng_ref[...]
    ln_b = lnb_ref[...]
    for c in range(n_chunks):
        rows = slice(c * chunk, (c + 1) * chunk)
        v = v_ref[rows, :]
        mu = jnp.mean(v, axis=-1, keepdims=True)
        vc = v - mu
        var = jnp.mean(vc * vc, axis=-1, keepdims=True)
        vn = vc * lax.rsqrt(var + LN_EPS) * ln_g + ln_b
        if vn_ref is not None:
            vn_ref[rows, :] = vn
        vnb = vn.astype(BF16)
        for g in range(SGU_GROUPS):
            cols = slice(g * SGU_GROUP_WIDTH, (g + 1) * SGU_GROUP_WIDTH)
            mixed = jnp.dot(ws_ref[g], vnb[:, cols], preferred_element_type=F32)
            mixed = mixed + bs_ref[:, g:g + 1]
            a_ref[rows, cols] = (u_ref[rows, cols] * mixed).astype(BF16)


def _sgu_mix(uv, ln_g, ln_b, ws, bs_t, a_prev, *, chunk, n_chunks, row_block0, n_blocks, emit_vn):
    m = uv.shape[0]
    d = D_MODEL
    rows = chunk * n_chunks
    kernel = functools.partial(_sgu_mix_kernel, chunk=chunk, n_chunks=n_chunks)
    in_specs = [
        pl.BlockSpec((rows, d), lambda i: (row_block0 + i, 0)),
        pl.BlockSpec((rows, d), lambda i: (row_block0 + i, 1)),
        pl.BlockSpec((1, d), lambda i: (0, 0)),
        pl.BlockSpec((1, d), lambda i: (0, 0)),
        pl.BlockSpec((SGU_GROUPS, chunk, chunk), lambda i: (0, 0, 0)),
        pl.BlockSpec((chunk, SGU_GROUPS), lambda i: (0, 0)),
    ]
    args = [uv, uv, ln_g, ln_b, ws, bs_t]
    out_shape = [jax.ShapeDtypeStruct((m, d), BF16)]
    out_specs = [pl.BlockSpec((rows, d), lambda i: (row_block0 + i, 0))]
    aliases = {}
    if a_prev is not None:
        in_specs.append(pl.BlockSpec(memory_space=pl.ANY))
        args.append(a_prev)
        aliases = {len(args) - 1: 0}
        kernel = _drop_last_input(kernel, n_in=len(args))
    if emit_vn:
        out_shape.append(jax.ShapeDtypeStruct((rows * n_blocks, d), F32))
        out_specs.append(pl.BlockSpec((rows, d), lambda i: (i, 0)))
    vmem = 2 * 2 * rows * d * 4 + 2 * rows * d * 2 + 2 * rows * d * 4 + 6 * chunk * d * 4
    return pl.pallas_call(
        kernel,
        grid=(n_blocks,),
        in_specs=in_specs,
        out_specs=out_specs,
        out_shape=out_shape,
        input_output_aliases=aliases,
        compiler_params=_params(vmem, 1),
        name="sgu_mix_%d" % chunk,
    )(*args)


def _drop_last_input(kernel, n_in):
    def wrapped(*refs):
        return kernel(*refs[:n_in - 1], *refs[n_in:])
    return wrapped


def _mm_res_kernel(a_ref, w_ref, res_ref, o_ref):
    o_ref[...] = res_ref[...] + jnp.dot(a_ref[...], w_ref[...], preferred_element_type=F32)


def _mm_res(a, w, res, *, bm, bn, name):
    m, k = a.shape
    n = w.shape[1]
    vmem = 2 * bm * k * 2 + 2 * k * bn * 2 + 5 * bm * bn * 4
    return pl.pallas_call(
        _mm_res_kernel,
        grid=(m // bm, n // bn),
        in_specs=[
            pl.BlockSpec((bm, k), lambda i, j: (i, 0)),
            pl.BlockSpec((k, bn), lambda i, j: (0, j)),
            pl.BlockSpec((bm, bn), lambda i, j: (i, j)),
        ],
        out_specs=pl.BlockSpec((bm, bn), lambda i, j: (i, j)),
        out_shape=jax.ShapeDtypeStruct((m, n), F32),
        compiler_params=_params(vmem, 2),
        name=name,
    )(a, w, res)


def _ffn_up_kernel(x_ref, g_ref, wg_ref, wu_ref, o_ref, xn_ref):
    @pl.when(pl.program_id(1) == 0)
    def _():
        _rms_rows_to_bf16(x_ref, g_ref, xn_ref)

    xn = xn_ref[...]
    gate = jnp.dot(xn, wg_ref[...], preferred_element_type=F32)
    up = jnp.dot(xn, wu_ref[...], preferred_element_type=F32)
    o_ref[...] = (gate * jax.nn.sigmoid(gate) * up).astype(BF16)


def _ffn_up(x, gain, w_gate, w_up, *, bm, bn):
    m, k = x.shape
    n = w_gate.shape[1]
    vmem = 2 * bm * k * 4 + bm * k * 2 + 4 * k * bn * 2 + 2 * bm * bn * 2 + 3 * bm * bn * 4
    return pl.pallas_call(
        _ffn_up_kernel,
        grid=(m // bm, n // bn),
        in_specs=[
            pl.BlockSpec((bm, k), lambda i, j: (i, 0)),
            pl.BlockSpec((1, k), lambda i, j: (0, 0)),
            pl.BlockSpec((k, bn), lambda i, j: (0, j)),
            pl.BlockSpec((k, bn), lambda i, j: (0, j)),
        ],
        out_specs=pl.BlockSpec((bm, bn), lambda i, j: (i, j)),
        out_shape=jax.ShapeDtypeStruct((m, n), BF16),
        scratch_shapes=[pltpu.VMEM((bm, k), BF16)],
        compiler_params=_params(vmem, 2),
        name="ffn_up",
    )(x, gain, w_gate, w_up)


def _q_proj_kernel(x_ref, g_ref, w_ref, qg_ref, q_ref, xn_ref, *, heads_per_block):
    @pl.when(pl.program_id(1) == 0)
    def _():
        _rms_rows_to_bf16(x_ref, g_ref, xn_ref)

    y = jnp.dot(xn_ref[...], w_ref[...], preferred_element_type=F32)
    qg = qg_ref[...]
    for h in range(heads_per_block):
        yh = y[:, h * HEAD_DIM:(h + 1) * HEAD_DIM]
        q_ref[h] = _head_rms(yh, qg, HEAD_DIM ** -0.5).astype(BF16)


def _q_proj(x, gain, w, q_gain, *, bm, bn):
    m, k = x.shape
    n = w.shape[1]
    hpb = bn // HEAD_DIM
    vmem = 2 * bm * k * 4 + bm * k * 2 + 2 * k * bn * 2 + 3 * bm * bn * 4 + 2 * hpb * bm * 128 * 2
    return pl.pallas_call(
        functools.partial(_q_proj_kernel, heads_per_block=hpb),
        grid=(m // bm, n // bn),
        in_specs=[
            pl.BlockSpec((bm, k), lambda i, j: (i, 0)),
            pl.BlockSpec((1, k), lambda i, j: (0, 0)),
            pl.BlockSpec((k, bn), lambda i, j: (0, j)),
            pl.BlockSpec((1, HEAD_DIM), lambda i, j: (0, 0)),
        ],
        out_specs=pl.BlockSpec((hpb, bm, HEAD_DIM), lambda i, j: (j, i, 0)),
        out_shape=jax.ShapeDtypeStruct((n // HEAD_DIM, m, HEAD_DIM), BF16),
        scratch_shapes=[pltpu.VMEM((bm, k), BF16)],
        compiler_params=_params(vmem, 2),
        name="q_proj",
    )(x, gain, w, q_gain)


def _kv_proj_kernel(x_ref, g_ref, w_ref, kg_ref, kf_ref, vf_ref, kh_ref, vh_ref, xn_ref):
    _rms_rows_to_bf16(x_ref, g_ref, xn_ref)
    y = jnp.dot(xn_ref[...], w_ref[...], preferred_element_type=F32)
    kg = kg_ref[...]
    k_heads = []
    for h in range(N_KV_HEADS):
        kn = _head_rms(y[:, h * HEAD_DIM:(h + 1) * HEAD_DIM], kg, 1.0)
        kh_ref[h] = kn.astype(BF16)
        k_heads.append(kn)
        vh_ref[h] = y[:, KV_WIDTH + h * HEAD_DIM:KV_WIDTH + (h + 1) * HEAD_DIM].astype(BF16)
    kf_ref[...] = jnp.concatenate(k_heads, axis=-1)
    vf_ref[...] = y[:, KV_WIDTH:]


def _kv_proj(x, gain, w, k_gain, *, bm):
    m, k = x.shape
    n = w.shape[1]
    vmem = (2 * bm * k * 4 + bm * k * 2 + 2 * k * n * 2 + 3 * bm * n * 4 + 2 * bm * n * 4
            + 4 * N_KV_HEADS * bm * 128 * 2)
    head_spec = pl.BlockSpec((N_KV_HEADS, bm, HEAD_DIM), lambda i: (0, i, 0))
    row_spec = pl.BlockSpec((bm, KV_WIDTH), lambda i: (i, 0))
    return pl.pallas_call(
        _kv_proj_kernel,
        grid=(m // bm,),
        in_specs=[
            pl.BlockSpec((bm, k), lambda i: (i, 0)),
            pl.BlockSpec((1, k), lambda i: (0, 0)),
            pl.BlockSpec((k, n), lambda i: (0, 0)),
            pl.BlockSpec((1, HEAD_DIM), lambda i: (0, 0)),
        ],
        out_specs=[row_spec, row_spec, head_spec, head_spec],
        out_shape=[
            jax.ShapeDtypeStruct((m, KV_WIDTH), F32),
            jax.ShapeDtypeStruct((m, KV_WIDTH), F32),
            jax.ShapeDtypeStruct((N_KV_HEADS, m, HEAD_DIM), BF16),
            jax.ShapeDtypeStruct((N_KV_HEADS, m, HEAD_DIM), BF16),
        ],
        scratch_shapes=[pltpu.VMEM((bm, k), BF16)],
        compiler_params=_params(vmem, 1),
        name="kv_proj",
    )(x, gain, w, k_gain)


def _attend(q_ref, keys_of, vals_of, sink_ref, o_ref, *, tq, valid):
    for kh in range(N_KV_HEADS):
        q = q_ref[kh * GQA:(kh + 1) * GQA].reshape(GQA * tq, HEAD_DIM)
        s = lax.dot_general(q, keys_of(kh), (((1,), (1,)), ((), ())),
                            preferred_element_type=F32)
        if valid is not None:
            s = jnp.where(valid, s, NEG)
        sink = jnp.concatenate(
            [jnp.full((tq, 1), sink_ref[kh * GQA + g], F32) for g in range(GQA)], axis=0)
        m = jnp.maximum(jnp.max(s, axis=-1, keepdims=True), sink)
        e = jnp.exp(s - m)
        denom = jnp.sum(e, axis=-1, keepdims=True) + jnp.exp(sink - m)
        pv = jnp.dot(e.astype(BF16), vals_of(kh), preferred_element_type=F32)
        o = pv / denom
        o_heads = jnp.concatenate([o[g * tq:(g + 1) * tq] for g in range(GQA)], axis=-1)
        o_ref[:, kh * GQA * HEAD_DIM:(kh + 1) * GQA * HEAD_DIM] = o_heads.astype(BF16)


def _attn_prompt_kernel(q_ref, kp_ref, kc_ref, vp_ref, vc_ref, sink_ref, o_ref):
    tq = ATT_ROWS
    chunk_shift = CHUNK.bit_length() - 1
    chunks_per_block = tq // CHUNK
    row = lax.broadcasted_iota(jnp.int32, (GQA * tq, 2 * tq), 0)
    col = lax.broadcasted_iota(jnp.int32, (GQA * tq, 2 * tq), 1)
    q_chunk = (row & (tq - 1)) >> chunk_shift
    k_chunk = (col >> chunk_shift) - chunks_per_block
    oldest = jnp.where(pl.program_id(1) == 0, 0, -chunks_per_block)
    valid = (k_chunk <= q_chunk) & (k_chunk >= jnp.maximum(q_chunk - 2, oldest))
    _attend(
        q_ref,
        lambda kh: jnp.concatenate([kp_ref[kh], kc_ref[kh]], axis=0),
        lambda kh: jnp.concatenate([vp_ref[kh], vc_ref[kh]], axis=0),
        sink_ref, o_ref, tq=tq, valid=valid)


def _attn_prompt(q_heads, k_heads, v_heads, sinks, *, batch, seq, m):
    tq = ATT_ROWS
    blocks = seq // tq
    cur = lambda b, c: (0, b * blocks + c, 0)
    prev = lambda b, c: (0, b * blocks + jnp.maximum(c - 1, 0), 0)
    kv_block = (N_KV_HEADS, tq, HEAD_DIM)
    vmem = 2 * N_HEADS * tq * 128 * 2 + 8 * N_KV_HEADS * tq * 128 * 2 + 2 * tq * D_MODEL * 2 \
        + 8 * GQA * tq * 2 * tq * 4
    return pl.pallas_call(
        _attn_prompt_kernel,
        grid=(batch, blocks),
        in_specs=[
            pl.BlockSpec((N_HEADS, tq, HEAD_DIM), cur),
            pl.BlockSpec(kv_block, prev),
            pl.BlockSpec(kv_block, cur),
            pl.BlockSpec(kv_block, prev),
            pl.BlockSpec(kv_block, cur),
            pl.BlockSpec(memory_space=pltpu.SMEM),
        ],
        out_specs=pl.BlockSpec((tq, D_MODEL), lambda b, c: (b * blocks + c, 0)),
        out_shape=jax.ShapeDtypeStruct((m, D_MODEL), BF16),
        compiler_params=_params(vmem, 2),
        name="attn_prompt",
    )(q_heads, k_heads, k_heads, v_heads, v_heads, sinks)


def _attn_sample_kernel(q_ref, ck_ref, cv_ref, kn_ref, vn_ref, sink_ref, o_prev_ref, o_ref, *, tq):
    del o_prev_ref
    _attend(
        q_ref,
        lambda kh: jnp.concatenate([ck_ref[0, kh], kn_ref[kh]], axis=0),
        lambda kh: jnp.concatenate([cv_ref[0, kh], vn_ref[kh]], axis=0),
        sink_ref, o_ref, tq=tq, valid=None)


def _attn_sample(q_heads, cache_k, cache_v, k_heads, v_heads, sinks, o_prev, *, streams, tq, row0):
    block0 = row0 // tq
    new = lambda b: (0, block0 + b, 0)
    cache_spec = pl.BlockSpec((1, N_KV_HEADS, WINDOW, HEAD_DIM), lambda b: (b, 0, 0, 0))
    new_spec = pl.BlockSpec((N_KV_HEADS, tq, HEAD_DIM), new)
    vmem = 2 * N_HEADS * tq * 128 * 2 + 8 * N_KV_HEADS * WINDOW * 128 * 2 + 2 * tq * D_MODEL * 2 \
        + 8 * GQA * tq * 256 * 4
    return pl.pallas_call(
        functools.partial(_attn_sample_kernel, tq=tq),
        grid=(streams,),
        in_specs=[
            pl.BlockSpec((N_HEADS, tq, HEAD_DIM), new),
            cache_spec, cache_spec, new_spec, new_spec,
            pl.BlockSpec(memory_space=pltpu.SMEM),
            pl.BlockSpec(memory_space=pl.ANY),
        ],
        out_specs=pl.BlockSpec((tq, D_MODEL), lambda b: (block0 + b, 0)),
        out_shape=jax.ShapeDtypeStruct(o_prev.shape, BF16),
        input_output_aliases={6: 0},
        compiler_params=_params(vmem, 1),
        name="attn_sample",
    )(q_heads, cache_k, cache_v, k_heads, v_heads, sinks, o_prev)


def _ffn(h, gain, w_gate, w_up, w_down, name):
    hidden = _ffn_up(h, gain, w_gate, w_up, bm=ROW_TILE, bn=256)
    return _mm_res(hidden, w_down, h, bm=ROW_TILE, bn=256, name=name)


def kernel(x_prompt, x_sample, cache_k, cache_v, norm_a, w_sgu_in, sgu_ln_g, sgu_ln_b, w_sgu_s,
           b_sgu_s, w_sgu_out, norm_kv, w_kv, k_norm, norm_b, w_q, q_norm, sinks, w_o,
           norm_ffn, w_ffn_gate, w_ffn_up, w_ffn_down):
    batch, seq, d = x_prompt.shape
    streams, dec_seq, _ = x_sample.shape
    m_prompt = batch * seq
    m_sample = streams * dec_seq
    m = m_prompt + m_sample
    assert d == D_MODEL and m % ROW_TILE == 0 and seq % ATT_ROWS == 0
    assert norm_a.shape[0] == 1 and norm_b.shape[0] == 1 and norm_ffn.shape[0] == 2

    row = lambda g: g.reshape(1, -1).astype(F32)
    h0 = jnp.concatenate([x_prompt.reshape(m_prompt, d), x_sample.reshape(m_sample, d)], axis=0)

    uv = _sgu_in(h0, row(norm_a[0]), w_sgu_in[0].astype(BF16), bm=ROW_TILE, bn=512)

    pos_chunk = jnp.arange(SGU_CHUNK) // CHUNK
    ws = jnp.where((pos_chunk[:, None] >= pos_chunk[None, :])[None], w_sgu_s[0], 0.0).astype(BF16)
    bs_t = b_sgu_s[0].T.astype(F32)
    ln_g, ln_b = row(sgu_ln_g[0]), row(sgu_ln_b[0])
    chunks_per_block = 2
    rows_per_block = SGU_CHUNK * chunks_per_block
    (a,) = _sgu_mix(uv, ln_g, ln_b, ws, bs_t, None, chunk=SGU_CHUNK, n_chunks=chunks_per_block,
                    row_block0=0, n_blocks=m_prompt // rows_per_block, emit_vn=False)
    a, v_rows = _sgu_mix(uv, ln_g, ln_b, ws[:, :dec_seq, :dec_seq], bs_t[:dec_seq], a,
                         chunk=dec_seq, n_chunks=streams, row_block0=m_prompt // m_sample,
                         n_blocks=1, emit_vn=True)
    h = _mm_res(a, w_sgu_out[0].astype(BF16), h0, bm=ROW_TILE, bn=512, name="sgu_out")
    h = _ffn(h, row(norm_ffn[0]), w_ffn_gate[0].astype(BF16), w_ffn_up[0].astype(BF16),
             w_ffn_down[0].astype(BF16), "ffn_down_0")

    k_rows, v_rows_kv, k_heads, v_heads = _kv_proj(
        h, row(norm_kv), w_kv.astype(BF16), row(k_norm), bm=ROW_TILE)
    q_heads = _q_proj(h, row(norm_b[0]), w_q[0].astype(BF16), row(q_norm[0]), bm=ROW_TILE, bn=512)
    sink = sinks[0].astype(F32)
    o = _attn_prompt(q_heads, k_heads, v_heads, sink, batch=batch, seq=seq, m=m)
    head_major = lambda c: c.transpose(0, 2, 1, 3).astype(BF16)
    o = _attn_sample(q_heads, head_major(cache_k), head_major(cache_v), k_heads, v_heads, sink, o,
                     streams=streams, tq=dec_seq, row0=m_prompt)
    h = _mm_res(o, w_o[0].astype(BF16), h, bm=ROW_TILE, bn=512, name="attn_out")
    h = _ffn(h, row(norm_ffn[1]), w_ffn_gate[1].astype(BF16), w_ffn_up[1].astype(BF16),
             w_ffn_down[1].astype(BF16), "ffn_down_1")

    y_prompt = h[:m_prompt].reshape(batch, seq, d)
    y_sample = h[m_prompt:].reshape(streams, dec_seq, d)
    kv4 = lambda t, b, s: t.reshape(b, s, N_KV_HEADS, HEAD_DIM)
    k_p = kv4(k_rows[:m_prompt], batch, seq)[:, -WINDOW:]
    v_p = kv4(v_rows_kv[:m_prompt], batch, seq)[:, -WINDOW:]
    k_s = kv4(k_rows[m_prompt:], streams, dec_seq)
    v_s = kv4(v_rows_kv[m_prompt:], streams, dec_seq)
    new_sgu_v = v_rows.reshape(1, streams, dec_seq, d)
    return (y_prompt, y_sample, k_p, v_p, k_s, v_s, new_sgu_v)
```

```python
import functools

import jax
import jax.numpy as jnp
from jax import lax
from jax.experimental import pallas as pl
from jax.experimental.pallas import tpu as pltpu

F32 = jnp.float32
BF16 = jnp.bfloat16

D_MODEL = 4096
SGU_CHUNK = 128
SGU_GROUPS = 8
SGU_GROUP_WIDTH = D_MODEL // SGU_GROUPS
CHUNK = 64
HEAD_DIM = 64
N_HEADS = 64
N_KV_HEADS = 8
GQA = N_HEADS // N_KV_HEADS
KV_WIDTH = N_KV_HEADS * HEAD_DIM
WINDOW = 128
NEG = -1e30
RMS_EPS = 1e-6
LN_EPS = 1e-5

V7X_VMEM_BYTES = 64 * 1024 * 1024
VMEM_LIMIT_CAP = V7X_VMEM_BYTES - 6 * 1024 * 1024
LANES = 128
BF16_SUBLANES = 16
HEADS_PER_VREG = LANES // HEAD_DIM
PAIRS = GQA // HEADS_PER_VREG

ROW_TILE = 768
KV_ROW_TILE = ROW_TILE // 2
ATT_ROWS = 128
RMS_STAT_ROWS = 64
RMS_SCALE_ROWS = 32


def _params(vmem_bytes, n_axes):
    limit = min(int(vmem_bytes) + 8 * 1024 * 1024, VMEM_LIMIT_CAP)
    return pltpu.CompilerParams(
        dimension_semantics=("arbitrary",) * n_axes, vmem_limit_bytes=limit)


def _layer_spec(k, bn, layer, col_of):
    return pl.BlockSpec((None, k, bn), lambda *idx: (layer, 0, col_of(*idx)))


def _rms_rows_to_bf16(x_ref, g_ref, xn_ref, r_ref):
    bm, k = x_ref.shape

    def stats(i, carry):
        rows = pl.ds(pl.multiple_of(i * RMS_STAT_ROWS, RMS_STAT_ROWS), RMS_STAT_ROWS)
        x = x_ref[rows, :]
        ms = jnp.mean(x * x, axis=-1, keepdims=True)
        r_ref[rows, :] = jnp.broadcast_to(lax.rsqrt(ms + RMS_EPS), (RMS_STAT_ROWS, LANES))
        return carry

    lax.fori_loop(0, bm // RMS_STAT_ROWS, stats, 0)
    g = g_ref[...]

    def scale(i, carry):
        rows = pl.ds(pl.multiple_of(i * RMS_SCALE_ROWS, RMS_SCALE_ROWS), RMS_SCALE_ROWS)
        r = pltpu.repeat(r_ref[rows, :], k // LANES, axis=1)
        xn_ref[rows, :] = (x_ref[rows, :] * r * g).astype(BF16)
        return carry

    lax.fori_loop(0, bm // RMS_SCALE_ROWS, scale, 0)


def _rms_scratch(bm, k):
    return [pltpu.VMEM((bm, k), BF16), pltpu.VMEM((bm, LANES), F32)]


def _head_rms(y, seg_ref, gain_row):
    y2 = y * y
    hi = y2.astype(BF16)
    lo = (y2 - hi.astype(F32)).astype(BF16)
    seg = seg_ref[...]
    ssq = (jnp.dot(hi, seg, preferred_element_type=F32)
           + jnp.dot(lo, seg, preferred_element_type=F32))
    return y * lax.rsqrt(ssq * (1.0 / HEAD_DIM) + RMS_EPS) * gain_row


def _sgu_in_kernel(x_ref, g_ref, w_ref, o_ref, xn_ref, r_ref):
    @pl.when(pl.program_id(1) == 0)
    def _():
        _rms_rows_to_bf16(x_ref, g_ref, xn_ref, r_ref)

    y = jnp.dot(xn_ref[...], w_ref[...], preferred_element_type=F32)
    o_ref[...] = 0.5 * y * (1.0 + lax.erf(y * (0.5 ** 0.5)))


def _sgu_in(x, gain, w, layer, *, bm, bn):
    m, k = x.shape
    n = w.shape[2]
    vmem = 2 * bm * k * 4 + bm * k * 2 + 2 * k * bn * 2 + 3 * bm * bn * 4
    return pl.pallas_call(
        _sgu_in_kernel,
        grid=(m // bm, n // bn),
        in_specs=[
            pl.BlockSpec((bm, k), lambda i, j: (i, 0)),
            pl.BlockSpec((1, k), lambda i, j: (0, 0)),
            _layer_spec(k, bn, layer, lambda i, j: j),
        ],
        out_specs=pl.BlockSpec((bm, bn), lambda i, j: (i, j)),
        out_shape=jax.ShapeDtypeStruct((m, n), F32),
        scratch_shapes=_rms_scratch(bm, k),
        compiler_params=_params(vmem, 2),
        name="sgu_in",
    )(x, gain, w)


def _sgu_mix_kernel(u_ref, v_ref, lng_ref, lnb_ref, ws_ref, bs_ref, *out_refs, chunk, n_chunks):
    a_ref = out_refs[0]
    vn_ref = out_refs[1] if len(out_refs) > 1 else None
    ln_g = lng_ref[...]
    ln_b = lnb_ref[...]
    for c in range(n_chunks):
        rows = slice(c * chunk, (c + 1) * chunk)
        v = v_ref[rows, :]
        mu = jnp.mean(v, axis=-1, keepdims=True)
        vc = v - mu
        var = jnp.mean(vc * vc, axis=-1, keepdims=True)
        vn = vc * lax.rsqrt(var + LN_EPS) * ln_g + ln_b
        if vn_ref is not None:
            vn_ref[rows, :] = vn
        vnb = vn.astype(BF16)
        for g in range(SGU_GROUPS):
            cols = slice(g * SGU_GROUP_WIDTH, (g + 1) * SGU_GROUP_WIDTH)
            mixed = jnp.dot(ws_ref[g], vnb[:, cols], preferred_element_type=F32)
            mixed = mixed + bs_ref[:, g:g + 1]
            a_ref[rows, cols] = (u_ref[rows, cols] * mixed).astype(BF16)


def _skip_ref(kernel, index):
    def wrapped(*refs):
        return kernel(*refs[:index], *refs[index + 1:])
    return wrapped


def _sgu_mix(uv, ln_g, ln_b, ws, bs_t, a_prev, *, chunk, n_chunks, row_block0, n_blocks, emit_vn):
    m = uv.shape[0]
    d = D_MODEL
    rows = chunk * n_chunks
    kernel = functools.partial(_sgu_mix_kernel, chunk=chunk, n_chunks=n_chunks)
    in_specs = [
        pl.BlockSpec((rows, d), lambda i: (row_block0 + i, 0)),
        pl.BlockSpec((rows, d), lambda i: (row_block0 + i, 1)),
        pl.BlockSpec((1, d), lambda i: (0, 0)),
        pl.BlockSpec((1, d), lambda i: (0, 0)),
        pl.BlockSpec((SGU_GROUPS, chunk, chunk), lambda i: (0, 0, 0)),
        pl.BlockSpec((chunk, SGU_GROUPS), lambda i: (0, 0)),
    ]
    args = [uv, uv, ln_g, ln_b, ws, bs_t]
    out_shape = [jax.ShapeDtypeStruct((m, d), BF16)]
    out_specs = [pl.BlockSpec((rows, d), lambda i: (row_block0 + i, 0))]
    aliases = {}
    if a_prev is not None:
        in_specs.append(pl.BlockSpec(memory_space=pl.ANY))
        args.append(a_prev)
        aliases = {len(args) - 1: 0}
        kernel = _skip_ref(kernel, len(args) - 1)
    if emit_vn:
        out_shape.append(jax.ShapeDtypeStruct((rows * n_blocks, d), F32))
        out_specs.append(pl.BlockSpec((rows, d), lambda i: (i, 0)))
    vmem = 2 * 2 * rows * d * 4 + 2 * rows * d * 2 + 2 * rows * d * 4 + 6 * chunk * d * 4
    return pl.pallas_call(
        kernel,
        grid=(n_blocks,),
        in_specs=in_specs,
        out_specs=out_specs,
        out_shape=out_shape,
        input_output_aliases=aliases,
        compiler_params=_params(vmem, 1),
        name="sgu_mix_%d" % chunk,
    )(*args)


def _mm_res_kernel(a_ref, w_ref, res_ref, o_ref):
    o_ref[...] = res_ref[...] + jnp.dot(a_ref[...], w_ref[...], preferred_element_type=F32)


def _mm_res(a, w, layer, res, *, bm, bn, name):
    m, k = a.shape
    n = w.shape[2]
    vmem = 2 * bm * k * 2 + 2 * k * bn * 2 + 5 * bm * bn * 4
    return pl.pallas_call(
        _mm_res_kernel,
        grid=(m // bm, n // bn),
        in_specs=[
            pl.BlockSpec((bm, k), lambda i, j: (i, 0)),
            _layer_spec(k, bn, layer, lambda i, j: j),
            pl.BlockSpec((bm, bn), lambda i, j: (i, j)),
        ],
        out_specs=pl.BlockSpec((bm, bn), lambda i, j: (i, j)),
        out_shape=jax.ShapeDtypeStruct((m, n), F32),
        compiler_params=_params(vmem, 2),
        name=name,
    )(a, w, res)


def _ffn_up_kernel(x_ref, g_ref, wg_ref, wu_ref, o_ref, xn_ref, r_ref):
    @pl.when(pl.program_id(1) == 0)
    def _():
        _rms_rows_to_bf16(x_ref, g_ref, xn_ref, r_ref)

    xn = xn_ref[...]
    gate = jnp.dot(xn, wg_ref[...], preferred_element_type=F32)
    up = jnp.dot(xn, wu_ref[...], preferred_element_type=F32)
    o_ref[...] = (gate * jax.nn.sigmoid(gate) * up).astype(BF16)


def _ffn_up(x, gain, w_gate, w_up, layer, *, bm, bn):
    m, k = x.shape
    n = w_gate.shape[2]
    vmem = 2 * bm * k * 4 + bm * k * 2 + 4 * k * bn * 2 + 2 * bm * bn * 2 + 3 * bm * bn * 4
    return pl.pallas_call(
        _ffn_up_kernel,
        grid=(m // bm, n // bn),
        in_specs=[
            pl.BlockSpec((bm, k), lambda i, j: (i, 0)),
            pl.BlockSpec((1, k), lambda i, j: (0, 0)),
            _layer_spec(k, bn, layer, lambda i, j: j),
            _layer_spec(k, bn, layer, lambda i, j: j),
        ],
        out_specs=pl.BlockSpec((bm, bn), lambda i, j: (i, j)),
        out_shape=jax.ShapeDtypeStruct((m, n), BF16),
        scratch_shapes=_rms_scratch(bm, k),
        compiler_params=_params(vmem, 2),
        name="ffn_up",
    )(x, gain, w_gate, w_up)


def _q_proj_kernel(x_ref, g_ref, w_ref, seg_ref, qg_ref, q_ref, xn_ref, r_ref):
    @pl.when(pl.program_id(1) == 0)
    def _():
        _rms_rows_to_bf16(x_ref, g_ref, xn_ref, r_ref)

    y = jnp.dot(xn_ref[...], w_ref[...], preferred_element_type=F32)
    q_ref[...] = (_head_rms(y, seg_ref, qg_ref[...]) * HEAD_DIM ** -0.5).astype(BF16)


def _q_proj(x, gain, w, layer, seg, q_gain, *, bm, bn):
    m, k = x.shape
    n = w.shape[2]
    vmem = 2 * bm * k * 4 + bm * k * 2 + 2 * k * bn * 2 + 8 * bm * bn * 4 + 2 * bn * bn * 2
    return pl.pallas_call(
        _q_proj_kernel,
        grid=(m // bm, n // bn),
        in_specs=[
            pl.BlockSpec((bm, k), lambda i, j: (i, 0)),
            pl.BlockSpec((1, k), lambda i, j: (0, 0)),
            _layer_spec(k, bn, layer, lambda i, j: j),
            pl.BlockSpec((bn, bn), lambda i, j: (0, 0)),
            pl.BlockSpec((1, bn), lambda i, j: (0, 0)),
        ],
        out_specs=pl.BlockSpec((bm, bn), lambda i, j: (i, j)),
        out_shape=jax.ShapeDtypeStruct((m, n), BF16),
        scratch_shapes=_rms_scratch(bm, k),
        compiler_params=_params(vmem, 2),
        name="q_proj",
    )(x, gain, w, seg, q_gain)


def _kv_proj_kernel(x_ref, g_ref, w_ref, seg_ref, kg_ref, kf_ref, vf_ref, kz_ref, vz_ref,
                    xn_ref, r_ref):
    _rms_rows_to_bf16(x_ref, g_ref, xn_ref, r_ref)
    y = jnp.dot(xn_ref[...], w_ref[...], preferred_element_type=F32)
    kn = _head_rms(y[:, :KV_WIDTH], seg_ref, kg_ref[...])
    v = y[:, KV_WIDTH:]
    kf_ref[...] = kn
    vf_ref[...] = v
    low = lax.broadcasted_iota(jnp.int32, (y.shape[0], LANES), 1) < HEAD_DIM
    for src, dst in ((kn, kz_ref), (v, vz_ref)):
        for p in range(N_KV_HEADS // HEADS_PER_VREG):
            both = src[:, p * LANES:(p + 1) * LANES]
            swapped = pltpu.roll(both, HEAD_DIM, axis=1)
            dst[2 * p, 0] = jnp.where(low, both, 0.0).astype(BF16)
            dst[2 * p, 1] = jnp.where(low, 0.0, swapped).astype(BF16)
            dst[2 * p + 1, 0] = jnp.where(low, swapped, 0.0).astype(BF16)
            dst[2 * p + 1, 1] = jnp.where(low, 0.0, both).astype(BF16)


def _kv_proj(x, gain, w, seg, k_gain, *, bm):
    m, k = x.shape
    n = w.shape[1]
    vmem = (2 * bm * k * 4 + bm * k * 2 + 2 * k * n * 2 + 6 * bm * n * 4
            + 2 * 2 * N_KV_HEADS * 2 * bm * LANES * 2)
    z_spec = pl.BlockSpec((N_KV_HEADS, 2, bm, LANES), lambda i: (0, 0, i, 0))
    row_spec = pl.BlockSpec((bm, KV_WIDTH), lambda i: (i, 0))
    z_shape = jax.ShapeDtypeStruct((N_KV_HEADS, 2, m, LANES), BF16)
    return pl.pallas_call(
        _kv_proj_kernel,
        grid=(m // bm,),
        in_specs=[
            pl.BlockSpec((bm, k), lambda i: (i, 0)),
            pl.BlockSpec((1, k), lambda i: (0, 0)),
            pl.BlockSpec((k, n), lambda i: (0, 0)),
            pl.BlockSpec((KV_WIDTH, KV_WIDTH), lambda i: (0, 0)),
            pl.BlockSpec((1, KV_WIDTH), lambda i: (0, 0)),
        ],
        out_specs=[row_spec, row_spec, z_spec, z_spec],
        out_shape=[
            jax.ShapeDtypeStruct((m, KV_WIDTH), F32),
            jax.ShapeDtypeStruct((m, KV_WIDTH), F32),
            z_shape, z_shape,
        ],
        scratch_shapes=_rms_scratch(bm, k),
        compiler_params=_params(vmem, 1),
        name="kv_proj",
    )(x, gain, w, seg, k_gain)


def _attend(q_ref, kk_of, vv_of, sink_ref, o_ref, *, tq, n_keys, bias_t):
    rows = PAIRS * tq
    for kh in range(N_KV_HEADS):
        base = kh * GQA * HEAD_DIM
        q4 = jnp.concatenate(
            [q_ref[:, base + p * LANES:base + (p + 1) * LANES] for p in range(PAIRS)], axis=0)
        s_t = lax.dot_general(kk_of(kh), q4, (((1,), (1,)), ((), ())),
                              preferred_element_type=F32)
        if bias_t is not None:
            s_t = s_t + bias_t
        exps, denoms = [], []
        for parity in range(HEADS_PER_VREG):
            sink = jnp.concatenate(
                [jnp.full((1, tq), sink_ref[kh * GQA + HEADS_PER_VREG * p + parity], F32)
                 for p in range(PAIRS)], axis=1)
            half = s_t[parity * n_keys:(parity + 1) * n_keys]
            mx = jnp.maximum(jnp.max(half, axis=0, keepdims=True), sink)
            e = jnp.exp(half - mx)
            denom = jnp.sum(e, axis=0, keepdims=True) + jnp.exp(sink - mx)
            denoms.append(jnp.broadcast_to(denom, (HEAD_DIM, rows)))
            exps.append(e.astype(BF16))
        o_t = lax.dot_general(vv_of(kh), jnp.concatenate(exps, axis=0),
                              (((0,), (0,)), ((), ())), preferred_element_type=F32)
        o = (o_t / jnp.concatenate(denoms, axis=0)).T
        for p in range(PAIRS):
            o_ref[:, base + p * LANES:base + (p + 1) * LANES] = (
                o[p * tq:(p + 1) * tq].astype(BF16))


def _attn_prompt_kernel(q_ref, kp_ref, kc_ref, vp_ref, vc_ref, sink_ref, o_ref):
    tq = ATT_ROWS
    n_keys = 2 * tq
    chunk_shift = CHUNK.bit_length() - 1
    chunks_per_block = tq // CHUNK
    key = lax.broadcasted_iota(jnp.int32, (2 * n_keys, PAIRS * tq), 0)
    query = lax.broadcasted_iota(jnp.int32, (2 * n_keys, PAIRS * tq), 1)
    q_chunk = (query & (tq - 1)) >> chunk_shift
    k_chunk = ((key & (n_keys - 1)) >> chunk_shift) - chunks_per_block
    oldest = jnp.where(pl.program_id(1) == 0, 0, -chunks_per_block)
    valid = (k_chunk <= q_chunk) & (k_chunk >= jnp.maximum(q_chunk - 2, oldest))
    bias = jnp.where(valid, 0.0, NEG)
    stack = lambda prev, cur, kh: jnp.concatenate(
        [prev[kh, 0], cur[kh, 0], prev[kh, 1], cur[kh, 1]], axis=0)
    _attend(q_ref,
            functools.partial(stack, kp_ref, kc_ref),
            functools.partial(stack, vp_ref, vc_ref),
            sink_ref, o_ref, tq=tq, n_keys=n_keys, bias_t=bias)


def _attn_prompt(q, kz, vz, sinks, *, batch, seq):
    m = q.shape[0]
    tq = ATT_ROWS
    blocks = seq // tq
    cur = lambda b, c: (0, 0, b * blocks + c, 0)
    prev = lambda b, c: (0, 0, b * blocks + jnp.maximum(c - 1, 0), 0)
    z_block = (N_KV_HEADS, 2, tq, LANES)
    vmem = 4 * tq * D_MODEL * 2 + 8 * N_KV_HEADS * 2 * tq * LANES * 2 \
        + 10 * PAIRS * tq * 4 * tq * 4
    return pl.pallas_call(
        _attn_prompt_kernel,
        grid=(batch, blocks),
        in_specs=[
            pl.BlockSpec((tq, D_MODEL), lambda b, c: (b * blocks + c, 0)),
            pl.BlockSpec(z_block, prev),
            pl.BlockSpec(z_block, cur),
            pl.BlockSpec(z_block, prev),
            pl.BlockSpec(z_block, cur),
            pl.BlockSpec(memory_space=pltpu.SMEM),
        ],
        out_specs=pl.BlockSpec((tq, D_MODEL), lambda b, c: (b * blocks + c, 0)),
        out_shape=jax.ShapeDtypeStruct((m, D_MODEL), BF16),
        compiler_params=_params(vmem, 2),
        name="attn_prompt",
    )(q, kz, kz, vz, vz, sinks)


def _attn_sample_kernel(q_ref, ck_ref, cv_ref, kn_ref, vn_ref, sink_ref, o_ref, *, tq):
    stack = lambda cache, new, kh: jnp.concatenate(
        [cache[0, kh, 0], new[kh, 0], cache[0, kh, 1], new[kh, 1]], axis=0)
    _attend(q_ref,
            functools.partial(stack, ck_ref, kn_ref),
            functools.partial(stack, cv_ref, vn_ref),
            sink_ref, o_ref, tq=tq, n_keys=WINDOW + tq, bias_t=None)


def _attn_sample(q, cache_kz, cache_vz, kz, vz, sinks, o_prev, *, streams, tq, row0):
    block0 = row0 // tq
    cache_spec = pl.BlockSpec((1, N_KV_HEADS, 2, WINDOW, LANES), lambda b: (b, 0, 0, 0, 0))
    new_spec = pl.BlockSpec((N_KV_HEADS, 2, tq, LANES), lambda b: (0, 0, block0 + b, 0))
    row_spec = pl.BlockSpec((tq, D_MODEL), lambda b: (block0 + b, 0))
    vmem = 4 * tq * D_MODEL * 2 + 4 * N_KV_HEADS * 2 * (WINDOW + tq) * LANES * 2 \
        + 10 * PAIRS * tq * 2 * (WINDOW + tq) * 4
    return pl.pallas_call(
        _skip_ref(functools.partial(_attn_sample_kernel, tq=tq), 6),
        grid=(streams,),
        in_specs=[
            row_spec, cache_spec, cache_spec, new_spec, new_spec,
            pl.BlockSpec(memory_space=pltpu.SMEM),
            pl.BlockSpec(memory_space=pl.ANY),
        ],
        out_specs=row_spec,
        out_shape=jax.ShapeDtypeStruct(o_prev.shape, BF16),
        input_output_aliases={6: 0},
        compiler_params=_params(vmem, 1),
        name="attn_sample",
    )(q, cache_kz, cache_vz, kz, vz, sinks, o_prev)


def _ffn(h, gain, w_gate, w_up, w_down, layer, name):
    hidden = _ffn_up(h, gain, w_gate, w_up, layer, bm=ROW_TILE, bn=256)
    return _mm_res(hidden, w_down, layer, h, bm=ROW_TILE, bn=256, name=name)


def _lane_pair_copies(cache):
    c = cache.transpose(0, 2, 1, 3).astype(BF16)
    z = jnp.zeros_like(c)
    return jnp.stack([jnp.concatenate([c, z], axis=-1), jnp.concatenate([z, c], axis=-1)], axis=2)


def kernel(x_prompt, x_sample, cache_k, cache_v, norm_a, w_sgu_in, sgu_ln_g, sgu_ln_b, w_sgu_s,
           b_sgu_s, w_sgu_out, norm_kv, w_kv, k_norm, norm_b, w_q, q_norm, sinks, w_o,
           norm_ffn, w_ffn_gate, w_ffn_up, w_ffn_down):
    batch, seq, d = x_prompt.shape
    streams, dec_seq, _ = x_sample.shape
    m_prompt = batch * seq
    m_sample = streams * dec_seq
    m = m_prompt + m_sample
    assert d == D_MODEL and m % ROW_TILE == 0 and seq % ATT_ROWS == 0
    assert norm_a.shape[0] == 1 and norm_b.shape[0] == 1 and norm_ffn.shape[0] == 2

    row = lambda g: g.reshape(1, -1).astype(F32)
    bf = lambda w: w.astype(BF16)
    h0 = jnp.concatenate([x_prompt.reshape(m_prompt, d), x_sample.reshape(m_sample, d)], axis=0)
    w_gate, w_up, w_down = bf(w_ffn_gate), bf(w_ffn_up), bf(w_ffn_down)

    uv = _sgu_in(h0, row(norm_a[0]), bf(w_sgu_in), 0, bm=ROW_TILE, bn=512)

    pos_chunk = jnp.arange(SGU_CHUNK) // CHUNK
    ws = bf(jnp.where((pos_chunk[:, None] >= pos_chunk[None, :])[None], w_sgu_s[0], 0.0))
    bs_t = b_sgu_s[0].T.astype(F32)
    ln_g, ln_b = row(sgu_ln_g[0]), row(sgu_ln_b[0])
    chunks_per_block = 2
    rows_per_block = SGU_CHUNK * chunks_per_block
    (a,) = _sgu_mix(uv, ln_g, ln_b, ws, bs_t, None, chunk=SGU_CHUNK, n_chunks=chunks_per_block,
                    row_block0=0, n_blocks=m_prompt // rows_per_block, emit_vn=False)
    a, v_rows = _sgu_mix(uv, ln_g, ln_b, ws[:, :dec_seq, :dec_seq], bs_t[:dec_seq], a,
                         chunk=dec_seq, n_chunks=streams, row_block0=m_prompt // m_sample,
                         n_blocks=1, emit_vn=True)
    h = _mm_res(a, bf(w_sgu_out), 0, h0, bm=ROW_TILE, bn=512, name="sgu_out")
    h = _ffn(h, row(norm_ffn[0]), w_gate, w_up, w_down, 0, "ffn_down_0")

    lane_head = jnp.arange(KV_WIDTH) // HEAD_DIM
    seg = (lane_head[:, None] == lane_head[None, :]).astype(BF16)
    k_rows, v_rows_kv, kz, vz = _kv_proj(
        h, row(norm_kv), bf(w_kv), seg, row(jnp.tile(k_norm, N_KV_HEADS)), bm=KV_ROW_TILE)
    q = _q_proj(h, row(norm_b[0]), bf(w_q), 0, seg, row(jnp.tile(q_norm[0], GQA)),
                bm=ROW_TILE, bn=KV_WIDTH)
    sink = sinks[0].astype(F32)
    o = _attn_prompt(q, kz, vz, sink, batch=batch, seq=seq)
    o = _attn_sample(q, _lane_pair_copies(cache_k), _lane_pair_copies(cache_v), kz, vz, sink, o,
                     streams=streams, tq=dec_seq, row0=m_prompt)
    h = _mm_res(o, bf(w_o), 0, h, bm=ROW_TILE, bn=512, name="attn_out")
    h = _ffn(h, row(norm_ffn[1]), w_gate, w_up, w_down, 1, "ffn_down_1")

    y_prompt = h[:m_prompt].reshape(batch, seq, d)
    y_sample = h[m_prompt:].reshape(streams, dec_seq, d)
    kv4 = lambda t, b, s: t.reshape(b, s, N_KV_HEADS, HEAD_DIM)
    k_p = kv4(k_rows[:m_prompt], batch, seq)[:, -WINDOW:]
    v_p = kv4(v_rows_kv[:m_prompt], batch, seq)[:, -WINDOW:]
    k_s = kv4(k_rows[m_prompt:], streams, dec_seq)
    v_s = kv4(v_rows_kv[m_prompt:], streams, dec_seq)
    new_sgu_v = v_rows.reshape(1, streams, dec_seq, d)
    return (y_prompt, y_sample, k_p, v_p, k_s, v_s, new_sgu_v)
```

```python
import functools

import jax
import jax.numpy as jnp
from jax import lax
from jax.experimental import pallas as pl
from jax.experimental.pallas import tpu as pltpu

F32 = jnp.float32
BF16 = jnp.bfloat16

D_MODEL = 4096
SGU_CHUNK = 128
SGU_GROUPS = 8
SGU_GROUP_WIDTH = D_MODEL // SGU_GROUPS
CHUNK = 64
HEAD_DIM = 64
N_HEADS = 64
N_KV_HEADS = 8
GQA = N_HEADS // N_KV_HEADS
KV_WIDTH = N_KV_HEADS * HEAD_DIM
WINDOW = 128
NEG = -1e30
RMS_EPS = 1e-6
LN_EPS = 1e-5

V7X_VMEM_BYTES = 64 * 1024 * 1024
VMEM_LIMIT_CAP = V7X_VMEM_BYTES - 6 * 1024 * 1024
LANES = 128
MXU_WIDTH = 256
HEADS_PER_VREG = LANES // HEAD_DIM
PAIRS = GQA // HEADS_PER_VREG

WIDE_TILE = 1408
DEEP_TILE = 768
KV_TILE = WIDE_TILE // 2
NORM_ROWS = 256
ATT_ROWS = 128
RMS_STAT_ROWS = 64
RMS_SCALE_ROWS = 32


def _params(vmem_bytes, n_axes):
    limit = min(int(vmem_bytes) + 8 * 1024 * 1024, VMEM_LIMIT_CAP)
    return pltpu.CompilerParams(
        dimension_semantics=("arbitrary",) * n_axes, vmem_limit_bytes=limit)


def _layer_spec(k, bn, layer, col_of):
    return pl.BlockSpec((None, k, bn), lambda *idx: (layer, 0, col_of(*idx)))


def _skip_ref(kernel, index):
    def wrapped(*refs):
        return kernel(*refs[:index], *refs[index + 1:])
    return wrapped


def _rms_rows(x_ref, gain_refs, xn_refs, r_ref):
    rows, k = x_ref.shape

    def stats(i, carry):
        sl = pl.ds(pl.multiple_of(i * RMS_STAT_ROWS, RMS_STAT_ROWS), RMS_STAT_ROWS)
        x = x_ref[sl, :]
        ms = jnp.mean(x * x, axis=-1, keepdims=True)
        r_ref[sl, :] = jnp.broadcast_to(lax.rsqrt(ms + RMS_EPS), (RMS_STAT_ROWS, LANES))
        return carry

    lax.fori_loop(0, rows // RMS_STAT_ROWS, stats, 0)

    for g_ref, xn_ref in zip(gain_refs, xn_refs):
        g = g_ref[...]

        def scale(i, carry, g=g, xn_ref=xn_ref):
            sl = pl.ds(pl.multiple_of(i * RMS_SCALE_ROWS, RMS_SCALE_ROWS), RMS_SCALE_ROWS)
            r = jnp.tile(r_ref[sl, :], (1, k // LANES))
            xn_ref[sl, :] = (x_ref[sl, :] * r * g).astype(BF16)
            return carry

        lax.fori_loop(0, rows // RMS_SCALE_ROWS, scale, 0)


def _rms_kernel(x_ref, *refs, n_gains):
    gain_refs, xn_refs, r_ref = refs[:n_gains], refs[n_gains:2 * n_gains], refs[2 * n_gains]
    _rms_rows(x_ref, gain_refs, xn_refs, r_ref)


def _rms(x, gains):
    m, k = x.shape
    n = len(gains)
    row_spec = pl.BlockSpec((NORM_ROWS, k), lambda i: (i, 0))
    gain_spec = pl.BlockSpec((1, k), lambda i: (0, 0))
    vmem = 2 * NORM_ROWS * k * 4 + 2 * n * NORM_ROWS * k * 2 + 4 * RMS_STAT_ROWS * k * 4
    return pl.pallas_call(
        functools.partial(_rms_kernel, n_gains=n),
        grid=(m // NORM_ROWS,),
        in_specs=[row_spec] + [gain_spec] * n,
        out_specs=[row_spec] * n,
        out_shape=[jax.ShapeDtypeStruct((m, k), BF16)] * n,
        scratch_shapes=[pltpu.VMEM((NORM_ROWS, LANES), F32)],
        compiler_params=_params(vmem, 1),
        name="rms_%d" % n,
    )(x, *gains)


def _rms_stack_kernel(xp_ref, xs_ref, g_ref, xn_ref, h_ref, r_ref, *, prompt_blocks):
    def from_ref(x_ref):
        _rms_rows(x_ref, (g_ref,), (xn_ref,), r_ref)
        h_ref[...] = x_ref[...]

    pl.when(pl.program_id(0) < prompt_blocks)(lambda: from_ref(xp_ref))
    pl.when(pl.program_id(0) >= prompt_blocks)(lambda: from_ref(xs_ref))


def _rms_stack(x_prompt, x_sample, gain):
    mp, k = x_prompt.shape
    ms = x_sample.shape[0]
    assert mp % NORM_ROWS == 0 and ms == NORM_ROWS
    prompt_blocks = mp // NORM_ROWS
    m = mp + ms
    row_spec = pl.BlockSpec((NORM_ROWS, k), lambda i: (i, 0))
    vmem = 4 * NORM_ROWS * k * 4 + 2 * NORM_ROWS * k * (2 + 4) + 4 * RMS_STAT_ROWS * k * 4
    return pl.pallas_call(
        functools.partial(_rms_stack_kernel, prompt_blocks=prompt_blocks),
        grid=(m // NORM_ROWS,),
        in_specs=[
            pl.BlockSpec((NORM_ROWS, k), lambda i: (jnp.minimum(i, prompt_blocks - 1), 0)),
            pl.BlockSpec((NORM_ROWS, k), lambda i: (0, 0)),
            pl.BlockSpec((1, k), lambda i: (0, 0)),
        ],
        out_specs=[row_spec, row_spec],
        out_shape=[jax.ShapeDtypeStruct((m, k), BF16), jax.ShapeDtypeStruct((m, k), F32)],
        scratch_shapes=[pltpu.VMEM((NORM_ROWS, LANES), F32)],
        compiler_params=_params(vmem, 1),
        name="rms_stack",
    )(x_prompt, x_sample, gain)


def _head_rms(y, seg_ref, gain_row):
    y2 = y * y
    hi = y2.astype(BF16)
    lo = (y2 - hi.astype(F32)).astype(BF16)
    seg = seg_ref[...]
    w = seg.shape[0]
    ssq = jnp.concatenate(
        [jnp.dot(hi[:, c:c + w], seg, preferred_element_type=F32)
         + jnp.dot(lo[:, c:c + w], seg, preferred_element_type=F32)
         for c in range(0, y.shape[1], w)], axis=-1)
    return y * lax.rsqrt(ssq * (1.0 / HEAD_DIM) + RMS_EPS) * gain_row


def _sgu_in_kernel(x_ref, w_ref, o_ref):
    y = jnp.dot(x_ref[...], w_ref[...].astype(BF16), preferred_element_type=F32)
    o_ref[...] = 0.5 * y * (1.0 + lax.erf(y * (0.5 ** 0.5)))


def _sgu_in(xn, w, layer, *, bm, bn):
    m, k = xn.shape
    n = w.shape[2]
    wb = w.dtype.itemsize
    vmem = 2 * bm * k * 2 + k * bn * (2 * wb + 2) + 5 * bm * bn * 4
    return pl.pallas_call(
        _sgu_in_kernel,
        grid=(m // bm, n // bn),
        in_specs=[
            pl.BlockSpec((bm, k), lambda i, j: (i, 0)),
            _layer_spec(k, bn, layer, lambda i, j: j),
        ],
        out_specs=pl.BlockSpec((bm, bn), lambda i, j: (i, j)),
        out_shape=jax.ShapeDtypeStruct((m, n), F32),
        compiler_params=_params(vmem, 2),
        name="sgu_in",
    )(xn, w)


def _sgu_mix_kernel(u_ref, v_ref, lng_ref, lnb_ref, ws_ref, bs_ref, *out_refs, chunk, n_chunks):
    a_ref = out_refs[0]
    vn_ref = out_refs[1] if len(out_refs) > 1 else None
    ln_g = lng_ref[...]
    ln_b = lnb_ref[...]
    for c in range(n_chunks):
        rows = slice(c * chunk, (c + 1) * chunk)
        v = v_ref[rows, :]
        mu = jnp.mean(v, axis=-1, keepdims=True)
        vc = v - mu
        var = jnp.mean(vc * vc, axis=-1, keepdims=True)
        vn = vc * lax.rsqrt(var + LN_EPS) * ln_g + ln_b
        if vn_ref is not None:
            vn_ref[rows, :] = vn
        vnb = vn.astype(BF16)
        for g in range(SGU_GROUPS):
            cols = slice(g * SGU_GROUP_WIDTH, (g + 1) * SGU_GROUP_WIDTH)
            mixed = jnp.dot(ws_ref[g], vnb[:, cols], preferred_element_type=F32)
            mixed = mixed + bs_ref[:, g:g + 1]
            a_ref[rows, cols] = (u_ref[rows, cols] * mixed).astype(BF16)


def _sgu_mix(uv, ln_g, ln_b, ws, bs_t, a_prev, *, chunk, n_chunks, row_block0, n_blocks, emit_vn):
    m = uv.shape[0]
    d = D_MODEL
    rows = chunk * n_chunks
    kernel = functools.partial(_sgu_mix_kernel, chunk=chunk, n_chunks=n_chunks)
    in_specs = [
        pl.BlockSpec((rows, d), lambda i: (row_block0 + i, 0)),
        pl.BlockSpec((rows, d), lambda i: (row_block0 + i, 1)),
        pl.BlockSpec((1, d), lambda i: (0, 0)),
        pl.BlockSpec((1, d), lambda i: (0, 0)),
        pl.BlockSpec((SGU_GROUPS, chunk, chunk), lambda i: (0, 0, 0)),
        pl.BlockSpec((chunk, SGU_GROUPS), lambda i: (0, 0)),
    ]
    args = [uv, uv, ln_g, ln_b, ws, bs_t]
    out_shape = [jax.ShapeDtypeStruct((m, d), BF16)]
    out_specs = [pl.BlockSpec((rows, d), lambda i: (row_block0 + i, 0))]
    aliases = {}
    if a_prev is not None:
        in_specs.append(pl.BlockSpec(memory_space=pl.ANY))
        args.append(a_prev)
        aliases = {len(args) - 1: 0}
        kernel = _skip_ref(kernel, len(args) - 1)
    if emit_vn:
        out_shape.append(jax.ShapeDtypeStruct((rows * n_blocks, d), F32))
        out_specs.append(pl.BlockSpec((rows, d), lambda i: (i, 0)))
    vmem = 2 * 2 * rows * d * 4 + 2 * rows * d * 2 + 2 * rows * d * 4 + 6 * chunk * d * 4
    return pl.pallas_call(
        kernel,
        grid=(n_blocks,),
        in_specs=in_specs,
        out_specs=out_specs,
        out_shape=out_shape,
        input_output_aliases=aliases,
        compiler_params=_params(vmem, 1),
        name="sgu_mix_%d" % chunk,
    )(*args)


def _mm_res_kernel(a_ref, w_ref, res_ref, o_ref):
    y = jnp.dot(a_ref[...], w_ref[...].astype(BF16), preferred_element_type=F32)
    o_ref[...] = res_ref[...] + y


def _mm_res_split_kernel(a_ref, w_ref, res_ref, top_ref, tail_ref, *, tail_rows):
    y = res_ref[...] + jnp.dot(a_ref[...], w_ref[...].astype(BF16), preferred_element_type=F32)
    top_ref[...] = y

    @pl.when(pl.program_id(0) == pl.num_programs(0) - 1)
    def _():
        tail_ref[...] = y[y.shape[0] - tail_rows:]


def _mm_res(a, w, layer, res, *, bm, bn, name, split_rows=None):
    m, k = a.shape
    n = w.shape[2]
    wb = w.dtype.itemsize
    vmem = 2 * bm * k * 2 + k * bn * (2 * wb + 2) + 6 * bm * bn * 4
    tile = pl.BlockSpec((bm, bn), lambda i, j: (i, j))
    if split_rows is None:
        kernel, out_specs, out_shape = _mm_res_kernel, tile, jax.ShapeDtypeStruct((m, n), F32)
    else:
        tail_rows = m - split_rows
        last = m // bm - 1
        assert last * bm < split_rows and tail_rows <= bm
        kernel = functools.partial(_mm_res_split_kernel, tail_rows=tail_rows)
        tail = pl.BlockSpec((tail_rows, bn), lambda i, j: (0, jnp.where(i == last, j, 0)))
        out_specs = [tile, tail]
        out_shape = [jax.ShapeDtypeStruct((split_rows, n), F32),
                     jax.ShapeDtypeStruct((tail_rows, n), F32)]
    return pl.pallas_call(
        kernel,
        grid=(m // bm, n // bn),
        in_specs=[
            pl.BlockSpec((bm, k), lambda i, j: (i, 0)),
            _layer_spec(k, bn, layer, lambda i, j: j),
            tile,
        ],
        out_specs=out_specs,
        out_shape=out_shape,
        compiler_params=_params(vmem, 2),
        name=name,
    )(a, w, res)


def _ffn_up_kernel(x_ref, wg_ref, wu_ref, o_ref):
    xn = x_ref[...]
    gate = jnp.dot(xn, wg_ref[...].astype(BF16), preferred_element_type=F32)
    up = jnp.dot(xn, wu_ref[...].astype(BF16), preferred_element_type=F32)
    o_ref[...] = (gate * jax.nn.sigmoid(gate) * up).astype(BF16)


def _ffn_up(xn, w_gate, w_up, layer, *, bm, bn):
    m, k = xn.shape
    n = w_gate.shape[2]
    wb = w_gate.dtype.itemsize
    vmem = 2 * bm * k * 2 + 2 * k * bn * (2 * wb + 2) + 2 * bm * bn * 2 + 5 * bm * bn * 4
    return pl.pallas_call(
        _ffn_up_kernel,
        grid=(m // bm, n // bn),
        in_specs=[
            pl.BlockSpec((bm, k), lambda i, j: (i, 0)),
            _layer_spec(k, bn, layer, lambda i, j: j),
            _layer_spec(k, bn, layer, lambda i, j: j),
        ],
        out_specs=pl.BlockSpec((bm, bn), lambda i, j: (i, j)),
        out_shape=jax.ShapeDtypeStruct((m, n), BF16),
        compiler_params=_params(vmem, 2),
        name="ffn_up",
    )(xn, w_gate, w_up)


def _q_proj_kernel(x_ref, w_ref, seg_ref, qg_ref, q_ref):
    y = jnp.dot(x_ref[...], w_ref[...].astype(BF16), preferred_element_type=F32)
    q_ref[...] = (_head_rms(y, seg_ref, qg_ref[...]) * HEAD_DIM ** -0.5).astype(BF16)


def _q_proj(xn, w, layer, seg, q_gain, *, bm, bn):
    m, k = xn.shape
    n = w.shape[2]
    wb = w.dtype.itemsize
    vmem = 2 * bm * k * 2 + k * bn * (2 * wb + 2) + 10 * bm * bn * 4
    return pl.pallas_call(
        _q_proj_kernel,
        grid=(m // bm, n // bn),
        in_specs=[
            pl.BlockSpec((bm, k), lambda i, j: (i, 0)),
            _layer_spec(k, bn, layer, lambda i, j: j),
            pl.BlockSpec(seg.shape, lambda i, j: (0, 0)),
            pl.BlockSpec((1, bn), lambda i, j: (0, 0)),
        ],
        out_specs=pl.BlockSpec((bm, bn), lambda i, j: (i, j)),
        out_shape=jax.ShapeDtypeStruct((m, n), BF16),
        compiler_params=_params(vmem, 2),
        name="q_proj",
    )(xn, w, seg, q_gain)


def _kv_proj_kernel(x_ref, w_ref, seg_ref, kg_ref, kf_ref, vf_ref, kz_ref, vz_ref):
    y = jnp.dot(x_ref[...], w_ref[...], preferred_element_type=F32)
    kn = _head_rms(y[:, :KV_WIDTH], seg_ref, kg_ref[...])
    v = y[:, KV_WIDTH:]
    kf_ref[...] = kn
    vf_ref[...] = v
    low = lax.broadcasted_iota(jnp.int32, (y.shape[0], LANES), 1) < HEAD_DIM
    for src, dst in ((kn, kz_ref), (v, vz_ref)):
        for p in range(N_KV_HEADS // HEADS_PER_VREG):
            both = src[:, p * LANES:(p + 1) * LANES]
            swapped = pltpu.roll(both, HEAD_DIM, axis=1)
            dst[2 * p, 0] = jnp.where(low, both, 0.0).astype(BF16)
            dst[2 * p, 1] = jnp.where(low, 0.0, swapped).astype(BF16)
            dst[2 * p + 1, 0] = jnp.where(low, swapped, 0.0).astype(BF16)
            dst[2 * p + 1, 1] = jnp.where(low, 0.0, both).astype(BF16)


def _kv_proj(xn, w, seg, k_gain, *, bm):
    m, k = xn.shape
    n = w.shape[1]
    vmem = (2 * bm * k * 2 + 2 * k * n * 2 + 8 * bm * n * 4
            + 2 * 2 * N_KV_HEADS * 2 * bm * LANES * 2)
    z_spec = pl.BlockSpec((N_KV_HEADS, 2, bm, LANES), lambda i: (0, 0, i, 0))
    row_spec = pl.BlockSpec((bm, KV_WIDTH), lambda i: (i, 0))
    z_shape = jax.ShapeDtypeStruct((N_KV_HEADS, 2, m, LANES), BF16)
    return pl.pallas_call(
        _kv_proj_kernel,
        grid=(m // bm,),
        in_specs=[
            pl.BlockSpec((bm, k), lambda i: (i, 0)),
            pl.BlockSpec((k, n), lambda i: (0, 0)),
            pl.BlockSpec(seg.shape, lambda i: (0, 0)),
            pl.BlockSpec((1, KV_WIDTH), lambda i: (0, 0)),
        ],
        out_specs=[row_spec, row_spec, z_spec, z_spec],
        out_shape=[
            jax.ShapeDtypeStruct((m, KV_WIDTH), F32),
            jax.ShapeDtypeStruct((m, KV_WIDTH), F32),
            z_shape, z_shape,
        ],
        compiler_params=_params(vmem, 1),
        name="kv_proj",
    )(xn, w, seg, k_gain)


def _attend(q_ref, kk_of, vv_of, sink_ref, o_ref, *, tq, n_keys, bias_t):
    rows = PAIRS * tq
    for kh in range(N_KV_HEADS):
        base = kh * GQA * HEAD_DIM
        q4 = jnp.concatenate(
            [q_ref[:, base + p * LANES:base + (p + 1) * LANES] for p in range(PAIRS)], axis=0)
        s_t = lax.dot_general(kk_of(kh), q4, (((1,), (1,)), ((), ())),
                              preferred_element_type=F32)
        if bias_t is not None:
            s_t = s_t + bias_t
        exps, denoms = [], []
        for parity in range(HEADS_PER_VREG):
            sink = jnp.concatenate(
                [jnp.full((1, tq), sink_ref[kh * GQA + HEADS_PER_VREG * p + parity], F32)
                 for p in range(PAIRS)], axis=1)
            half = s_t[parity * n_keys:(parity + 1) * n_keys]
            mx = jnp.maximum(jnp.max(half, axis=0, keepdims=True), sink)
            e = jnp.exp(half - mx)
            denom = jnp.sum(e, axis=0, keepdims=True) + jnp.exp(sink - mx)
            denoms.append(jnp.broadcast_to(denom, (HEAD_DIM, rows)))
            exps.append(e.astype(BF16))
        o_t = lax.dot_general(vv_of(kh), jnp.concatenate(exps, axis=0),
                              (((0,), (0,)), ((), ())), preferred_element_type=F32)
        o = (o_t / jnp.concatenate(denoms, axis=0)).T
        for p in range(PAIRS):
            o_ref[:, base + p * LANES:base + (p + 1) * LANES] = (
                o[p * tq:(p + 1) * tq].astype(BF16))


def _attn_prompt_kernel(q_ref, kp_ref, kc_ref, vp_ref, vc_ref, sink_ref, o_ref):
    tq = ATT_ROWS
    n_keys = 2 * tq
    chunk_shift = CHUNK.bit_length() - 1
    chunks_per_block = tq // CHUNK
    key = lax.broadcasted_iota(jnp.int32, (2 * n_keys, PAIRS * tq), 0)
    query = lax.broadcasted_iota(jnp.int32, (2 * n_keys, PAIRS * tq), 1)
    q_chunk = (query & (tq - 1)) >> chunk_shift
    k_chunk = ((key & (n_keys - 1)) >> chunk_shift) - chunks_per_block
    oldest = jnp.where(pl.program_id(1) == 0, 0, -chunks_per_block)
    valid = (k_chunk <= q_chunk) & (k_chunk >= jnp.maximum(q_chunk - 2, oldest))
    bias = jnp.where(valid, 0.0, NEG)
    stack = lambda prev, cur, kh: jnp.concatenate(
        [prev[kh, 0], cur[kh, 0], prev[kh, 1], cur[kh, 1]], axis=0)
    _attend(q_ref,
            functools.partial(stack, kp_ref, kc_ref),
            functools.partial(stack, vp_ref, vc_ref),
            sink_ref, o_ref, tq=tq, n_keys=n_keys, bias_t=bias)


def _attn_prompt(q, kz, vz, sinks, *, batch, seq):
    m = q.shape[0]
    tq = ATT_ROWS
    blocks = seq // tq
    cur = lambda b, c: (0, 0, b * blocks + c, 0)
    prev = lambda b, c: (0, 0, b * blocks + jnp.maximum(c - 1, 0), 0)
    z_block = (N_KV_HEADS, 2, tq, LANES)
    vmem = 4 * tq * D_MODEL * 2 + 8 * N_KV_HEADS * 2 * tq * LANES * 2 \
        + 10 * PAIRS * tq * 4 * tq * 4
    return pl.pallas_call(
        _attn_prompt_kernel,
        grid=(batch, blocks),
        in_specs=[
            pl.BlockSpec((tq, D_MODEL), lambda b, c: (b * blocks + c, 0)),
            pl.BlockSpec(z_block, prev),
            pl.BlockSpec(z_block, cur),
            pl.BlockSpec(z_block, prev),
            pl.BlockSpec(z_block, cur),
            pl.BlockSpec(memory_space=pltpu.SMEM),
        ],
        out_specs=pl.BlockSpec((tq, D_MODEL), lambda b, c: (b * blocks + c, 0)),
        out_shape=jax.ShapeDtypeStruct((m, D_MODEL), BF16),
        compiler_params=_params(vmem, 2),
        name="attn_prompt",
    )(q, kz, kz, vz, vz, sinks)


def _attn_sample_kernel(q_ref, ck_ref, cv_ref, kn_ref, vn_ref, sink_ref, o_ref, *, tq):
    stack = lambda cache, new, kh: jnp.concatenate(
        [cache[0, kh, 0], new[kh, 0], cache[0, kh, 1], new[kh, 1]], axis=0)
    _attend(q_ref,
            functools.partial(stack, ck_ref, kn_ref),
            functools.partial(stack, cv_ref, vn_ref),
            sink_ref, o_ref, tq=tq, n_keys=WINDOW + tq, bias_t=None)


def _attn_sample(q, cache_kz, cache_vz, kz, vz, sinks, o_prev, *, streams, tq, row0):
    block0 = row0 // tq
    cache_spec = pl.BlockSpec((1, N_KV_HEADS, 2, WINDOW, LANES), lambda b: (b, 0, 0, 0, 0))
    new_spec = pl.BlockSpec((N_KV_HEADS, 2, tq, LANES), lambda b: (0, 0, block0 + b, 0))
    row_spec = pl.BlockSpec((tq, D_MODEL), lambda b: (block0 + b, 0))
    vmem = 4 * tq * D_MODEL * 2 + 4 * N_KV_HEADS * 2 * (WINDOW + tq) * LANES * 2 \
        + 10 * PAIRS * tq * 2 * (WINDOW + tq) * 4
    return pl.pallas_call(
        _skip_ref(functools.partial(_attn_sample_kernel, tq=tq), 6),
        grid=(streams,),
        in_specs=[
            row_spec, cache_spec, cache_spec, new_spec, new_spec,
            pl.BlockSpec(memory_space=pltpu.SMEM),
            pl.BlockSpec(memory_space=pl.ANY),
        ],
        out_specs=row_spec,
        out_shape=jax.ShapeDtypeStruct(o_prev.shape, BF16),
        input_output_aliases={6: 0},
        compiler_params=_params(vmem, 1),
        name="attn_sample",
    )(q, cache_kz, cache_vz, kz, vz, sinks, o_prev)


def _lane_pair_copies(cache):
    c = cache.transpose(0, 2, 1, 3).astype(BF16)
    z = jnp.zeros_like(c)
    return jnp.stack([jnp.concatenate([c, z], axis=-1), jnp.concatenate([z, c], axis=-1)], axis=2)


def kernel(x_prompt, x_sample, cache_k, cache_v, norm_a, w_sgu_in, sgu_ln_g, sgu_ln_b, w_sgu_s,
           b_sgu_s, w_sgu_out, norm_kv, w_kv, k_norm, norm_b, w_q, q_norm, sinks, w_o,
           norm_ffn, w_ffn_gate, w_ffn_up, w_ffn_down):
    batch, seq, d = x_prompt.shape
    streams, dec_seq, _ = x_sample.shape
    m_prompt = batch * seq
    m_sample = streams * dec_seq
    m = m_prompt + m_sample
    assert d == D_MODEL and seq % ATT_ROWS == 0
    assert m % WIDE_TILE == 0 and m % DEEP_TILE == 0 and m % KV_TILE == 0
    assert norm_a.shape[0] == 1 and norm_b.shape[0] == 1 and norm_ffn.shape[0] == 2

    row = lambda g: g.reshape(1, -1).astype(F32)
    w_down = w_ffn_down.astype(BF16)

    def ffn(h, layer, name, split_rows=None):
        (xn,) = _rms(h, [row(norm_ffn[layer])])
        hidden = _ffn_up(xn, w_ffn_gate, w_ffn_up, layer, bm=WIDE_TILE, bn=MXU_WIDTH)
        return _mm_res(hidden, w_down, layer, h, bm=DEEP_TILE, bn=MXU_WIDTH, name=name,
                       split_rows=split_rows)

    xn, h0 = _rms_stack(x_prompt.reshape(m_prompt, d), x_sample.reshape(m_sample, d),
                        row(norm_a[0]))
    uv = _sgu_in(xn, w_sgu_in, 0, bm=WIDE_TILE, bn=2 * MXU_WIDTH)

    pos_chunk = jnp.arange(SGU_CHUNK) // CHUNK
    ws = jnp.where((pos_chunk[:, None] >= pos_chunk[None, :])[None], w_sgu_s[0], 0.0).astype(BF16)
    bs_t = b_sgu_s[0].T.astype(F32)
    ln_g, ln_b = row(sgu_ln_g[0]), row(sgu_ln_b[0])
    chunks_per_block = 2
    rows_per_block = SGU_CHUNK * chunks_per_block
    (a,) = _sgu_mix(uv, ln_g, ln_b, ws, bs_t, None, chunk=SGU_CHUNK, n_chunks=chunks_per_block,
                    row_block0=0, n_blocks=m_prompt // rows_per_block, emit_vn=False)
    a, v_rows = _sgu_mix(uv, ln_g, ln_b, ws[:, :dec_seq, :dec_seq], bs_t[:dec_seq], a,
                         chunk=dec_seq, n_chunks=streams, row_block0=m_prompt // m_sample,
                         n_blocks=1, emit_vn=True)
    h = _mm_res(a, w_sgu_out, 0, h0, bm=WIDE_TILE, bn=MXU_WIDTH, name="sgu_out")
    h = ffn(h, 0, "ffn_down_0")

    lane_head = jnp.arange(MXU_WIDTH) // HEAD_DIM
    seg = (lane_head[:, None] == lane_head[None, :]).astype(BF16)
    xn_kv, xn_q = _rms(h, [row(norm_kv), row(norm_b[0])])
    k_rows, v_rows_kv, kz, vz = _kv_proj(
        xn_kv, w_kv.astype(BF16), seg, row(jnp.tile(k_norm, N_KV_HEADS)), bm=KV_TILE)
    q = _q_proj(xn_q, w_q, 0, seg, row(jnp.tile(q_norm[0], MXU_WIDTH // HEAD_DIM)),
                bm=WIDE_TILE, bn=MXU_WIDTH)
    sink = sinks[0].astype(F32)
    o = _attn_prompt(q, kz, vz, sink, batch=batch, seq=seq)
    o = _attn_sample(q, _lane_pair_copies(cache_k), _lane_pair_copies(cache_v), kz, vz, sink, o,
                     streams=streams, tq=dec_seq, row0=m_prompt)
    h = _mm_res(o, w_o, 0, h, bm=WIDE_TILE, bn=MXU_WIDTH, name="attn_out")
    y_prompt, y_sample = ffn(h, 1, "ffn_down_1", split_rows=m_prompt)

    kv4 = lambda t, b, s: t.reshape(b, s, N_KV_HEADS, HEAD_DIM)
    k_p = kv4(k_rows[:m_prompt], batch, seq)[:, -WINDOW:]
    v_p = kv4(v_rows_kv[:m_prompt], batch, seq)[:, -WINDOW:]
    k_s = kv4(k_rows[m_prompt:], streams, dec_seq)
    v_s = kv4(v_rows_kv[m_prompt:], streams, dec_seq)
    return (y_prompt.reshape(batch, seq, d), y_sample.reshape(streams, dec_seq, d),
            k_p, v_p, k_s, v_s, v_rows.reshape(1, streams, dec_seq, d))
```

```python
import functools

import jax
import jax.numpy as jnp
from jax import lax
from jax.experimental import pallas as pl
from jax.experimental.pallas import tpu as pltpu

F32 = jnp.float32
BF16 = jnp.bfloat16

D_MODEL = 4096
SGU_CHUNK = 128
SGU_GROUPS = 8
SGU_GROUP_WIDTH = D_MODEL // SGU_GROUPS
CHUNK = 64
HEAD_DIM = 64
N_HEADS = 64
N_KV_HEADS = 8
GQA = N_HEADS // N_KV_HEADS
KV_WIDTH = N_KV_HEADS * HEAD_DIM
WINDOW = 128
NEG = -1e30
RMS_EPS = 1e-6
LN_EPS = 1e-5

V7X_VMEM_BYTES = 64 * 1024 * 1024
VMEM_LIMIT_CAP = V7X_VMEM_BYTES - 6 * 1024 * 1024
LANES = 128
MXU_WIDTH = 256
HEADS_PER_VREG = LANES // HEAD_DIM
PAIRS = GQA // HEADS_PER_VREG

WIDE_TILE = 1408
KV_TILE = WIDE_TILE // 2
NORM_ROWS = 256
ATT_ROWS = 128
RMS_STAT_ROWS = 64
RMS_SCALE_ROWS = 32


def _params(vmem_bytes, n_axes):
    limit = min(int(vmem_bytes) + 8 * 1024 * 1024, VMEM_LIMIT_CAP)
    return pltpu.CompilerParams(
        dimension_semantics=("arbitrary",) * n_axes, vmem_limit_bytes=limit)


def _layer_spec(k, bn, layer, col_of):
    return pl.BlockSpec((None, k, bn), lambda *idx: (layer, 0, col_of(*idx)))


def _skip_ref(kernel, index):
    def wrapped(*refs):
        return kernel(*refs[:index], *refs[index + 1:])
    return wrapped


def _rms_rows(x_ref, gain_refs, xn_refs, r_ref):
    rows, k = x_ref.shape

    def stats(i, carry):
        sl = pl.ds(pl.multiple_of(i * RMS_STAT_ROWS, RMS_STAT_ROWS), RMS_STAT_ROWS)
        x = x_ref[sl, :]
        ms = jnp.mean(x * x, axis=-1, keepdims=True)
        r_ref[sl, :] = jnp.broadcast_to(lax.rsqrt(ms + RMS_EPS), (RMS_STAT_ROWS, LANES))
        return carry

    lax.fori_loop(0, rows // RMS_STAT_ROWS, stats, 0)

    for g_ref, xn_ref in zip(gain_refs, xn_refs):
        g = g_ref[...]

        def scale(i, carry, g=g, xn_ref=xn_ref):
            sl = pl.ds(pl.multiple_of(i * RMS_SCALE_ROWS, RMS_SCALE_ROWS), RMS_SCALE_ROWS)
            r = jnp.tile(r_ref[sl, :], (1, k // LANES))
            xn_ref[sl, :] = (x_ref[sl, :] * r * g).astype(BF16)
            return carry

        lax.fori_loop(0, rows // RMS_SCALE_ROWS, scale, 0)


def _rms_kernel(x_ref, *refs, n_gains):
    gain_refs, xn_refs, r_ref = refs[:n_gains], refs[n_gains:2 * n_gains], refs[2 * n_gains]
    _rms_rows(x_ref, gain_refs, xn_refs, r_ref)


def _rms(x, gains):
    m, k = x.shape
    n = len(gains)
    row_spec = pl.BlockSpec((NORM_ROWS, k), lambda i: (i, 0))
    gain_spec = pl.BlockSpec((1, k), lambda i: (0, 0))
    vmem = 2 * NORM_ROWS * k * 4 + 2 * n * NORM_ROWS * k * 2 + 4 * RMS_STAT_ROWS * k * 4
    return pl.pallas_call(
        functools.partial(_rms_kernel, n_gains=n),
        grid=(m // NORM_ROWS,),
        in_specs=[row_spec] + [gain_spec] * n,
        out_specs=[row_spec] * n,
        out_shape=[jax.ShapeDtypeStruct((m, k), BF16)] * n,
        scratch_shapes=[pltpu.VMEM((NORM_ROWS, LANES), F32)],
        compiler_params=_params(vmem, 1),
        name="rms_%d" % n,
    )(x, *gains)


def _rms_stack_kernel(xp_ref, xs_ref, g_ref, xn_ref, r_ref, *, prompt_blocks):
    from_ref = lambda x_ref: _rms_rows(x_ref, (g_ref,), (xn_ref,), r_ref)
    pl.when(pl.program_id(0) < prompt_blocks)(lambda: from_ref(xp_ref))
    pl.when(pl.program_id(0) >= prompt_blocks)(lambda: from_ref(xs_ref))


def _rms_stack(x_prompt, x_sample, gain):
    mp, k = x_prompt.shape
    ms = x_sample.shape[0]
    assert mp % NORM_ROWS == 0 and ms == NORM_ROWS
    prompt_blocks = mp // NORM_ROWS
    m = mp + ms
    vmem = 4 * NORM_ROWS * k * 4 + 2 * NORM_ROWS * k * 2 + 4 * RMS_STAT_ROWS * k * 4
    return pl.pallas_call(
        functools.partial(_rms_stack_kernel, prompt_blocks=prompt_blocks),
        grid=(m // NORM_ROWS,),
        in_specs=[
            pl.BlockSpec((NORM_ROWS, k), lambda i: (jnp.minimum(i, prompt_blocks - 1), 0)),
            pl.BlockSpec((NORM_ROWS, k), lambda i: (0, 0)),
            pl.BlockSpec((1, k), lambda i: (0, 0)),
        ],
        out_specs=pl.BlockSpec((NORM_ROWS, k), lambda i: (i, 0)),
        out_shape=jax.ShapeDtypeStruct((m, k), BF16),
        scratch_shapes=[pltpu.VMEM((NORM_ROWS, LANES), F32)],
        compiler_params=_params(vmem, 1),
        name="rms_stack",
    )(x_prompt, x_sample, gain)


def _head_rms(y, seg_ref, gain_row):
    y2 = y * y
    hi = y2.astype(BF16)
    lo = (y2 - hi.astype(F32)).astype(BF16)
    seg = seg_ref[...]
    w = seg.shape[0]
    ssq = jnp.concatenate(
        [jnp.dot(hi[:, c:c + w], seg, preferred_element_type=F32)
         + jnp.dot(lo[:, c:c + w], seg, preferred_element_type=F32)
         for c in range(0, y.shape[1], w)], axis=-1)
    return y * lax.rsqrt(ssq * (1.0 / HEAD_DIM) + RMS_EPS) * gain_row


def _sgu_in_kernel(x_ref, w_ref, o_ref):
    y = jnp.dot(x_ref[...], w_ref[...].astype(BF16), preferred_element_type=F32)
    o_ref[...] = 0.5 * y * (1.0 + lax.erf(y * (0.5 ** 0.5)))


def _sgu_in(xn, w, layer, *, bm, bn):
    m, k = xn.shape
    n = w.shape[2]
    wb = w.dtype.itemsize
    vmem = 2 * bm * k * 2 + k * bn * (2 * wb + 2) + 5 * bm * bn * 4
    return pl.pallas_call(
        _sgu_in_kernel,
        grid=(m // bm, n // bn),
        in_specs=[
            pl.BlockSpec((bm, k), lambda i, j: (i, 0)),
            _layer_spec(k, bn, layer, lambda i, j: j),
        ],
        out_specs=pl.BlockSpec((bm, bn), lambda i, j: (i, j)),
        out_shape=jax.ShapeDtypeStruct((m, n), F32),
        compiler_params=_params(vmem, 2),
        name="sgu_in",
    )(xn, w)


def _sgu_mix_kernel(u_ref, v_ref, lng_ref, lnb_ref, ws_ref, bs_ref, *out_refs, chunk, n_chunks):
    a_ref = out_refs[0]
    vn_ref = out_refs[1] if len(out_refs) > 1 else None
    ln_g = lng_ref[...]
    ln_b = lnb_ref[...]
    for c in range(n_chunks):
        rows = slice(c * chunk, (c + 1) * chunk)
        v = v_ref[rows, :]
        mu = jnp.mean(v, axis=-1, keepdims=True)
        vc = v - mu
        var = jnp.mean(vc * vc, axis=-1, keepdims=True)
        vn = vc * lax.rsqrt(var + LN_EPS) * ln_g + ln_b
        if vn_ref is not None:
            vn_ref[rows, :] = vn
        vnb = vn.astype(BF16)
        for g in range(SGU_GROUPS):
            cols = slice(g * SGU_GROUP_WIDTH, (g + 1) * SGU_GROUP_WIDTH)
            mixed = jnp.dot(ws_ref[g], vnb[:, cols], preferred_element_type=F32)
            mixed = mixed + bs_ref[:, g:g + 1]
            a_ref[rows, cols] = (u_ref[rows, cols] * mixed).astype(BF16)


def _sgu_mix(uv, ln_g, ln_b, ws, bs_t, a_prev, *, chunk, n_chunks, row_block0, n_blocks, emit_vn):
    m = uv.shape[0]
    d = D_MODEL
    rows = chunk * n_chunks
    kernel = functools.partial(_sgu_mix_kernel, chunk=chunk, n_chunks=n_chunks)
    in_specs = [
        pl.BlockSpec((rows, d), lambda i: (row_block0 + i, 0)),
        pl.BlockSpec((rows, d), lambda i: (row_block0 + i, 1)),
        pl.BlockSpec((1, d), lambda i: (0, 0)),
        pl.BlockSpec((1, d), lambda i: (0, 0)),
        pl.BlockSpec((SGU_GROUPS, chunk, chunk), lambda i: (0, 0, 0)),
        pl.BlockSpec((chunk, SGU_GROUPS), lambda i: (0, 0)),
    ]
    args = [uv, uv, ln_g, ln_b, ws, bs_t]
    out_shape = [jax.ShapeDtypeStruct((m, d), BF16)]
    out_specs = [pl.BlockSpec((rows, d), lambda i: (row_block0 + i, 0))]
    aliases = {}
    if a_prev is not None:
        in_specs.append(pl.BlockSpec(memory_space=pl.ANY))
        args.append(a_prev)
        aliases = {len(args) - 1: 0}
        kernel = _skip_ref(kernel, len(args) - 1)
    if emit_vn:
        out_shape.append(jax.ShapeDtypeStruct((rows * n_blocks, d), F32))
        out_specs.append(pl.BlockSpec((rows, d), lambda i: (i, 0)))
    vmem = 2 * 2 * rows * d * 4 + 2 * rows * d * 2 + 2 * rows * d * 4 + 6 * chunk * d * 4
    return pl.pallas_call(
        kernel,
        grid=(n_blocks,),
        in_specs=in_specs,
        out_specs=out_specs,
        out_shape=out_shape,
        input_output_aliases=aliases,
        compiler_params=_params(vmem, 1),
        name="sgu_mix_%d" % chunk,
    )(*args)


def _mm_res_kernel(a_ref, w_ref, res_ref, o_ref):
    y = jnp.dot(a_ref[...], w_ref[...].astype(BF16), preferred_element_type=F32)
    o_ref[...] = res_ref[...] + y


def _mm_res2_kernel(a_ref, w_ref, top_ref, tail_ref, o_ref, *, top_rows):
    y = jnp.dot(a_ref[...], w_ref[...].astype(BF16), preferred_element_type=F32)
    is_last = pl.program_id(0) == pl.num_programs(0) - 1
    o_ref[:top_rows, :] = top_ref[:top_rows, :] + y[:top_rows]

    @pl.when(jnp.logical_not(is_last))
    def _():
        o_ref[top_rows:, :] = top_ref[top_rows:, :] + y[top_rows:]

    @pl.when(is_last)
    def _():
        o_ref[top_rows:, :] = tail_ref[...] + y[top_rows:]


def _mm_res(a, w, layer, res, *, bm, bn, name):
    m, k = a.shape
    n = w.shape[2]
    wb = w.dtype.itemsize
    vmem = 2 * bm * k * 2 + k * bn * (2 * wb + 2) + 6 * bm * bn * 4
    tile = pl.BlockSpec((bm, bn), lambda i, j: (i, j))
    if isinstance(res, tuple):
        top, tail = res
        tail_rows = tail.shape[0]
        top_rows = bm - tail_rows
        assert top.shape[0] + tail_rows == m and top_rows > 0 and top_rows % 8 == 0
        kernel = functools.partial(_mm_res2_kernel, top_rows=top_rows)
        res_specs = [tile, pl.BlockSpec((tail_rows, bn), lambda i, j: (0, j))]
        res_args = [top, tail]
        vmem += 2 * tail_rows * bn * 4
    else:
        kernel, res_specs, res_args = _mm_res_kernel, [tile], [res]
    return pl.pallas_call(
        kernel,
        grid=(m // bm, n // bn),
        in_specs=[
            pl.BlockSpec((bm, k), lambda i, j: (i, 0)),
            _layer_spec(k, bn, layer, lambda i, j: j),
        ] + res_specs,
        out_specs=tile,
        out_shape=jax.ShapeDtypeStruct((m, n), F32),
        compiler_params=_params(vmem, 2),
        name=name,
    )(a, w, *res_args)


K_SPLIT = 2


def _ffn_down_kernel(x_ref, w_ref, h_ref, o_ref, *group_refs, tail_rows):
    y = h_ref[...] + jnp.dot(x_ref[...], w_ref[...].astype(BF16), preferred_element_type=F32)
    o_ref[...] = y
    if group_refs:
        top_ref, tail_ref = group_refs
        top_ref[...] = y
        tail_ref[...] = y[y.shape[0] - tail_rows:]


def _ffn_down(x, w, layer, h, *, bm, bn, name, split_rows=None):
    m, k = x.shape
    n = w.shape[2]
    kc = k // K_SPLIT
    assert kc * K_SPLIT == k and kc % LANES == 0
    wb = w.dtype.itemsize
    tile = pl.BlockSpec((bm, bn), lambda i, p, j: (i, j))
    out_specs = [tile]
    out_shape = [jax.ShapeDtypeStruct((m, n), F32)]
    tail_rows = 0
    if split_rows is not None:
        tail_rows = m - split_rows
        last = m // bm - 1
        assert last * bm < split_rows and tail_rows <= bm
        final = K_SPLIT - 1
        out_specs += [
            pl.BlockSpec((bm, bn), lambda i, p, j: (i, jnp.where(p == final, j, 0))),
            pl.BlockSpec((tail_rows, bn),
                         lambda i, p, j: (0, jnp.where((i == last) & (p == final), j, 0))),
        ]
        out_shape += [jax.ShapeDtypeStruct((split_rows, n), F32),
                      jax.ShapeDtypeStruct((tail_rows, n), F32)]
    vmem = 2 * bm * kc * 2 + kc * bn * (2 * wb + 2) + 8 * bm * bn * 4
    return pl.pallas_call(
        functools.partial(_ffn_down_kernel, tail_rows=tail_rows),
        grid=(m // bm, K_SPLIT, n // bn),
        in_specs=[
            pl.BlockSpec((bm, kc), lambda i, p, j: (i, p)),
            pl.BlockSpec((None, kc, bn), lambda i, p, j: (layer, p, j)),
            tile,
        ],
        out_specs=out_specs,
        out_shape=out_shape,
        input_output_aliases={2: 0},
        compiler_params=_params(vmem, 3),
        name=name,
    )(x, w, h)


def _ffn_up_kernel(x_ref, wg_ref, wu_ref, o_ref):
    xn = x_ref[...]
    gate = jnp.dot(xn, wg_ref[...].astype(BF16), preferred_element_type=F32)
    up = jnp.dot(xn, wu_ref[...].astype(BF16), preferred_element_type=F32)
    o_ref[...] = (gate * jax.nn.sigmoid(gate) * up).astype(BF16)


def _ffn_up(xn, w_gate, w_up, layer, *, bm, bn):
    m, k = xn.shape
    n = w_gate.shape[2]
    wb = w_gate.dtype.itemsize
    vmem = 2 * bm * k * 2 + 2 * k * bn * (2 * wb + 2) + 2 * bm * bn * 2 + 5 * bm * bn * 4
    return pl.pallas_call(
        _ffn_up_kernel,
        grid=(m // bm, n // bn),
        in_specs=[
            pl.BlockSpec((bm, k), lambda i, j: (i, 0)),
            _layer_spec(k, bn, layer, lambda i, j: j),
            _layer_spec(k, bn, layer, lambda i, j: j),
        ],
        out_specs=pl.BlockSpec((bm, bn), lambda i, j: (i, j)),
        out_shape=jax.ShapeDtypeStruct((m, n), BF16),
        compiler_params=_params(vmem, 2),
        name="ffn_up",
    )(xn, w_gate, w_up)


def _q_proj_kernel(x_ref, w_ref, seg_ref, qg_ref, q_ref):
    y = jnp.dot(x_ref[...], w_ref[...].astype(BF16), preferred_element_type=F32)
    q_ref[...] = (_head_rms(y, seg_ref, qg_ref[...]) * HEAD_DIM ** -0.5).astype(BF16)


def _q_proj(xn, w, layer, seg, q_gain, *, bm, bn):
    m, k = xn.shape
    n = w.shape[2]
    wb = w.dtype.itemsize
    vmem = 2 * bm * k * 2 + k * bn * (2 * wb + 2) + 10 * bm * bn * 4
    return pl.pallas_call(
        _q_proj_kernel,
        grid=(m // bm, n // bn),
        in_specs=[
            pl.BlockSpec((bm, k), lambda i, j: (i, 0)),
            _layer_spec(k, bn, layer, lambda i, j: j),
            pl.BlockSpec(seg.shape, lambda i, j: (0, 0)),
            pl.BlockSpec((1, bn), lambda i, j: (0, 0)),
        ],
        out_specs=pl.BlockSpec((bm, bn), lambda i, j: (i, j)),
        out_shape=jax.ShapeDtypeStruct((m, n), BF16),
        compiler_params=_params(vmem, 2),
        name="q_proj",
    )(xn, w, seg, q_gain)


def _kv_proj_kernel(x_ref, w_ref, seg_ref, kg_ref, kf_ref, vf_ref, kz_ref, vz_ref):
    y = jnp.dot(x_ref[...], w_ref[...], preferred_element_type=F32)
    kn = _head_rms(y[:, :KV_WIDTH], seg_ref, kg_ref[...])
    v = y[:, KV_WIDTH:]
    kf_ref[...] = kn
    vf_ref[...] = v
    low = lax.broadcasted_iota(jnp.int32, (y.shape[0], LANES), 1) < HEAD_DIM
    for src, dst in ((kn, kz_ref), (v, vz_ref)):
        for p in range(N_KV_HEADS // HEADS_PER_VREG):
            both = src[:, p * LANES:(p + 1) * LANES]
            swapped = pltpu.roll(both, HEAD_DIM, axis=1)
            dst[2 * p, 0] = jnp.where(low, both, 0.0).astype(BF16)
            dst[2 * p, 1] = jnp.where(low, 0.0, swapped).astype(BF16)
            dst[2 * p + 1, 0] = jnp.where(low, swapped, 0.0).astype(BF16)
            dst[2 * p + 1, 1] = jnp.where(low, 0.0, both).astype(BF16)


def _kv_proj(xn, w, seg, k_gain, *, bm):
    m, k = xn.shape
    n = w.shape[1]
    vmem = (2 * bm * k * 2 + 2 * k * n * 2 + 8 * bm * n * 4
            + 2 * 2 * N_KV_HEADS * 2 * bm * LANES * 2)
    z_spec = pl.BlockSpec((N_KV_HEADS, 2, bm, LANES), lambda i: (0, 0, i, 0))
    row_spec = pl.BlockSpec((bm, KV_WIDTH), lambda i: (i, 0))
    z_shape = jax.ShapeDtypeStruct((N_KV_HEADS, 2, m, LANES), BF16)
    return pl.pallas_call(
        _kv_proj_kernel,
        grid=(m // bm,),
        in_specs=[
            pl.BlockSpec((bm, k), lambda i: (i, 0)),
            pl.BlockSpec((k, n), lambda i: (0, 0)),
            pl.BlockSpec(seg.shape, lambda i: (0, 0)),
            pl.BlockSpec((1, KV_WIDTH), lambda i: (0, 0)),
        ],
        out_specs=[row_spec, row_spec, z_spec, z_spec],
        out_shape=[
            jax.ShapeDtypeStruct((m, KV_WIDTH), F32),
            jax.ShapeDtypeStruct((m, KV_WIDTH), F32),
            z_shape, z_shape,
        ],
        compiler_params=_params(vmem, 1),
        name="kv_proj",
    )(xn, w, seg, k_gain)


def _attend(q_ref, o_ref, sink_ref, key_parts, val_parts, *, tq, bias_t=None):
    rows = PAIRS * tq
    lane_pair = lax.broadcasted_iota(jnp.int32, (1, rows), 1) >> (tq.bit_length() - 1)
    for kh in range(N_KV_HEADS):
        base = kh * GQA * HEAD_DIM
        q4 = jnp.concatenate(
            [q_ref[:, base + p * LANES:base + (p + 1) * LANES] for p in range(PAIRS)], axis=0)
        kk = jnp.concatenate(key_parts(kh, 0) + key_parts(kh, 1), axis=0)
        vv = jnp.concatenate(val_parts(kh, 0) + val_parts(kh, 1), axis=0)
        n_keys = kk.shape[0] // HEADS_PER_VREG
        s_t = lax.dot_general(kk, q4, (((1,), (1,)), ((), ())),
                              preferred_element_type=F32)
        if bias_t is not None:
            s_t = s_t + bias_t
        exps, denoms = [], []
        for parity in range(HEADS_PER_VREG):
            head = kh * GQA + parity
            sink = jnp.full((1, rows), sink_ref[head + HEADS_PER_VREG * (PAIRS - 1)], F32)
            for p in range(PAIRS - 1):
                sink = jnp.where(lane_pair == p, sink_ref[head + HEADS_PER_VREG * p], sink)
            half = s_t[parity * n_keys:(parity + 1) * n_keys]
            mx = jnp.maximum(jnp.max(half, axis=0, keepdims=True), sink)
            e = jnp.exp(half - mx)
            denom = jnp.sum(e, axis=0, keepdims=True) + jnp.exp(sink - mx)
            denoms.append(jnp.broadcast_to(denom, (HEAD_DIM, rows)))
            exps.append(e.astype(BF16))
        o_t = lax.dot_general(vv, jnp.concatenate(exps, axis=0),
                              (((0,), (0,)), ((), ())), preferred_element_type=F32)
        o = (o_t / jnp.concatenate(denoms, axis=0)).T
        for p in range(PAIRS):
            o_ref[:, base + p * LANES:base + (p + 1) * LANES] = (
                o[p * tq:(p + 1) * tq].astype(BF16))


def _attn_prompt_kernel(q_ref, kp_ref, kc_ref, vp_ref, vc_ref, sink_ref, o_ref):
    tq = ATT_ROWS
    n_keys = 2 * tq
    chunk_shift = CHUNK.bit_length() - 1
    chunks_per_block = tq // CHUNK
    key = lax.broadcasted_iota(jnp.int32, (2 * n_keys, PAIRS * tq), 0)
    query = lax.broadcasted_iota(jnp.int32, (2 * n_keys, PAIRS * tq), 1)
    q_chunk = (query & (tq - 1)) >> chunk_shift
    k_chunk = ((key & (n_keys - 1)) >> chunk_shift) - chunks_per_block
    oldest = jnp.where(pl.program_id(1) == 0, 0, -chunks_per_block)
    valid = (k_chunk <= q_chunk) & (k_chunk >= jnp.maximum(q_chunk - 2, oldest))
    bias = jnp.where(valid, 0.0, NEG)
    parts = lambda prev, cur: (lambda kh, parity: [prev[kh, parity], cur[kh, parity]])
    _attend(q_ref, o_ref, sink_ref, parts(kp_ref, kc_ref), parts(vp_ref, vc_ref),
            tq=tq, bias_t=bias)


def _attn_prompt(q, kz, vz, sinks, *, batch, seq):
    m = q.shape[0]
    tq = ATT_ROWS
    blocks = seq // tq
    cur = lambda b, c: (0, 0, b * blocks + c, 0)
    prev = lambda b, c: (0, 0, b * blocks + jnp.maximum(c - 1, 0), 0)
    z_block = (N_KV_HEADS, 2, tq, LANES)
    vmem = 4 * tq * D_MODEL * 2 + 8 * N_KV_HEADS * 2 * tq * LANES * 2 \
        + 10 * PAIRS * tq * 4 * tq * 4
    return pl.pallas_call(
        _attn_prompt_kernel,
        grid=(batch, blocks),
        in_specs=[
            pl.BlockSpec((tq, D_MODEL), lambda b, c: (b * blocks + c, 0)),
            pl.BlockSpec(z_block, prev),
            pl.BlockSpec(z_block, cur),
            pl.BlockSpec(z_block, prev),
            pl.BlockSpec(z_block, cur),
            pl.BlockSpec(memory_space=pltpu.SMEM),
        ],
        out_specs=pl.BlockSpec((tq, D_MODEL), lambda b, c: (b * blocks + c, 0)),
        out_shape=jax.ShapeDtypeStruct((m, D_MODEL), BF16),
        compiler_params=_params(vmem, 2),
        name="attn_prompt",
    )(q, kz, kz, vz, vz, sinks)


def _attn_sample_kernel(q_ref, ck_ref, cv_ref, kn_ref, vn_ref, sink_ref, o_ref, *, tq):
    parts = lambda cache, new: (lambda kh, parity: [cache[0, kh, parity], new[kh, parity]])
    _attend(q_ref, o_ref, sink_ref, parts(ck_ref, kn_ref), parts(cv_ref, vn_ref), tq=tq)


def _attn_sample(q, cache_kz, cache_vz, kz, vz, sinks, o_prev, *, streams, tq, row0):
    block0 = row0 // tq
    cache_spec = pl.BlockSpec((1, N_KV_HEADS, 2, WINDOW, LANES), lambda b: (b, 0, 0, 0, 0))
    new_spec = pl.BlockSpec((N_KV_HEADS, 2, tq, LANES), lambda b: (0, 0, block0 + b, 0))
    row_spec = pl.BlockSpec((tq, D_MODEL), lambda b: (block0 + b, 0))
    vmem = 4 * tq * D_MODEL * 2 + 4 * N_KV_HEADS * 2 * (WINDOW + tq) * LANES * 2 \
        + 10 * PAIRS * tq * 2 * (WINDOW + tq) * 4
    return pl.pallas_call(
        _skip_ref(functools.partial(_attn_sample_kernel, tq=tq), 6),
        grid=(streams,),
        in_specs=[
            row_spec, cache_spec, cache_spec, new_spec, new_spec,
            pl.BlockSpec(memory_space=pltpu.SMEM),
            pl.BlockSpec(memory_space=pl.ANY),
        ],
        out_specs=row_spec,
        out_shape=jax.ShapeDtypeStruct(o_prev.shape, BF16),
        input_output_aliases={6: 0},
        compiler_params=_params(vmem, 1),
        name="attn_sample",
    )(q, cache_kz, cache_vz, kz, vz, sinks, o_prev)


def _lane_pair_copies(cache):
    c = cache.transpose(0, 2, 1, 3).astype(BF16)
    z = jnp.zeros_like(c)
    return jnp.stack([jnp.concatenate([c, z], axis=-1), jnp.concatenate([z, c], axis=-1)], axis=2)


def kernel(x_prompt, x_sample, cache_k, cache_v, norm_a, w_sgu_in, sgu_ln_g, sgu_ln_b, w_sgu_s,
           b_sgu_s, w_sgu_out, norm_kv, w_kv, k_norm, norm_b, w_q, q_norm, sinks, w_o,
           norm_ffn, w_ffn_gate, w_ffn_up, w_ffn_down):
    batch, seq, d = x_prompt.shape
    streams, dec_seq, _ = x_sample.shape
    m_prompt = batch * seq
    m_sample = streams * dec_seq
    m = m_prompt + m_sample
    assert d == D_MODEL and seq % ATT_ROWS == 0
    assert m % WIDE_TILE == 0 and m % KV_TILE == 0
    assert norm_a.shape[0] == 1 and norm_b.shape[0] == 1 and norm_ffn.shape[0] == 2

    row = lambda g: g.reshape(1, -1).astype(F32)

    def ffn(h, layer, name, split_rows=None):
        (xn,) = _rms(h, [row(norm_ffn[layer])])
        hidden = _ffn_up(xn, w_ffn_gate, w_ffn_up, layer, bm=WIDE_TILE, bn=MXU_WIDTH)
        return _ffn_down(hidden, w_ffn_down, layer, h, bm=WIDE_TILE, bn=MXU_WIDTH, name=name,
                         split_rows=split_rows)

    x_rows = (x_prompt.reshape(m_prompt, d), x_sample.reshape(m_sample, d))
    xn = _rms_stack(*x_rows, row(norm_a[0]))
    uv = _sgu_in(xn, w_sgu_in, 0, bm=WIDE_TILE, bn=2 * MXU_WIDTH)

    pos_chunk = jnp.arange(SGU_CHUNK) // CHUNK
    ws = jnp.where((pos_chunk[:, None] >= pos_chunk[None, :])[None], w_sgu_s[0], 0.0).astype(BF16)
    bs_t = b_sgu_s[0].T.astype(F32)
    ln_g, ln_b = row(sgu_ln_g[0]), row(sgu_ln_b[0])
    chunks_per_block = 2
    rows_per_block = SGU_CHUNK * chunks_per_block
    (a,) = _sgu_mix(uv, ln_g, ln_b, ws, bs_t, None, chunk=SGU_CHUNK, n_chunks=chunks_per_block,
                    row_block0=0, n_blocks=m_prompt // rows_per_block, emit_vn=False)
    a, v_rows = _sgu_mix(uv, ln_g, ln_b, ws[:, :dec_seq, :dec_seq], bs_t[:dec_seq], a,
                         chunk=dec_seq, n_chunks=streams, row_block0=m_prompt // m_sample,
                         n_blocks=1, emit_vn=True)
    h = _mm_res(a, w_sgu_out, 0, x_rows, bm=WIDE_TILE, bn=MXU_WIDTH, name="sgu_out")
    (h,) = ffn(h, 0, "ffn_down_0")

    lane_head = jnp.arange(MXU_WIDTH) // HEAD_DIM
    seg = (lane_head[:, None] == lane_head[None, :]).astype(BF16)
    xn_kv, xn_q = _rms(h, [row(norm_kv), row(norm_b[0])])
    k_rows, v_rows_kv, kz, vz = _kv_proj(
        xn_kv, w_kv.astype(BF16), seg, row(jnp.tile(k_norm, N_KV_HEADS)), bm=KV_TILE)
    q = _q_proj(xn_q, w_q, 0, seg, row(jnp.tile(q_norm[0], MXU_WIDTH // HEAD_DIM)),
                bm=WIDE_TILE, bn=MXU_WIDTH)
    sink = sinks[0].astype(F32)
    o = _attn_prompt(q, kz, vz, sink, batch=batch, seq=seq)
    o = _attn_sample(q, _lane_pair_copies(cache_k), _lane_pair_copies(cache_v), kz, vz, sink, o,
                     streams=streams, tq=dec_seq, row0=m_prompt)
    h = _mm_res(o, w_o, 0, h, bm=WIDE_TILE, bn=MXU_WIDTH, name="attn_out")
    _, y_prompt, y_sample = ffn(h, 1, "ffn_down_1", split_rows=m_prompt)

    kv4 = lambda t, b, s: t.reshape(b, s, N_KV_HEADS, HEAD_DIM)
    k_p = kv4(k_rows[:m_prompt], batch, seq)[:, -WINDOW:]
    v_p = kv4(v_rows_kv[:m_prompt], batch, seq)[:, -WINDOW:]
    k_s = kv4(k_rows[m_prompt:], streams, dec_seq)
    v_s = kv4(v_rows_kv[m_prompt:], streams, dec_seq)
    return (y_prompt.reshape(batch, seq, d), y_sample.reshape(streams, dec_seq, d),
            k_p, v_p, k_s, v_s, v_rows.reshape(1, streams, dec_seq, d))
```

```python
import functools

import jax
import jax.numpy as jnp
from jax import lax
from jax.experimental import pallas as pl
from jax.experimental.pallas import tpu as pltpu

F32 = jnp.float32
BF16 = jnp.bfloat16

D_MODEL = 4096
SGU_CHUNK = 128
SGU_GROUPS = 8
SGU_GROUP_WIDTH = D_MODEL // SGU_GROUPS
CHUNK = 64
HEAD_DIM = 64
N_HEADS = 64
N_KV_HEADS = 8
GQA = N_HEADS // N_KV_HEADS
KV_WIDTH = N_KV_HEADS * HEAD_DIM
WINDOW = 128
NEG = -1e30
RMS_EPS = 1e-6
LN_EPS = 1e-5

V7X_VMEM_BYTES = 64 * 1024 * 1024
VMEM_LIMIT_CAP = V7X_VMEM_BYTES - 6 * 1024 * 1024
LANES = 128
MXU_WIDTH = 256
HEADS_PER_VREG = LANES // HEAD_DIM
PAIRS = GQA // HEADS_PER_VREG

WIDE_TILE = 1408
KV_TILE = WIDE_TILE // 2
NORM_ROWS = 256
ATT_ROWS = 128
RMS_STAT_ROWS = 64
RMS_SCALE_ROWS = 32


def _params(vmem_bytes, n_axes):
    limit = min(int(vmem_bytes) + 8 * 1024 * 1024, VMEM_LIMIT_CAP)
    return pltpu.CompilerParams(
        dimension_semantics=("arbitrary",) * n_axes, vmem_limit_bytes=limit)


def _layer_spec(k, bn, layer, col_of):
    return pl.BlockSpec((None, k, bn), lambda *idx: (layer, 0, col_of(*idx)))


def _skip_ref(kernel, index):
    def wrapped(*refs):
        return kernel(*refs[:index], *refs[index + 1:])
    return wrapped


def _rms_rows(x_ref, gain_refs, xn_refs, r_ref):
    rows, k = x_ref.shape

    def stats(i, carry):
        sl = pl.ds(pl.multiple_of(i * RMS_STAT_ROWS, RMS_STAT_ROWS), RMS_STAT_ROWS)
        x = x_ref[sl, :]
        ms = jnp.mean(x * x, axis=-1, keepdims=True)
        r_ref[sl, :] = jnp.broadcast_to(lax.rsqrt(ms + RMS_EPS), (RMS_STAT_ROWS, LANES))
        return carry

    lax.fori_loop(0, rows // RMS_STAT_ROWS, stats, 0)

    for g_ref, xn_ref in zip(gain_refs, xn_refs):
        g = g_ref[...]

        def scale(i, carry, g=g, xn_ref=xn_ref):
            sl = pl.ds(pl.multiple_of(i * RMS_SCALE_ROWS, RMS_SCALE_ROWS), RMS_SCALE_ROWS)
            r = jnp.tile(r_ref[sl, :], (1, k // LANES))
            xn_ref[sl, :] = (x_ref[sl, :] * r * g).astype(BF16)
            return carry

        lax.fori_loop(0, rows // RMS_SCALE_ROWS, scale, 0)


def _rms_kernel(x_ref, *refs, n_gains):
    gain_refs, xn_refs, r_ref = refs[:n_gains], refs[n_gains:2 * n_gains], refs[2 * n_gains]
    _rms_rows(x_ref, gain_refs, xn_refs, r_ref)


def _rms(x, gains):
    m, k = x.shape
    n = len(gains)
    row_spec = pl.BlockSpec((NORM_ROWS, k), lambda i: (i, 0))
    gain_spec = pl.BlockSpec((1, k), lambda i: (0, 0))
    vmem = 2 * NORM_ROWS * k * 4 + 2 * n * NORM_ROWS * k * 2 + 4 * RMS_STAT_ROWS * k * 4
    return pl.pallas_call(
        functools.partial(_rms_kernel, n_gains=n),
        grid=(m // NORM_ROWS,),
        in_specs=[row_spec] + [gain_spec] * n,
        out_specs=[row_spec] * n,
        out_shape=[jax.ShapeDtypeStruct((m, k), BF16)] * n,
        scratch_shapes=[pltpu.VMEM((NORM_ROWS, LANES), F32)],
        compiler_params=_params(vmem, 1),
        name="rms_%d" % n,
    )(x, *gains)


def _rms_stack_kernel(xp_ref, xs_ref, g_ref, xn_ref, r_ref, *, prompt_blocks):
    from_ref = lambda x_ref: _rms_rows(x_ref, (g_ref,), (xn_ref,), r_ref)
    pl.when(pl.program_id(0) < prompt_blocks)(lambda: from_ref(xp_ref))
    pl.when(pl.program_id(0) >= prompt_blocks)(lambda: from_ref(xs_ref))


def _rms_stack(x_prompt, x_sample, gain):
    mp, k = x_prompt.shape
    ms = x_sample.shape[0]
    assert mp % NORM_ROWS == 0 and ms == NORM_ROWS
    prompt_blocks = mp // NORM_ROWS
    m = mp + ms
    vmem = 4 * NORM_ROWS * k * 4 + 2 * NORM_ROWS * k * 2 + 4 * RMS_STAT_ROWS * k * 4
    return pl.pallas_call(
        functools.partial(_rms_stack_kernel, prompt_blocks=prompt_blocks),
        grid=(m // NORM_ROWS,),
        in_specs=[
            pl.BlockSpec((NORM_ROWS, k), lambda i: (jnp.minimum(i, prompt_blocks - 1), 0)),
            pl.BlockSpec((NORM_ROWS, k), lambda i: (0, 0)),
            pl.BlockSpec((1, k), lambda i: (0, 0)),
        ],
        out_specs=pl.BlockSpec((NORM_ROWS, k), lambda i: (i, 0)),
        out_shape=jax.ShapeDtypeStruct((m, k), BF16),
        scratch_shapes=[pltpu.VMEM((NORM_ROWS, LANES), F32)],
        compiler_params=_params(vmem, 1),
        name="rms_stack",
    )(x_prompt, x_sample, gain)


def _head_rms(y, seg_ref, gain_row):
    y2 = y * y
    hi = y2.astype(BF16)
    lo = (y2 - hi.astype(F32)).astype(BF16)
    seg = seg_ref[...]
    w = seg.shape[0]
    ssq = jnp.concatenate(
        [jnp.dot(hi[:, c:c + w], seg, preferred_element_type=F32)
         + jnp.dot(lo[:, c:c + w], seg, preferred_element_type=F32)
         for c in range(0, y.shape[1], w)], axis=-1)
    return y * lax.rsqrt(ssq * (1.0 / HEAD_DIM) + RMS_EPS) * gain_row


def _sgu_in_kernel(x_ref, w_ref, o_ref):
    y = jnp.dot(x_ref[...], w_ref[...].astype(BF16), preferred_element_type=F32)
    o_ref[...] = 0.5 * y * (1.0 + lax.erf(y * (0.5 ** 0.5)))


def _sgu_in(xn, w, layer, *, bm, bn):
    m, k = xn.shape
    n = w.shape[2]
    wb = w.dtype.itemsize
    vmem = 2 * bm * k * 2 + k * bn * (2 * wb + 2) + 5 * bm * bn * 4
    return pl.pallas_call(
        _sgu_in_kernel,
        grid=(m // bm, n // bn),
        in_specs=[
            pl.BlockSpec((bm, k), lambda i, j: (i, 0)),
            _layer_spec(k, bn, layer, lambda i, j: j),
        ],
        out_specs=pl.BlockSpec((bm, bn), lambda i, j: (i, j)),
        out_shape=jax.ShapeDtypeStruct((m, n), F32),
        compiler_params=_params(vmem, 2),
        name="sgu_in",
    )(xn, w)


def _sgu_mix_rows(u_ref, v_ref, lng_ref, lnb_ref, ws_ref, bs_ref, a_ref, vn_ref, *, chunk):
    ln_g = lng_ref[...]
    ln_b = lnb_ref[...]
    for c in range(u_ref.shape[0] // chunk):
        rows = slice(c * chunk, (c + 1) * chunk)
        v = v_ref[rows, :]
        mu = jnp.mean(v, axis=-1, keepdims=True)
        vc = v - mu
        var = jnp.mean(vc * vc, axis=-1, keepdims=True)
        vn = vc * lax.rsqrt(var + LN_EPS) * ln_g + ln_b
        if vn_ref is not None:
            vn_ref[rows, :] = vn
        vnb = vn.astype(BF16)
        for g in range(SGU_GROUPS):
            cols = slice(g * SGU_GROUP_WIDTH, (g + 1) * SGU_GROUP_WIDTH)
            mixed = jnp.dot(ws_ref[g], vnb[:, cols], preferred_element_type=F32)
            mixed = mixed + bs_ref[:, g:g + 1]
            a_ref[rows, cols] = (u_ref[rows, cols] * mixed).astype(BF16)


def _sgu_mix_kernel(u_ref, v_ref, lng_ref, lnb_ref, ws_ref, bs_ref, wss_ref, bss_ref,
                    a_ref, vn_ref, *, prompt_blocks, sample_chunk):
    common = (u_ref, v_ref, lng_ref, lnb_ref)
    is_prompt = pl.program_id(0) < prompt_blocks
    pl.when(is_prompt)(
        lambda: _sgu_mix_rows(*common, ws_ref, bs_ref, a_ref, None, chunk=SGU_CHUNK))
    pl.when(jnp.logical_not(is_prompt))(
        lambda: _sgu_mix_rows(*common, wss_ref, bss_ref, a_ref, vn_ref, chunk=sample_chunk))


def _sgu_mix(uv, ln_g, ln_b, ws, bs_t, *, m_prompt, sample_chunk):
    m = uv.shape[0]
    d = D_MODEL
    rows = m - m_prompt
    assert m_prompt % rows == 0 and rows % SGU_CHUNK == 0 and rows % sample_chunk == 0
    prompt_blocks = m_prompt // rows
    const = lambda shape: pl.BlockSpec(shape, lambda i: (0,) * len(shape))
    wss, bss = ws[:, :sample_chunk, :sample_chunk], bs_t[:sample_chunk]
    vmem = 2 * 2 * rows * d * 4 + 2 * rows * d * 2 + 2 * rows * d * 4 + 6 * SGU_CHUNK * d * 4
    return pl.pallas_call(
        functools.partial(_sgu_mix_kernel, prompt_blocks=prompt_blocks,
                          sample_chunk=sample_chunk),
        grid=(prompt_blocks + 1,),
        in_specs=[
            pl.BlockSpec((rows, d), lambda i: (i, 0)),
            pl.BlockSpec((rows, d), lambda i: (i, 1)),
            const((1, d)), const((1, d)),
            const(ws.shape), const(bs_t.shape), const(wss.shape), const(bss.shape),
        ],
        out_specs=[pl.BlockSpec((rows, d), lambda i: (i, 0)), const((rows, d))],
        out_shape=[jax.ShapeDtypeStruct((m, d), BF16), jax.ShapeDtypeStruct((rows, d), F32)],
        compiler_params=_params(vmem, 1),
        name="sgu_mix",
    )(uv, uv, ln_g, ln_b, ws, bs_t, wss, bss)


def _mm_res_kernel(a_ref, w_ref, res_ref, o_ref):
    y = jnp.dot(a_ref[...], w_ref[...].astype(BF16), preferred_element_type=F32)
    o_ref[...] = res_ref[...] + y


def _mm_res2_kernel(a_ref, w_ref, top_ref, tail_ref, o_ref, *, top_rows):
    y = jnp.dot(a_ref[...], w_ref[...].astype(BF16), preferred_element_type=F32)
    is_last = pl.program_id(0) == pl.num_programs(0) - 1
    o_ref[:top_rows, :] = top_ref[:top_rows, :] + y[:top_rows]

    @pl.when(jnp.logical_not(is_last))
    def _():
        o_ref[top_rows:, :] = top_ref[top_rows:, :] + y[top_rows:]

    @pl.when(is_last)
    def _():
        o_ref[top_rows:, :] = tail_ref[...] + y[top_rows:]


def _mm_res(a, w, layer, res, *, bm, bn, name):
    m, k = a.shape
    n = w.shape[2]
    wb = w.dtype.itemsize
    vmem = 2 * bm * k * 2 + k * bn * (2 * wb + 2) + 6 * bm * bn * 4
    tile = pl.BlockSpec((bm, bn), lambda i, j: (i, j))
    if isinstance(res, tuple):
        top, tail = res
        tail_rows = tail.shape[0]
        top_rows = bm - tail_rows
        assert top.shape[0] + tail_rows == m and top_rows > 0 and top_rows % 8 == 0
        kernel = functools.partial(_mm_res2_kernel, top_rows=top_rows)
        res_specs = [tile, pl.BlockSpec((tail_rows, bn), lambda i, j: (0, j))]
        res_args = [top, tail]
        vmem += 2 * tail_rows * bn * 4
    else:
        kernel, res_specs, res_args = _mm_res_kernel, [tile], [res]
    return pl.pallas_call(
        kernel,
        grid=(m // bm, n // bn),
        in_specs=[
            pl.BlockSpec((bm, k), lambda i, j: (i, 0)),
            _layer_spec(k, bn, layer, lambda i, j: j),
        ] + res_specs,
        out_specs=tile,
        out_shape=jax.ShapeDtypeStruct((m, n), F32),
        compiler_params=_params(vmem, 2),
        name=name,
    )(a, w, *res_args)


K_SPLIT = 2


def _ffn_down_kernel(x_ref, w_ref, h_ref, *out_refs, tail_rows):
    y = h_ref[...] + jnp.dot(x_ref[...], w_ref[...].astype(BF16), preferred_element_type=F32)
    out_refs[0][...] = y
    if tail_rows:
        out_refs[1][...] = y[y.shape[0] - tail_rows:]


def _ffn_down_part(x, w, layer, part, h, *, bm, bn, name, split_rows):
    m, k = x.shape
    n = w.shape[2]
    kc = k // K_SPLIT
    assert kc * K_SPLIT == k and kc % LANES == 0
    wb = w.dtype.itemsize
    tile = pl.BlockSpec((bm, bn), lambda i, j: (i, j))
    if split_rows is None:
        tail_rows = 0
        out_specs, out_shape = [tile], [jax.ShapeDtypeStruct((m, n), F32)]
    else:
        tail_rows = m - split_rows
        last = m // bm - 1
        assert last * bm < split_rows and tail_rows <= bm
        out_specs = [tile, pl.BlockSpec((tail_rows, bn),
                                        lambda i, j: (0, jnp.where(i == last, j, 0)))]
        out_shape = [jax.ShapeDtypeStruct((split_rows, n), F32),
                     jax.ShapeDtypeStruct((tail_rows, n), F32)]
    vmem = 2 * bm * kc * 2 + kc * bn * (2 * wb + 2) + 8 * bm * bn * 4
    return pl.pallas_call(
        functools.partial(_ffn_down_kernel, tail_rows=tail_rows),
        grid=(m // bm, n // bn),
        in_specs=[
            pl.BlockSpec((bm, kc), lambda i, j: (i, part)),
            pl.BlockSpec((None, kc, bn), lambda i, j: (layer, part, j)),
            tile,
        ],
        out_specs=out_specs,
        out_shape=out_shape,
        compiler_params=_params(vmem, 2),
        name="%s_%d" % (name, part),
    )(x, w, h)


def _ffn_down(x, w, layer, h, *, bm, bn, name, split_rows=None):
    for part in range(K_SPLIT):
        final = part == K_SPLIT - 1
        outs = _ffn_down_part(x, w, layer, part, h, bm=bm, bn=bn, name=name,
                              split_rows=split_rows if final else None)
        h = outs[0]
    return outs


def _ffn_up_kernel(x_ref, wg_ref, wu_ref, o_ref):
    xn = x_ref[...]
    gate = jnp.dot(xn, wg_ref[...].astype(BF16), preferred_element_type=F32)
    up = jnp.dot(xn, wu_ref[...].astype(BF16), preferred_element_type=F32)
    o_ref[...] = (gate * jax.nn.sigmoid(gate) * up).astype(BF16)


def _ffn_up(xn, w_gate, w_up, layer, *, bm, bn):
    m, k = xn.shape
    n = w_gate.shape[2]
    wb = w_gate.dtype.itemsize
    vmem = 2 * bm * k * 2 + 2 * k * bn * (2 * wb + 2) + 2 * bm * bn * 2 + 5 * bm * bn * 4
    return pl.pallas_call(
        _ffn_up_kernel,
        grid=(m // bm, n // bn),
        in_specs=[
            pl.BlockSpec((bm, k), lambda i, j: (i, 0)),
            _layer_spec(k, bn, layer, lambda i, j: j),
            _layer_spec(k, bn, layer, lambda i, j: j),
        ],
        out_specs=pl.BlockSpec((bm, bn), lambda i, j: (i, j)),
        out_shape=jax.ShapeDtypeStruct((m, n), BF16),
        compiler_params=_params(vmem, 2),
        name="ffn_up",
    )(xn, w_gate, w_up)


def _q_proj_kernel(x_ref, w_ref, seg_ref, qg_ref, q_ref):
    y = jnp.dot(x_ref[...], w_ref[...].astype(BF16), preferred_element_type=F32)
    q_ref[...] = (_head_rms(y, seg_ref, qg_ref[...]) * HEAD_DIM ** -0.5).astype(BF16)


def _q_proj(xn, w, layer, seg, q_gain, *, bm, bn):
    m, k = xn.shape
    n = w.shape[2]
    wb = w.dtype.itemsize
    vmem = 2 * bm * k * 2 + k * bn * (2 * wb + 2) + 10 * bm * bn * 4
    return pl.pallas_call(
        _q_proj_kernel,
        grid=(m // bm, n // bn),
        in_specs=[
            pl.BlockSpec((bm, k), lambda i, j: (i, 0)),
            _layer_spec(k, bn, layer, lambda i, j: j),
            pl.BlockSpec(seg.shape, lambda i, j: (0, 0)),
            pl.BlockSpec((1, bn), lambda i, j: (0, 0)),
        ],
        out_specs=pl.BlockSpec((bm, bn), lambda i, j: (i, j)),
        out_shape=jax.ShapeDtypeStruct((m, n), BF16),
        compiler_params=_params(vmem, 2),
        name="q_proj",
    )(xn, w, seg, q_gain)


def _kv_proj_kernel(x_ref, w_ref, seg_ref, kg_ref, kf_ref, vf_ref, kz_ref, vz_ref):
    y = jnp.dot(x_ref[...], w_ref[...], preferred_element_type=F32)
    kn = _head_rms(y[:, :KV_WIDTH], seg_ref, kg_ref[...])
    v = y[:, KV_WIDTH:]
    kf_ref[...] = kn
    vf_ref[...] = v
    low = lax.broadcasted_iota(jnp.int32, (y.shape[0], LANES), 1) < HEAD_DIM
    for src, dst in ((kn, kz_ref), (v, vz_ref)):
        for p in range(N_KV_HEADS // HEADS_PER_VREG):
            both = src[:, p * LANES:(p + 1) * LANES]
            swapped = pltpu.roll(both, HEAD_DIM, axis=1)
            dst[2 * p, 0] = jnp.where(low, both, 0.0).astype(BF16)
            dst[2 * p, 1] = jnp.where(low, 0.0, swapped).astype(BF16)
            dst[2 * p + 1, 0] = jnp.where(low, swapped, 0.0).astype(BF16)
            dst[2 * p + 1, 1] = jnp.where(low, 0.0, both).astype(BF16)


def _kv_proj(xn, w, seg, k_gain, *, bm):
    m, k = xn.shape
    n = w.shape[1]
    vmem = (2 * bm * k * 2 + 2 * k * n * 2 + 8 * bm * n * 4
            + 2 * 2 * N_KV_HEADS * 2 * bm * LANES * 2)
    z_spec = pl.BlockSpec((N_KV_HEADS, 2, bm, LANES), lambda i: (0, 0, i, 0))
    row_spec = pl.BlockSpec((bm, KV_WIDTH), lambda i: (i, 0))
    z_shape = jax.ShapeDtypeStruct((N_KV_HEADS, 2, m, LANES), BF16)
    return pl.pallas_call(
        _kv_proj_kernel,
        grid=(m // bm,),
        in_specs=[
            pl.BlockSpec((bm, k), lambda i: (i, 0)),
            pl.BlockSpec((k, n), lambda i: (0, 0)),
            pl.BlockSpec(seg.shape, lambda i: (0, 0)),
            pl.BlockSpec((1, KV_WIDTH), lambda i: (0, 0)),
        ],
        out_specs=[row_spec, row_spec, z_spec, z_spec],
        out_shape=[
            jax.ShapeDtypeStruct((m, KV_WIDTH), F32),
            jax.ShapeDtypeStruct((m, KV_WIDTH), F32),
            z_shape, z_shape,
        ],
        compiler_params=_params(vmem, 1),
        name="kv_proj",
    )(xn, w, seg, k_gain)


def _attend(q_ref, o_ref, sink_ref, key_parts, val_parts, *, row0, tq, bias_t=None):
    rows = PAIRS * tq
    lane_pair = lax.broadcasted_iota(jnp.int32, (1, rows), 1) >> (tq.bit_length() - 1)
    for kh in range(N_KV_HEADS):
        base = kh * GQA * HEAD_DIM
        q4 = jnp.concatenate(
            [q_ref[row0:row0 + tq, base + p * LANES:base + (p + 1) * LANES]
             for p in range(PAIRS)], axis=0)
        kk = jnp.concatenate(key_parts(kh, 0) + key_parts(kh, 1), axis=0)
        vv = jnp.concatenate(val_parts(kh, 0) + val_parts(kh, 1), axis=0)
        n_keys = kk.shape[0] // HEADS_PER_VREG
        s_t = lax.dot_general(kk, q4, (((1,), (1,)), ((), ())),
                              preferred_element_type=F32)
        if bias_t is not None:
            s_t = s_t + bias_t
        exps, denoms = [], []
        for parity in range(HEADS_PER_VREG):
            head = kh * GQA + parity
            sink = jnp.full((1, rows), sink_ref[head + HEADS_PER_VREG * (PAIRS - 1)], F32)
            for p in range(PAIRS - 1):
                sink = jnp.where(lane_pair == p, sink_ref[head + HEADS_PER_VREG * p], sink)
            half = s_t[parity * n_keys:(parity + 1) * n_keys]
            mx = jnp.maximum(jnp.max(half, axis=0, keepdims=True), sink)
            e = jnp.exp(half - mx)
            denom = jnp.sum(e, axis=0, keepdims=True) + jnp.exp(sink - mx)
            denoms.append(jnp.broadcast_to(denom, (HEAD_DIM, rows)))
            exps.append(e.astype(BF16))
        o_t = lax.dot_general(vv, jnp.concatenate(exps, axis=0),
                              (((0,), (0,)), ((), ())), preferred_element_type=F32)
        o = (o_t / jnp.concatenate(denoms, axis=0)).T
        for p in range(PAIRS):
            o_ref[row0:row0 + tq, base + p * LANES:base + (p + 1) * LANES] = (
                o[p * tq:(p + 1) * tq].astype(BF16))


def _attend_prompt_block(q_ref, kp_ref, kc_ref, vp_ref, vc_ref, sink_ref, o_ref, first):
    tq = ATT_ROWS
    n_keys = 2 * tq
    chunk_shift = CHUNK.bit_length() - 1
    chunks_per_block = tq // CHUNK
    key = lax.broadcasted_iota(jnp.int32, (2 * n_keys, PAIRS * tq), 0)
    query = lax.broadcasted_iota(jnp.int32, (2 * n_keys, PAIRS * tq), 1)
    q_chunk = (query & (tq - 1)) >> chunk_shift
    k_chunk = ((key & (n_keys - 1)) >> chunk_shift) - chunks_per_block
    oldest = jnp.where(first, 0, -chunks_per_block)
    valid = (k_chunk <= q_chunk) & (k_chunk >= jnp.maximum(q_chunk - 2, oldest))
    bias = jnp.where(valid, 0.0, NEG)
    parts = lambda prev, cur: (lambda kh, parity: [prev[kh, parity], cur[kh, parity]])
    _attend(q_ref, o_ref, sink_ref, parts(kp_ref, kc_ref), parts(vp_ref, vc_ref),
            row0=0, tq=tq, bias_t=bias)


def _attend_sample_block(q_ref, ck_ref, cv_ref, kc_ref, vc_ref, sink_ref, o_ref, *, dec_seq):
    for st in range(ATT_ROWS // dec_seq):
        new = slice(st * dec_seq, (st + 1) * dec_seq)
        parts = lambda cache, cur, st=st, new=new: (
            lambda kh, parity: [cache[st, kh, parity], cur[kh, parity, new, :]])
        _attend(q_ref, o_ref, sink_ref, parts(ck_ref, kc_ref), parts(cv_ref, vc_ref),
                row0=st * dec_seq, tq=dec_seq)


def _attn_kernel(q_ref, kp_ref, kc_ref, vp_ref, vc_ref, ck_ref, cv_ref, sink_ref, o_ref, *,
                 prompt_blocks, blocks_per_seq, dec_seq):
    step = pl.program_id(0)
    is_prompt = step < prompt_blocks
    first = (step & (blocks_per_seq - 1)) == 0
    pl.when(is_prompt)(lambda: _attend_prompt_block(
        q_ref, kp_ref, kc_ref, vp_ref, vc_ref, sink_ref, o_ref, first))
    pl.when(jnp.logical_not(is_prompt))(lambda: _attend_sample_block(
        q_ref, ck_ref, cv_ref, kc_ref, vc_ref, sink_ref, o_ref, dec_seq=dec_seq))


def _attn(q, kz, vz, cache_kz, cache_vz, sinks, *, m_prompt, seq, dec_seq):
    m = q.shape[0]
    tq = ATT_ROWS
    blocks_per_seq = seq // tq
    prompt_blocks = m_prompt // tq
    streams_per_block = tq // dec_seq
    assert blocks_per_seq & (blocks_per_seq - 1) == 0 and (m - m_prompt) % tq == 0
    rows_here = lambda s: (s, 0)
    cur = lambda s: (0, 0, s, 0)
    prev = lambda s: (0, 0, jnp.where((s & (blocks_per_seq - 1)) == 0, s, s - 1), 0)
    cache = lambda s: (jnp.maximum(s - prompt_blocks, 0), 0, 0, 0, 0)
    z_block = (N_KV_HEADS, 2, tq, LANES)
    cache_block = (streams_per_block, N_KV_HEADS, 2, WINDOW, LANES)
    vmem = 4 * tq * D_MODEL * 2 + 8 * N_KV_HEADS * 2 * tq * LANES * 2 \
        + 4 * streams_per_block * N_KV_HEADS * 2 * WINDOW * LANES * 2 \
        + 10 * PAIRS * tq * 4 * tq * 4
    return pl.pallas_call(
        functools.partial(_attn_kernel, prompt_blocks=prompt_blocks,
                          blocks_per_seq=blocks_per_seq, dec_seq=dec_seq),
        grid=(m // tq,),
        in_specs=[
            pl.BlockSpec((tq, D_MODEL), rows_here),
            pl.BlockSpec(z_block, prev),
            pl.BlockSpec(z_block, cur),
            pl.BlockSpec(z_block, prev),
            pl.BlockSpec(z_block, cur),
            pl.BlockSpec(cache_block, cache),
            pl.BlockSpec(cache_block, cache),
            pl.BlockSpec(memory_space=pltpu.SMEM),
        ],
        out_specs=pl.BlockSpec((tq, D_MODEL), rows_here),
        out_shape=jax.ShapeDtypeStruct((m, D_MODEL), BF16),
        compiler_params=_params(vmem, 1),
        name="attn",
    )(q, kz, kz, vz, vz, cache_kz, cache_vz, sinks)


def _lane_pair_copies(cache):
    c = cache.transpose(0, 2, 1, 3).astype(BF16)
    z = jnp.zeros_like(c)
    return jnp.stack([jnp.concatenate([c, z], axis=-1), jnp.concatenate([z, c], axis=-1)], axis=2)


def kernel(x_prompt, x_sample, cache_k, cache_v, norm_a, w_sgu_in, sgu_ln_g, sgu_ln_b, w_sgu_s,
           b_sgu_s, w_sgu_out, norm_kv, w_kv, k_norm, norm_b, w_q, q_norm, sinks, w_o,
           norm_ffn, w_ffn_gate, w_ffn_up, w_ffn_down):
    batch, seq, d = x_prompt.shape
    streams, dec_seq, _ = x_sample.shape
    m_prompt = batch * seq
    m_sample = streams * dec_seq
    m = m_prompt + m_sample
    assert d == D_MODEL and seq % ATT_ROWS == 0
    assert m % WIDE_TILE == 0 and m % KV_TILE == 0
    assert norm_a.shape[0] == 1 and norm_b.shape[0] == 1 and norm_ffn.shape[0] == 2

    row = lambda g: g.reshape(1, -1).astype(F32)

    def ffn(h, layer, name, split_rows=None):
        (xn,) = _rms(h, [row(norm_ffn[layer])])
        hidden = _ffn_up(xn, w_ffn_gate, w_ffn_up, layer, bm=WIDE_TILE, bn=MXU_WIDTH)
        return _ffn_down(hidden, w_ffn_down, layer, h, bm=WIDE_TILE, bn=MXU_WIDTH, name=name,
                         split_rows=split_rows)

    x_rows = (x_prompt.reshape(m_prompt, d), x_sample.reshape(m_sample, d))
    xn = _rms_stack(*x_rows, row(norm_a[0]))
    uv = _sgu_in(xn, w_sgu_in, 0, bm=WIDE_TILE, bn=2 * MXU_WIDTH)

    pos_chunk = jnp.arange(SGU_CHUNK) // CHUNK
    ws = jnp.where((pos_chunk[:, None] >= pos_chunk[None, :])[None], w_sgu_s[0], 0.0).astype(BF16)
    bs_t = b_sgu_s[0].T.astype(F32)
    ln_g, ln_b = row(sgu_ln_g[0]), row(sgu_ln_b[0])
    a, v_rows = _sgu_mix(uv, ln_g, ln_b, ws, bs_t, m_prompt=m_prompt, sample_chunk=dec_seq)
    h = _mm_res(a, w_sgu_out, 0, x_rows, bm=WIDE_TILE, bn=MXU_WIDTH, name="sgu_out")
    (h,) = ffn(h, 0, "ffn_down_0")

    lane_head = jnp.arange(MXU_WIDTH) // HEAD_DIM
    seg = (lane_head[:, None] == lane_head[None, :]).astype(BF16)
    xn_kv, xn_q = _rms(h, [row(norm_kv), row(norm_b[0])])
    k_rows, v_rows_kv, kz, vz = _kv_proj(
        xn_kv, w_kv.astype(BF16), seg, row(jnp.tile(k_norm, N_KV_HEADS)), bm=KV_TILE)
    q = _q_proj(xn_q, w_q, 0, seg, row(jnp.tile(q_norm[0], MXU_WIDTH // HEAD_DIM)),
                bm=WIDE_TILE, bn=MXU_WIDTH)
    sink = sinks[0].astype(F32)
    o = _attn(q, kz, vz, _lane_pair_copies(cache_k), _lane_pair_copies(cache_v), sink,
              m_prompt=m_prompt, seq=seq, dec_seq=dec_seq)
    h = _mm_res(o, w_o, 0, h, bm=WIDE_TILE, bn=MXU_WIDTH, name="attn_out")
    y_prompt, y_sample = ffn(h, 1, "ffn_down_1", split_rows=m_prompt)

    heads = lambda t: t.reshape(t.shape[:-1] + (N_KV_HEADS, HEAD_DIM))
    tail = lambda t: heads(t[:m_prompt].reshape(batch, seq, KV_WIDTH)[:, seq - WINDOW:])
    fresh = lambda t: heads(t[m_prompt:].reshape(streams, dec_seq, KV_WIDTH))
    k_p, v_p = tail(k_rows), tail(v_rows_kv)
    k_s, v_s = fresh(k_rows), fresh(v_rows_kv)
    return (y_prompt.reshape(batch, seq, d), y_sample.reshape(streams, dec_seq, d),
            k_p, v_p, k_s, v_s, v_rows.reshape(1, streams, dec_seq, d))
```

```python
import functools

import jax
import jax.numpy as jnp
from jax import lax
from jax.experimental import pallas as pl
from jax.experimental.pallas import tpu as pltpu

F32 = jnp.float32
BF16 = jnp.bfloat16

D_MODEL = 4096
SGU_CHUNK = 128
SGU_GROUPS = 8
SGU_GROUP_WIDTH = D_MODEL // SGU_GROUPS
CHUNK = 64
HEAD_DIM = 64
N_HEADS = 64
N_KV_HEADS = 8
GQA = N_HEADS // N_KV_HEADS
KV_WIDTH = N_KV_HEADS * HEAD_DIM
WINDOW = 128
NEG = -1e30
RMS_EPS = 1e-6
LN_EPS = 1e-5

V7X_VMEM_BYTES = 64 * 1024 * 1024
VMEM_LIMIT_CAP = V7X_VMEM_BYTES - 6 * 1024 * 1024
LANES = 128
MXU_WIDTH = 256
HEADS_PER_VREG = LANES // HEAD_DIM
PAIRS = GQA // HEADS_PER_VREG

WIDE_TILE = 1408
KV_TILE = WIDE_TILE // 2
NORM_ROWS = 256
ATT_ROWS = 128
RMS_STAT_ROWS = 64
RMS_SCALE_ROWS = 32
FFN_DOWN_PARTS = 2
ROW_GROUPS = 2


def _params(vmem_bytes, n_axes):
    limit = min(int(vmem_bytes) + 8 * 1024 * 1024, VMEM_LIMIT_CAP)
    return pltpu.CompilerParams(
        dimension_semantics=("arbitrary",) * n_axes, vmem_limit_bytes=limit)


def _row_groups(rows):
    size = rows // ROW_GROUPS
    assert size * ROW_GROUPS == rows and size % 16 == 0
    return [slice(g * size, (g + 1) * size) for g in range(ROW_GROUPS)]


def _layer_spec(k, bn, layer, col_of, row_block=0):
    return pl.BlockSpec((None, k, bn), lambda *idx: (layer, row_block, col_of(*idx)))


def _rms_rows(x_ref, g_ref, xn_ref, r_ref):
    rows, k = x_ref.shape

    def stats(i, carry):
        sl = pl.ds(pl.multiple_of(i * RMS_STAT_ROWS, RMS_STAT_ROWS), RMS_STAT_ROWS)
        x = x_ref[sl, :]
        ms = jnp.mean(x * x, axis=-1, keepdims=True)
        r_ref[sl, :] = jnp.broadcast_to(lax.rsqrt(ms + RMS_EPS), (RMS_STAT_ROWS, LANES))
        return carry

    lax.fori_loop(0, rows // RMS_STAT_ROWS, stats, 0)
    g = g_ref[...]

    def scale(i, carry):
        sl = pl.ds(pl.multiple_of(i * RMS_SCALE_ROWS, RMS_SCALE_ROWS), RMS_SCALE_ROWS)
        r = jnp.tile(r_ref[sl, :], (1, k // LANES))
        xn_ref[sl, :] = (x_ref[sl, :] * r * g).astype(BF16)
        return carry

    lax.fori_loop(0, rows // RMS_SCALE_ROWS, scale, 0)


def _rms_stack_kernel(xp_ref, xs_ref, g_ref, xn_ref, r_ref, *, prompt_blocks):
    from_ref = lambda x_ref: _rms_rows(x_ref, g_ref, xn_ref, r_ref)
    pl.when(pl.program_id(0) < prompt_blocks)(lambda: from_ref(xp_ref))
    pl.when(pl.program_id(0) >= prompt_blocks)(lambda: from_ref(xs_ref))


def _rms_stack(x_prompt, x_sample, gain):
    mp, k = x_prompt.shape
    ms = x_sample.shape[0]
    assert mp % NORM_ROWS == 0 and ms == NORM_ROWS
    prompt_blocks = mp // NORM_ROWS
    m = mp + ms
    vmem = 4 * NORM_ROWS * k * 4 + 2 * NORM_ROWS * k * 2 + 4 * RMS_STAT_ROWS * k * 4
    return pl.pallas_call(
        functools.partial(_rms_stack_kernel, prompt_blocks=prompt_blocks),
        grid=(m // NORM_ROWS,),
        in_specs=[
            pl.BlockSpec((NORM_ROWS, k), lambda i: (jnp.minimum(i, prompt_blocks - 1), 0)),
            pl.BlockSpec((NORM_ROWS, k), lambda i: (0, 0)),
            pl.BlockSpec((1, k), lambda i: (0, 0)),
        ],
        out_specs=pl.BlockSpec((NORM_ROWS, k), lambda i: (i, 0)),
        out_shape=jax.ShapeDtypeStruct((m, k), BF16),
        scratch_shapes=[pltpu.VMEM((NORM_ROWS, LANES), F32)],
        compiler_params=_params(vmem, 1),
        name="rms_stack",
    )(x_prompt, x_sample, gain)


def _head_rms(y, seg2_ref, gain_row):
    y2 = y * y
    hi = y2.astype(BF16)
    lo = (y2 - hi.astype(F32)).astype(BF16)
    seg2 = seg2_ref[...]
    w = seg2.shape[1]
    ssq = jnp.concatenate(
        [jnp.dot(jnp.concatenate([hi[:, c:c + w], lo[:, c:c + w]], axis=1), seg2,
                 preferred_element_type=F32)
         for c in range(0, y.shape[1], w)], axis=-1)
    return y * lax.rsqrt(ssq * (1.0 / HEAD_DIM) + RMS_EPS) * gain_row


def _row_factor(ssq_ref, k, n):
    r = lax.rsqrt(ssq_ref[...] * (1.0 / k) + RMS_EPS)
    return jnp.tile(r, (1, n // LANES))


def _sgu_in_kernel(x_ref, w_ref, o_ref):
    w = w_ref[...].astype(BF16)
    for rows in _row_groups(x_ref.shape[0]):
        y = jnp.dot(x_ref[rows, :], w, preferred_element_type=F32)
        o_ref[rows, :] = 0.5 * y * (1.0 + lax.erf(y * (0.5 ** 0.5)))


def _sgu_in(xn, w, layer, *, bm, bn):
    m, k = xn.shape
    n = w.shape[2]
    wb = w.dtype.itemsize
    vmem = 2 * bm * k * 2 + k * bn * (2 * wb + 2) + 5 * bm * bn * 4
    return pl.pallas_call(
        _sgu_in_kernel,
        grid=(m // bm, n // bn),
        in_specs=[
            pl.BlockSpec((bm, k), lambda i, j: (i, 0)),
            _layer_spec(k, bn, layer, lambda i, j: j),
        ],
        out_specs=pl.BlockSpec((bm, bn), lambda i, j: (i, j)),
        out_shape=jax.ShapeDtypeStruct((m, n), F32),
        compiler_params=_params(vmem, 2),
        name="sgu_in",
    )(xn, w)


def _sgu_mix_rows(u_ref, v_ref, lng_ref, lnb_ref, ws_ref, bs_ref, a_ref, vn_ref, *, chunk):
    ln_g = lng_ref[...]
    ln_b = lnb_ref[...]
    for c in range(u_ref.shape[0] // chunk):
        rows = slice(c * chunk, (c + 1) * chunk)
        v = v_ref[rows, :]
        mu = jnp.mean(v, axis=-1, keepdims=True)
        vc = v - mu
        var = jnp.mean(vc * vc, axis=-1, keepdims=True)
        vn = vc * lax.rsqrt(var + LN_EPS) * ln_g + ln_b
        if vn_ref is not None:
            vn_ref[rows, :] = vn
        vnb = vn.astype(BF16)
        for g in range(SGU_GROUPS):
            cols = slice(g * SGU_GROUP_WIDTH, (g + 1) * SGU_GROUP_WIDTH)
            mixed = jnp.dot(ws_ref[g], vnb[:, cols], preferred_element_type=F32)
            mixed = mixed + bs_ref[:, g:g + 1]
            a_ref[rows, cols] = (u_ref[rows, cols] * mixed).astype(BF16)


def _sgu_mix_kernel(u_ref, v_ref, lng_ref, lnb_ref, ws_ref, bs_ref, wss_ref, bss_ref,
                    a_ref, vn_ref, *, prompt_blocks, sample_chunk):
    common = (u_ref, v_ref, lng_ref, lnb_ref)
    is_prompt = pl.program_id(0) < prompt_blocks
    pl.when(is_prompt)(
        lambda: _sgu_mix_rows(*common, ws_ref, bs_ref, a_ref, None, chunk=SGU_CHUNK))
    pl.when(jnp.logical_not(is_prompt))(
        lambda: _sgu_mix_rows(*common, wss_ref, bss_ref, a_ref, vn_ref, chunk=sample_chunk))


def _sgu_mix(uv, ln_g, ln_b, ws, bs_t, *, m_prompt, sample_chunk):
    m = uv.shape[0]
    d = D_MODEL
    rows = m - m_prompt
    assert m_prompt % rows == 0 and rows % SGU_CHUNK == 0 and rows % sample_chunk == 0
    prompt_blocks = m_prompt // rows
    const = lambda shape: pl.BlockSpec(shape, lambda i: (0,) * len(shape))
    wss, bss = ws[:, :sample_chunk, :sample_chunk], bs_t[:sample_chunk]
    vmem = 2 * 2 * rows * d * 4 + 2 * rows * d * 2 + 2 * rows * d * 4 + 6 * SGU_CHUNK * d * 4
    return pl.pallas_call(
        functools.partial(_sgu_mix_kernel, prompt_blocks=prompt_blocks,
                          sample_chunk=sample_chunk),
        grid=(prompt_blocks + 1,),
        in_specs=[
            pl.BlockSpec((rows, d), lambda i: (i, 0)),
            pl.BlockSpec((rows, d), lambda i: (i, 1)),
            const((1, d)), const((1, d)),
            const(ws.shape), const(bs_t.shape), const(wss.shape), const(bss.shape),
        ],
        out_specs=[pl.BlockSpec((rows, d), lambda i: (i, 0)), const((rows, d))],
        out_shape=[jax.ShapeDtypeStruct((m, d), BF16), jax.ShapeDtypeStruct((rows, d), F32)],
        compiler_params=_params(vmem, 1),
        name="sgu_mix",
    )(uv, uv, ln_g, ln_b, ws, bs_t, wss, bss)


def _residual_kernel(x_ref, w_ref, *refs, n_res, n_gains, top_rows, tail_rows):
    res_refs, refs = refs[:n_res], refs[n_res:]
    gain_refs, refs = refs[:n_gains], refs[n_gains:]
    n_main = 2 if tail_rows else 1
    main_refs, refs = refs[:n_main], refs[n_main:]
    scaled_refs, ssq_refs = refs[:n_gains], refs[n_gains:]
    o_ref = main_refs[0]
    w = w_ref[...].astype(BF16)
    bm = x_ref.shape[0]
    squares = []
    plain = n_res == 1 and n_gains == 0
    for rows in (_row_groups(bm) if plain else [slice(0, bm)]):
        y = jnp.dot(x_ref[rows, :], w, preferred_element_type=F32)
        if n_res == 1:
            h = res_refs[0][rows, :] + y
            o_ref[rows, :] = h
        else:
            top_ref, tail_ref = res_refs
            is_last = pl.program_id(0) == pl.num_programs(0) - 1
            o_ref[:top_rows, :] = top_ref[:top_rows, :] + y[:top_rows]

            @pl.when(jnp.logical_not(is_last))
            def _():
                o_ref[top_rows:, :] = top_ref[top_rows:, :] + y[top_rows:]

            @pl.when(is_last)
            def _():
                o_ref[top_rows:, :] = tail_ref[...] + y[top_rows:]

            h = o_ref[...]
        for g_ref, s_ref in zip(gain_refs, scaled_refs):
            s_ref[rows, :] = (h * g_ref[...]).astype(BF16)
        if n_gains:
            squares.append(jnp.sum(h * h, axis=-1, keepdims=True))
    if tail_rows:
        main_refs[1][...] = o_ref[bm - tail_rows:, :]
    if n_gains:
        (ssq_ref,) = ssq_refs
        part = jnp.broadcast_to(jnp.concatenate(squares, axis=0), ssq_ref.shape)
        first = pl.program_id(1) == 0

        @pl.when(first)
        def _():
            ssq_ref[...] = part

        @pl.when(jnp.logical_not(first))
        def _():
            ssq_ref[...] += part


def _residual(x, w, layer, res, *, bm, bn, name, part=(0, 1), gains=(), split_rows=None):
    m, k = x.shape
    n = w.shape[2]
    p, n_parts = part
    kc = k // n_parts
    assert kc * n_parts == k and kc % LANES == 0
    wb = w.dtype.itemsize
    tile = pl.BlockSpec((bm, bn), lambda i, j: (i, j))
    last = m // bm - 1

    top_rows = 0
    if isinstance(res, tuple):
        top, tail = res
        top_rows = bm - tail.shape[0]
        assert top.shape[0] + tail.shape[0] == m and top_rows > 0 and top_rows % 8 == 0
        res_specs = [tile, pl.BlockSpec((tail.shape[0], bn), lambda i, j: (0, j))]
        res_args = [top, tail]
    else:
        res_specs, res_args = [tile], [res]

    tail_rows = 0
    if split_rows is None:
        out_specs, out_shape = [tile], [jax.ShapeDtypeStruct((m, n), F32)]
    else:
        tail_rows = m - split_rows
        assert last * bm < split_rows and tail_rows <= bm
        out_specs = [tile, pl.BlockSpec((tail_rows, bn),
                                        lambda i, j: (0, jnp.where(i == last, j, 0)))]
        out_shape = [jax.ShapeDtypeStruct((split_rows, n), F32),
                     jax.ShapeDtypeStruct((tail_rows, n), F32)]
    out_specs += [tile] * len(gains)
    out_shape += [jax.ShapeDtypeStruct((m, n), BF16)] * len(gains)
    if gains:
        out_specs.append(pl.BlockSpec((bm, LANES), lambda i, j: (i, 0)))
        out_shape.append(jax.ShapeDtypeStruct((m, LANES), F32))

    vmem = 2 * bm * kc * 2 + kc * bn * (2 * wb + 2) + (8 + 2 * len(gains)) * bm * bn * 4
    return pl.pallas_call(
        functools.partial(_residual_kernel, n_res=len(res_args), n_gains=len(gains),
                          top_rows=top_rows, tail_rows=tail_rows),
        grid=(m // bm, n // bn),
        in_specs=[
            pl.BlockSpec((bm, kc), lambda i, j: (i, p)),
            _layer_spec(kc, bn, layer, lambda i, j: j, row_block=p),
        ] + res_specs + [pl.BlockSpec((1, bn), lambda i, j: (0, j))] * len(gains),
        out_specs=out_specs,
        out_shape=out_shape,
        compiler_params=_params(vmem, 2),
        name=name,
    )(x, w, *res_args, *gains)


def _ffn_up_kernel(x_ref, ssq_ref, wg_ref, wu_ref, o_ref):
    wg = wg_ref[...].astype(BF16)
    wu = wu_ref[...].astype(BF16)
    k = x_ref.shape[1]
    for rows in _row_groups(x_ref.shape[0]):
        x = x_ref[rows, :]
        r = _row_factor(ssq_ref.at[rows, :], k, o_ref.shape[1])
        gate = jnp.dot(x, wg, preferred_element_type=F32) * r
        up = jnp.dot(x, wu, preferred_element_type=F32) * r
        o_ref[rows, :] = (gate * jax.nn.sigmoid(gate) * up).astype(BF16)


def _ffn_up(x, ssq, w_gate, w_up, layer, *, bm, bn):
    m, k = x.shape
    n = w_gate.shape[2]
    wb = w_gate.dtype.itemsize
    vmem = 2 * bm * k * 2 + 2 * k * bn * (2 * wb + 2) + 2 * bm * bn * 2 + 5 * bm * bn * 4
    return pl.pallas_call(
        _ffn_up_kernel,
        grid=(m // bm, n // bn),
        in_specs=[
            pl.BlockSpec((bm, k), lambda i, j: (i, 0)),
            pl.BlockSpec((bm, LANES), lambda i, j: (i, 0)),
            _layer_spec(k, bn, layer, lambda i, j: j),
            _layer_spec(k, bn, layer, lambda i, j: j),
        ],
        out_specs=pl.BlockSpec((bm, bn), lambda i, j: (i, j)),
        out_shape=jax.ShapeDtypeStruct((m, n), BF16),
        compiler_params=_params(vmem, 2),
        name="ffn_up",
    )(x, ssq, w_gate, w_up)


def _q_proj_kernel(x_ref, ssq_ref, w_ref, seg2_ref, qg_ref, q_ref):
    x = x_ref[...]
    y = jnp.dot(x, w_ref[...].astype(BF16), preferred_element_type=F32)
    y = y * _row_factor(ssq_ref, x.shape[1], y.shape[1])
    q_ref[...] = (_head_rms(y, seg2_ref, qg_ref[...]) * HEAD_DIM ** -0.5).astype(BF16)


def _q_proj(x, ssq, w, layer, seg2, q_gain, *, bm, bn):
    m, k = x.shape
    n = w.shape[2]
    wb = w.dtype.itemsize
    vmem = 2 * bm * k * 2 + k * bn * (2 * wb + 2) + 10 * bm * bn * 4
    return pl.pallas_call(
        _q_proj_kernel,
        grid=(m // bm, n // bn),
        in_specs=[
            pl.BlockSpec((bm, k), lambda i, j: (i, 0)),
            pl.BlockSpec((bm, LANES), lambda i, j: (i, 0)),
            _layer_spec(k, bn, layer, lambda i, j: j),
            pl.BlockSpec(seg2.shape, lambda i, j: (0, 0)),
            pl.BlockSpec((1, bn), lambda i, j: (0, 0)),
        ],
        out_specs=pl.BlockSpec((bm, bn), lambda i, j: (i, j)),
        out_shape=jax.ShapeDtypeStruct((m, n), BF16),
        compiler_params=_params(vmem, 2),
        name="q_proj",
    )(x, ssq, w, seg2, q_gain)


def _kv_proj_kernel(x_ref, ssq_ref, w_ref, seg2_ref, kg_ref, kf_ref, vf_ref, kz_ref, vz_ref):
    x = x_ref[...]
    y = jnp.dot(x, w_ref[...], preferred_element_type=F32)
    y = y * _row_factor(ssq_ref, x.shape[1], y.shape[1])
    kn = _head_rms(y[:, :KV_WIDTH], seg2_ref, kg_ref[...])
    v = y[:, KV_WIDTH:]
    kf_ref[...] = kn
    vf_ref[...] = v
    low = lax.broadcasted_iota(jnp.int32, (y.shape[0], LANES), 1) < HEAD_DIM
    for src, dst in ((kn, kz_ref), (v, vz_ref)):
        for p in range(N_KV_HEADS // HEADS_PER_VREG):
            both = src[:, p * LANES:(p + 1) * LANES]
            swapped = pltpu.roll(both, HEAD_DIM, axis=1)
            dst[2 * p, 0] = jnp.where(low, both, 0.0).astype(BF16)
            dst[2 * p, 1] = jnp.where(low, 0.0, swapped).astype(BF16)
            dst[2 * p + 1, 0] = jnp.where(low, swapped, 0.0).astype(BF16)
            dst[2 * p + 1, 1] = jnp.where(low, 0.0, both).astype(BF16)


def _kv_proj(x, ssq, w, seg2, k_gain, *, bm):
    m, k = x.shape
    n = w.shape[1]
    vmem = (2 * bm * k * 2 + 2 * k * n * 2 + 8 * bm * n * 4
            + 2 * 2 * N_KV_HEADS * 2 * bm * LANES * 2)
    z_spec = pl.BlockSpec((N_KV_HEADS, 2, bm, LANES), lambda i: (0, 0, i, 0))
    row_spec = pl.BlockSpec((bm, KV_WIDTH), lambda i: (i, 0))
    z_shape = jax.ShapeDtypeStruct((N_KV_HEADS, 2, m, LANES), BF16)
    return pl.pallas_call(
        _kv_proj_kernel,
        grid=(m // bm,),
        in_specs=[
            pl.BlockSpec((bm, k), lambda i: (i, 0)),
            pl.BlockSpec((bm, LANES), lambda i: (i, 0)),
            pl.BlockSpec((k, n), lambda i: (0, 0)),
            pl.BlockSpec(seg2.shape, lambda i: (0, 0)),
            pl.BlockSpec((1, KV_WIDTH), lambda i: (0, 0)),
        ],
        out_specs=[row_spec, row_spec, z_spec, z_spec],
        out_shape=[
            jax.ShapeDtypeStruct((m, KV_WIDTH), F32),
            jax.ShapeDtypeStruct((m, KV_WIDTH), F32),
            z_shape, z_shape,
        ],
        compiler_params=_params(vmem, 1),
        name="kv_proj",
    )(x, ssq, w, seg2, k_gain)


def _attend(q_ref, o_ref, sink_ref, key_parts, val_parts, *, row0, tq, bias_t=None):
    rows = PAIRS * tq
    lane_pair = lax.broadcasted_iota(jnp.int32, (1, rows), 1) >> (tq.bit_length() - 1)
    for kh in range(N_KV_HEADS):
        base = kh * GQA * HEAD_DIM
        q4 = jnp.concatenate(
            [q_ref[row0:row0 + tq, base + p * LANES:base + (p + 1) * LANES]
             for p in range(PAIRS)], axis=0)
        kk = jnp.concatenate(key_parts(kh, 0) + key_parts(kh, 1), axis=0)
        vv = jnp.concatenate(val_parts(kh, 0) + val_parts(kh, 1), axis=0)
        n_keys = kk.shape[0] // HEADS_PER_VREG
        s_t = lax.dot_general(kk, q4, (((1,), (1,)), ((), ())),
                              preferred_element_type=F32)
        if bias_t is not None:
            s_t = s_t + bias_t
        exps, denoms = [], []
        for parity in range(HEADS_PER_VREG):
            head = kh * GQA + parity
            sink = jnp.full((1, rows), sink_ref[head + HEADS_PER_VREG * (PAIRS - 1)], F32)
            for p in range(PAIRS - 1):
                sink = jnp.where(lane_pair == p, sink_ref[head + HEADS_PER_VREG * p], sink)
            half = s_t[parity * n_keys:(parity + 1) * n_keys]
            mx = jnp.maximum(jnp.max(half, axis=0, keepdims=True), sink)
            e = jnp.exp(half - mx)
            denom = jnp.sum(e, axis=0, keepdims=True) + jnp.exp(sink - mx)
            denoms.append(jnp.broadcast_to(denom, (HEAD_DIM, rows)))
            exps.append(e.astype(BF16))
        o_t = lax.dot_general(vv, jnp.concatenate(exps, axis=0),
                              (((0,), (0,)), ((), ())), preferred_element_type=F32)
        o = (o_t / jnp.concatenate(denoms, axis=0)).T
        for p in range(PAIRS):
            o_ref[row0:row0 + tq, base + p * LANES:base + (p + 1) * LANES] = (
                o[p * tq:(p + 1) * tq].astype(BF16))


def _attend_prompt_block(q_ref, kp_ref, kc_ref, vp_ref, vc_ref, sink_ref, o_ref, first):
    tq = ATT_ROWS
    n_keys = 2 * tq
    chunk_shift = CHUNK.bit_length() - 1
    chunks_per_block = tq // CHUNK
    key = lax.broadcasted_iota(jnp.int32, (2 * n_keys, PAIRS * tq), 0)
    query = lax.broadcasted_iota(jnp.int32, (2 * n_keys, PAIRS * tq), 1)
    q_chunk = (query & (tq - 1)) >> chunk_shift
    k_chunk = ((key & (n_keys - 1)) >> chunk_shift) - chunks_per_block
    oldest = jnp.where(first, 0, -chunks_per_block)
    valid = (k_chunk <= q_chunk) & (k_chunk >= jnp.maximum(q_chunk - 2, oldest))
    bias = jnp.where(valid, 0.0, NEG)
    parts = lambda prev, cur: (lambda kh, parity: [prev[kh, parity], cur[kh, parity]])
    _attend(q_ref, o_ref, sink_ref, parts(kp_ref, kc_ref), parts(vp_ref, vc_ref),
            row0=0, tq=tq, bias_t=bias)


def _attend_sample_block(q_ref, ck_ref, cv_ref, kc_ref, vc_ref, sink_ref, o_ref, *, dec_seq):
    for st in range(ATT_ROWS // dec_seq):
        new = slice(st * dec_seq, (st + 1) * dec_seq)
        parts = lambda cache, cur, st=st, new=new: (
            lambda kh, parity: [cache[st, kh, parity], cur[kh, parity, new, :]])
        _attend(q_ref, o_ref, sink_ref, parts(ck_ref, kc_ref), parts(cv_ref, vc_ref),
                row0=st * dec_seq, tq=dec_seq)


def _attn_kernel(q_ref, kp_ref, kc_ref, vp_ref, vc_ref, ck_ref, cv_ref, sink_ref, o_ref, *,
                 prompt_blocks, blocks_per_seq, dec_seq):
    step = pl.program_id(0)
    is_prompt = step < prompt_blocks
    first = (step & (blocks_per_seq - 1)) == 0
    pl.when(is_prompt)(lambda: _attend_prompt_block(
        q_ref, kp_ref, kc_ref, vp_ref, vc_ref, sink_ref, o_ref, first))
    pl.when(jnp.logical_not(is_prompt))(lambda: _attend_sample_block(
        q_ref, ck_ref, cv_ref, kc_ref, vc_ref, sink_ref, o_ref, dec_seq=dec_seq))


def _attn(q, kz, vz, cache_kz, cache_vz, sinks, *, m_prompt, seq, dec_seq):
    m = q.shape[0]
    tq = ATT_ROWS
    blocks_per_seq = seq // tq
    prompt_blocks = m_prompt // tq
    streams_per_block = tq // dec_seq
    assert blocks_per_seq & (blocks_per_seq - 1) == 0 and (m - m_prompt) % tq == 0
    rows_here = lambda s: (s, 0)
    cur = lambda s: (0, 0, s, 0)
    prev = lambda s: (0, 0, jnp.where((s & (blocks_per_seq - 1)) == 0, s, s - 1), 0)
    cache = lambda s: (jnp.maximum(s - prompt_blocks, 0), 0, 0, 0, 0)
    z_block = (N_KV_HEADS, 2, tq, LANES)
    cache_block = (streams_per_block, N_KV_HEADS, 2, WINDOW, LANES)
    vmem = 4 * tq * D_MODEL * 2 + 8 * N_KV_HEADS * 2 * tq * LANES * 2 \
        + 4 * streams_per_block * N_KV_HEADS * 2 * WINDOW * LANES * 2 \
        + 10 * PAIRS * tq * 4 * tq * 4
    return pl.pallas_call(
        functools.partial(_attn_kernel, prompt_blocks=prompt_blocks,
                          blocks_per_seq=blocks_per_seq, dec_seq=dec_seq),
        grid=(m // tq,),
        in_specs=[
            pl.BlockSpec((tq, D_MODEL), rows_here),
            pl.BlockSpec(z_block, prev),
            pl.BlockSpec(z_block, cur),
            pl.BlockSpec(z_block, prev),
            pl.BlockSpec(z_block, cur),
            pl.BlockSpec(cache_block, cache),
            pl.BlockSpec(cache_block, cache),
            pl.BlockSpec(memory_space=pltpu.SMEM),
        ],
        out_specs=pl.BlockSpec((tq, D_MODEL), rows_here),
        out_shape=jax.ShapeDtypeStruct((m, D_MODEL), BF16),
        compiler_params=_params(vmem, 1),
        name="attn",
    )(q, kz, kz, vz, vz, cache_kz, cache_vz, sinks)


def _lane_pair_copies(cache):
    c = cache.transpose(0, 2, 1, 3).astype(BF16)
    z = jnp.zeros_like(c)
    return jnp.stack([jnp.concatenate([c, z], axis=-1), jnp.concatenate([z, c], axis=-1)], axis=2)


def kernel(x_prompt, x_sample, cache_k, cache_v, norm_a, w_sgu_in, sgu_ln_g, sgu_ln_b, w_sgu_s,
           b_sgu_s, w_sgu_out, norm_kv, w_kv, k_norm, norm_b, w_q, q_norm, sinks, w_o,
           norm_ffn, w_ffn_gate, w_ffn_up, w_ffn_down):
    batch, seq, d = x_prompt.shape
    streams, dec_seq, _ = x_sample.shape
    m_prompt = batch * seq
    m_sample = streams * dec_seq
    m = m_prompt + m_sample
    assert d == D_MODEL and seq % ATT_ROWS == 0
    assert m % WIDE_TILE == 0 and m % KV_TILE == 0
    assert norm_a.shape[0] == 1 and norm_b.shape[0] == 1 and norm_ffn.shape[0] == 2

    row = lambda g: g.reshape(1, -1).astype(F32)
    tiles = dict(bm=WIDE_TILE, bn=MXU_WIDTH)

    def ffn(h, scaled, ssq, layer, **last_part):
        hidden = _ffn_up(scaled, ssq, w_ffn_gate, w_ffn_up, layer, **tiles)
        for p in range(FFN_DOWN_PARTS):
            final = p == FFN_DOWN_PARTS - 1
            outs = _residual(hidden, w_ffn_down, layer, h, part=(p, FFN_DOWN_PARTS),
                             name="ffn_down_%d_%d" % (layer, p), **tiles,
                             **(last_part if final else {}))
            h = outs[0]
        return outs

    x_rows = (x_prompt.reshape(m_prompt, d), x_sample.reshape(m_sample, d))
    xn = _rms_stack(*x_rows, row(norm_a[0]))
    uv = _sgu_in(xn, w_sgu_in, 0, bm=WIDE_TILE, bn=2 * MXU_WIDTH)

    pos_chunk = jnp.arange(SGU_CHUNK) // CHUNK
    ws = jnp.where((pos_chunk[:, None] >= pos_chunk[None, :])[None], w_sgu_s[0], 0.0).astype(BF16)
    bs_t = b_sgu_s[0].T.astype(F32)
    ln_g, ln_b = row(sgu_ln_g[0]), row(sgu_ln_b[0])
    a, v_rows = _sgu_mix(uv, ln_g, ln_b, ws, bs_t, m_prompt=m_prompt, sample_chunk=dec_seq)
    h, scaled, ssq = _residual(a, w_sgu_out, 0, x_rows, name="sgu_out", **tiles,
                               gains=[row(norm_ffn[0])])
    h, scaled_kv, scaled_q, ssq = ffn(h, scaled, ssq, 0, gains=[row(norm_kv), row(norm_b[0])])

    lane_head = jnp.arange(MXU_WIDTH) // HEAD_DIM
    seg = (lane_head[:, None] == lane_head[None, :]).astype(BF16)
    seg2 = jnp.concatenate([seg, seg], axis=0)
    k_rows, v_rows_kv, kz, vz = _kv_proj(
        scaled_kv, ssq, w_kv.astype(BF16), seg2, row(jnp.tile(k_norm, N_KV_HEADS)), bm=KV_TILE)
    q = _q_proj(scaled_q, ssq, w_q, 0, seg2, row(jnp.tile(q_norm[0], MXU_WIDTH // HEAD_DIM)),
                **tiles)
    sink = sinks[0].astype(F32)
    o = _attn(q, kz, vz, _lane_pair_copies(cache_k), _lane_pair_copies(cache_v), sink,
              m_prompt=m_prompt, seq=seq, dec_seq=dec_seq)
    h, scaled, ssq = _residual(o, w_o, 0, h, name="attn_out", **tiles,
                               gains=[row(norm_ffn[1])])
    y_prompt, y_sample = ffn(h, scaled, ssq, 1, split_rows=m_prompt)

    heads = lambda t: t.reshape(t.shape[:-1] + (N_KV_HEADS, HEAD_DIM))
    tail = lambda t: heads(t[:m_prompt].reshape(batch, seq, KV_WIDTH)[:, seq - WINDOW:])
    fresh = lambda t: heads(t[m_prompt:].reshape(streams, dec_seq, KV_WIDTH))
    k_p, v_p = tail(k_rows), tail(v_rows_kv)
    k_s, v_s = fresh(k_rows), fresh(v_rows_kv)
    return (y_prompt.reshape(batch, seq, d), y_sample.reshape(streams, dec_seq, d),
            k_p, v_p, k_s, v_s, v_rows.reshape(1, streams, dec_seq, d))
```

```python
import functools

import jax
import jax.numpy as jnp
from jax import lax
from jax.experimental import pallas as pl
from jax.experimental.pallas import tpu as pltpu

F32 = jnp.float32
BF16 = jnp.bfloat16

D_MODEL = 4096
SGU_CHUNK = 128
SGU_GROUPS = 8
SGU_GROUP_WIDTH = D_MODEL // SGU_GROUPS
CHUNK = 64
HEAD_DIM = 64
N_HEADS = 64
N_KV_HEADS = 8
GQA = N_HEADS // N_KV_HEADS
KV_WIDTH = N_KV_HEADS * HEAD_DIM
WINDOW = 128
NEG = -1e30
RMS_EPS = 1e-6
LN_EPS = 1e-5

V7X_VMEM_BYTES = 64 * 1024 * 1024
VMEM_LIMIT_CAP = V7X_VMEM_BYTES - 6 * 1024 * 1024
LANES = 128
MXU_WIDTH = 256
HEADS_PER_VREG = LANES // HEAD_DIM
PAIRS = GQA // HEADS_PER_VREG

WIDE_TILE = 1408
KV_TILE = WIDE_TILE // 2
NORM_ROWS = 256
ATT_ROWS = 128
RMS_STAT_ROWS = 64
RMS_SCALE_ROWS = 32
FFN_DOWN_PARTS = 2
ROW_GROUPS = 2


def _params(vmem_bytes, n_axes):
    limit = min(int(vmem_bytes) + 8 * 1024 * 1024, VMEM_LIMIT_CAP)
    return pltpu.CompilerParams(
        dimension_semantics=("arbitrary",) * n_axes, vmem_limit_bytes=limit)


def _row_groups(rows):
    size = rows // ROW_GROUPS
    assert size * ROW_GROUPS == rows and size % 16 == 0
    return [slice(g * size, (g + 1) * size) for g in range(ROW_GROUPS)]


_LHS_SPEC = pl.BlockSpec(memory_space=pl.ANY)


def _lhs_scratch(bm, kc):
    return [pltpu.VMEM((2, bm, kc), BF16), pltpu.SemaphoreType.DMA((2,))]


def _lhs_tile(x_hbm, x_buf, sem, col_block=0):
    _, bm, kc = x_buf.shape
    i, n_tiles = pl.program_id(0), pl.num_programs(0)
    slot = lax.rem(i, 2)

    def copy(tile, slot):
        src = x_hbm.at[pl.ds(tile * bm, bm), pl.ds(col_block * kc, kc)]
        return pltpu.make_async_copy(src, x_buf.at[slot], sem.at[slot])

    @pl.when(pl.program_id(1) == 0)
    def _():
        @pl.when(i == 0)
        def _():
            copy(0, 0).start()

        copy(i, slot).wait()

        @pl.when(i + 1 < n_tiles)
        def _():
            copy(i + 1, 1 - slot).start()

    return x_buf.at[slot]


def _layer_spec(k, bn, layer, col_of, row_block=0):
    return pl.BlockSpec((None, k, bn), lambda *idx: (layer, row_block, col_of(*idx)))


def _rms_rows(x_ref, g_ref, xn_ref, r_ref):
    rows, k = x_ref.shape

    def stats(i, carry):
        sl = pl.ds(pl.multiple_of(i * RMS_STAT_ROWS, RMS_STAT_ROWS), RMS_STAT_ROWS)
        x = x_ref[sl, :]
        ms = jnp.mean(x * x, axis=-1, keepdims=True)
        r_ref[sl, :] = jnp.broadcast_to(lax.rsqrt(ms + RMS_EPS), (RMS_STAT_ROWS, LANES))
        return carry

    lax.fori_loop(0, rows // RMS_STAT_ROWS, stats, 0)
    g = g_ref[...]

    def scale(i, carry):
        sl = pl.ds(pl.multiple_of(i * RMS_SCALE_ROWS, RMS_SCALE_ROWS), RMS_SCALE_ROWS)
        r = jnp.tile(r_ref[sl, :], (1, k // LANES))
        xn_ref[sl, :] = (x_ref[sl, :] * r * g).astype(BF16)
        return carry

    lax.fori_loop(0, rows // RMS_SCALE_ROWS, scale, 0)


def _rms_stack_kernel(xp_ref, xs_ref, g_ref, xn_ref, r_ref, *, prompt_blocks):
    from_ref = lambda x_ref: _rms_rows(x_ref, g_ref, xn_ref, r_ref)
    pl.when(pl.program_id(0) < prompt_blocks)(lambda: from_ref(xp_ref))
    pl.when(pl.program_id(0) >= prompt_blocks)(lambda: from_ref(xs_ref))


def _rms_stack(x_prompt, x_sample, gain):
    mp, k = x_prompt.shape
    ms = x_sample.shape[0]
    assert mp % NORM_ROWS == 0 and ms == NORM_ROWS
    prompt_blocks = mp // NORM_ROWS
    m = mp + ms
    vmem = 4 * NORM_ROWS * k * 4 + 2 * NORM_ROWS * k * 2 + 4 * RMS_STAT_ROWS * k * 4
    return pl.pallas_call(
        functools.partial(_rms_stack_kernel, prompt_blocks=prompt_blocks),
        grid=(m // NORM_ROWS,),
        in_specs=[
            pl.BlockSpec((NORM_ROWS, k), lambda i: (jnp.minimum(i, prompt_blocks - 1), 0)),
            pl.BlockSpec((NORM_ROWS, k), lambda i: (0, 0)),
            pl.BlockSpec((1, k), lambda i: (0, 0)),
        ],
        out_specs=pl.BlockSpec((NORM_ROWS, k), lambda i: (i, 0)),
        out_shape=jax.ShapeDtypeStruct((m, k), BF16),
        scratch_shapes=[pltpu.VMEM((NORM_ROWS, LANES), F32)],
        compiler_params=_params(vmem, 1),
        name="rms_stack",
    )(x_prompt, x_sample, gain)


def _head_rms(y, seg2_ref, gain_row):
    y2 = y * y
    hi = y2.astype(BF16)
    lo = (y2 - hi.astype(F32)).astype(BF16)
    seg2 = seg2_ref[...]
    w = seg2.shape[1]
    ssq = jnp.concatenate(
        [jnp.dot(jnp.concatenate([hi[:, c:c + w], lo[:, c:c + w]], axis=1), seg2,
                 preferred_element_type=F32)
         for c in range(0, y.shape[1], w)], axis=-1)
    return y * lax.rsqrt(ssq * (1.0 / HEAD_DIM) + RMS_EPS) * gain_row


def _row_factor(ssq_ref, k, n):
    r = lax.rsqrt(ssq_ref[...] * (1.0 / k) + RMS_EPS)
    return jnp.tile(r, (1, n // LANES))


def _sgu_in_kernel(x_hbm, w_ref, o_ref, x_buf, sem):
    x_ref = _lhs_tile(x_hbm, x_buf, sem)
    w = w_ref[...].astype(BF16)
    for rows in _row_groups(x_ref.shape[0]):
        y = jnp.dot(x_ref[rows, :], w, preferred_element_type=F32)
        o_ref[rows, :] = 0.5 * y * (1.0 + lax.erf(y * (0.5 ** 0.5)))


def _sgu_in(xn, w, layer, *, bm, bn):
    m, k = xn.shape
    n = w.shape[2]
    wb = w.dtype.itemsize
    vmem = 2 * bm * k * 2 + k * bn * (2 * wb + 2) + 5 * bm * bn * 4
    return pl.pallas_call(
        _sgu_in_kernel,
        grid=(m // bm, n // bn),
        in_specs=[_LHS_SPEC, _layer_spec(k, bn, layer, lambda i, j: j)],
        out_specs=pl.BlockSpec((bm, bn), lambda i, j: (i, j)),
        out_shape=jax.ShapeDtypeStruct((m, n), F32),
        scratch_shapes=_lhs_scratch(bm, k),
        compiler_params=_params(vmem, 2),
        name="sgu_in",
    )(xn, w)


def _sgu_mix_rows(u_ref, v_ref, lng_ref, lnb_ref, ws_ref, bs_ref, a_ref, vn_ref, *, chunk):
    ln_g = lng_ref[...]
    ln_b = lnb_ref[...]
    for c in range(u_ref.shape[0] // chunk):
        rows = slice(c * chunk, (c + 1) * chunk)
        v = v_ref[rows, :]
        mu = jnp.mean(v, axis=-1, keepdims=True)
        vc = v - mu
        var = jnp.mean(vc * vc, axis=-1, keepdims=True)
        vn = vc * lax.rsqrt(var + LN_EPS) * ln_g + ln_b
        if vn_ref is not None:
            vn_ref[rows, :] = vn
        vnb = vn.astype(BF16)
        for g in range(SGU_GROUPS):
            cols = slice(g * SGU_GROUP_WIDTH, (g + 1) * SGU_GROUP_WIDTH)
            mixed = jnp.dot(ws_ref[g], vnb[:, cols], preferred_element_type=F32)
            mixed = mixed + bs_ref[:, g:g + 1]
            a_ref[rows, cols] = (u_ref[rows, cols] * mixed).astype(BF16)


def _sgu_mix_kernel(u_ref, v_ref, lng_ref, lnb_ref, ws_ref, bs_ref, wss_ref, bss_ref,
                    a_ref, vn_ref, *, prompt_blocks, sample_chunk):
    common = (u_ref, v_ref, lng_ref, lnb_ref)
    is_prompt = pl.program_id(0) < prompt_blocks
    pl.when(is_prompt)(
        lambda: _sgu_mix_rows(*common, ws_ref, bs_ref, a_ref, None, chunk=SGU_CHUNK))
    pl.when(jnp.logical_not(is_prompt))(
        lambda: _sgu_mix_rows(*common, wss_ref, bss_ref, a_ref, vn_ref, chunk=sample_chunk))


def _sgu_mix(uv, ln_g, ln_b, ws, bs_t, *, m_prompt, sample_chunk):
    m = uv.shape[0]
    d = D_MODEL
    rows = m - m_prompt
    assert m_prompt % rows == 0 and rows % SGU_CHUNK == 0 and rows % sample_chunk == 0
    prompt_blocks = m_prompt // rows
    const = lambda shape: pl.BlockSpec(shape, lambda i: (0,) * len(shape))
    wss, bss = ws[:, :sample_chunk, :sample_chunk], bs_t[:sample_chunk]
    vmem = 2 * 2 * rows * d * 4 + 2 * rows * d * 2 + 2 * rows * d * 4 + 6 * SGU_CHUNK * d * 4
    return pl.pallas_call(
        functools.partial(_sgu_mix_kernel, prompt_blocks=prompt_blocks,
                          sample_chunk=sample_chunk),
        grid=(prompt_blocks + 1,),
        in_specs=[
            pl.BlockSpec((rows, d), lambda i: (i, 0)),
            pl.BlockSpec((rows, d), lambda i: (i, 1)),
            const((1, d)), const((1, d)),
            const(ws.shape), const(bs_t.shape), const(wss.shape), const(bss.shape),
        ],
        out_specs=[pl.BlockSpec((rows, d), lambda i: (i, 0)), const((rows, d))],
        out_shape=[jax.ShapeDtypeStruct((m, d), BF16), jax.ShapeDtypeStruct((rows, d), F32)],
        compiler_params=_params(vmem, 1),
        name="sgu_mix",
    )(uv, uv, ln_g, ln_b, ws, bs_t, wss, bss)


def _residual_kernel(x_hbm, w_ref, *refs, n_res, n_gains, top_rows, tail_rows, col_block):
    x_ref = _lhs_tile(x_hbm, *refs[-2:], col_block=col_block)
    refs = refs[:-2]
    res_refs, refs = refs[:n_res], refs[n_res:]
    gain_refs, refs = refs[:n_gains], refs[n_gains:]
    n_main = 2 if tail_rows else 1
    main_refs, refs = refs[:n_main], refs[n_main:]
    scaled_refs, ssq_refs = refs[:n_gains], refs[n_gains:]
    o_ref = main_refs[0]
    w = w_ref[...].astype(BF16)
    bm = x_ref.shape[0]
    squares = []
    plain = n_res == 1 and n_gains == 0
    for rows in (_row_groups(bm) if plain else [slice(0, bm)]):
        y = jnp.dot(x_ref[rows, :], w, preferred_element_type=F32)
        if n_res == 1:
            h = res_refs[0][rows, :] + y
            o_ref[rows, :] = h
        else:
            top_ref, tail_ref = res_refs
            is_last = pl.program_id(0) == pl.num_programs(0) - 1
            o_ref[:top_rows, :] = top_ref[:top_rows, :] + y[:top_rows]

            @pl.when(jnp.logical_not(is_last))
            def _():
                o_ref[top_rows:, :] = top_ref[top_rows:, :] + y[top_rows:]

            @pl.when(is_last)
            def _():
                o_ref[top_rows:, :] = tail_ref[...] + y[top_rows:]

            h = o_ref[...]
        for g_ref, s_ref in zip(gain_refs, scaled_refs):
            s_ref[rows, :] = (h * g_ref[...]).astype(BF16)
        if n_gains:
            squares.append(jnp.sum(h * h, axis=-1, keepdims=True))
    if tail_rows:
        main_refs[1][...] = o_ref[bm - tail_rows:, :]
    if n_gains:
        (ssq_ref,) = ssq_refs
        part = jnp.broadcast_to(jnp.concatenate(squares, axis=0), ssq_ref.shape)
        first = pl.program_id(1) == 0

        @pl.when(first)
        def _():
            ssq_ref[...] = part

        @pl.when(jnp.logical_not(first))
        def _():
            ssq_ref[...] += part


def _residual(x, w, layer, res, *, bm, bn, name, part=(0, 1), gains=(), split_rows=None):
    m, k = x.shape
    n = w.shape[2]
    p, n_parts = part
    kc = k // n_parts
    assert kc * n_parts == k and kc % LANES == 0
    wb = w.dtype.itemsize
    tile = pl.BlockSpec((bm, bn), lambda i, j: (i, j))
    last = m // bm - 1

    top_rows = 0
    if isinstance(res, tuple):
        top, tail = res
        top_rows = bm - tail.shape[0]
        assert top.shape[0] + tail.shape[0] == m and top_rows > 0 and top_rows % 8 == 0
        res_specs = [tile, pl.BlockSpec((tail.shape[0], bn), lambda i, j: (0, j))]
        res_args = [top, tail]
    else:
        res_specs, res_args = [tile], [res]

    tail_rows = 0
    if split_rows is None:
        out_specs, out_shape = [tile], [jax.ShapeDtypeStruct((m, n), F32)]
    else:
        tail_rows = m - split_rows
        assert last * bm < split_rows and tail_rows <= bm
        out_specs = [tile, pl.BlockSpec((tail_rows, bn),
                                        lambda i, j: (0, jnp.where(i == last, j, 0)))]
        out_shape = [jax.ShapeDtypeStruct((split_rows, n), F32),
                     jax.ShapeDtypeStruct((tail_rows, n), F32)]
    out_specs += [tile] * len(gains)
    out_shape += [jax.ShapeDtypeStruct((m, n), BF16)] * len(gains)
    if gains:
        out_specs.append(pl.BlockSpec((bm, LANES), lambda i, j: (i, 0)))
        out_shape.append(jax.ShapeDtypeStruct((m, LANES), F32))

    vmem = 2 * bm * kc * 2 + kc * bn * (2 * wb + 2) + (8 + 2 * len(gains)) * bm * bn * 4
    return pl.pallas_call(
        functools.partial(_residual_kernel, n_res=len(res_args), n_gains=len(gains),
                          top_rows=top_rows, tail_rows=tail_rows, col_block=p),
        grid=(m // bm, n // bn),
        in_specs=[
            _LHS_SPEC,
            _layer_spec(kc, bn, layer, lambda i, j: j, row_block=p),
        ] + res_specs + [pl.BlockSpec((1, bn), lambda i, j: (0, j))] * len(gains),
        out_specs=out_specs,
        out_shape=out_shape,
        scratch_shapes=_lhs_scratch(bm, kc),
        compiler_params=_params(vmem, 2),
        name=name,
    )(x, w, *res_args, *gains)


def _ffn_up_kernel(x_hbm, ssq_ref, wg_ref, wu_ref, o_ref, x_buf, sem):
    x_ref = _lhs_tile(x_hbm, x_buf, sem)
    wg = wg_ref[...].astype(BF16)
    wu = wu_ref[...].astype(BF16)
    k = x_ref.shape[1]
    for rows in _row_groups(x_ref.shape[0]):
        x = x_ref[rows, :]
        r = _row_factor(ssq_ref.at[rows, :], k, o_ref.shape[1])
        gate = jnp.dot(x, wg, preferred_element_type=F32) * r
        up = jnp.dot(x, wu, preferred_element_type=F32) * r
        o_ref[rows, :] = (gate * jax.nn.sigmoid(gate) * up).astype(BF16)


def _ffn_up(x, ssq, w_gate, w_up, layer, *, bm, bn):
    m, k = x.shape
    n = w_gate.shape[2]
    wb = w_gate.dtype.itemsize
    vmem = 2 * bm * k * 2 + 2 * k * bn * (2 * wb + 2) + 2 * bm * bn * 2 + 5 * bm * bn * 4
    return pl.pallas_call(
        _ffn_up_kernel,
        grid=(m // bm, n // bn),
        in_specs=[
            _LHS_SPEC,
            pl.BlockSpec((bm, LANES), lambda i, j: (i, 0)),
            _layer_spec(k, bn, layer, lambda i, j: j),
            _layer_spec(k, bn, layer, lambda i, j: j),
        ],
        scratch_shapes=_lhs_scratch(bm, k),
        out_specs=pl.BlockSpec((bm, bn), lambda i, j: (i, j)),
        out_shape=jax.ShapeDtypeStruct((m, n), BF16),
        compiler_params=_params(vmem, 2),
        name="ffn_up",
    )(x, ssq, w_gate, w_up)


def _q_proj_kernel(x_hbm, ssq_ref, w_ref, seg2_ref, qg_ref, q_ref, x_buf, sem):
    x = _lhs_tile(x_hbm, x_buf, sem)[...]
    y = jnp.dot(x, w_ref[...].astype(BF16), preferred_element_type=F32)
    y = y * _row_factor(ssq_ref, x.shape[1], y.shape[1])
    q_ref[...] = (_head_rms(y, seg2_ref, qg_ref[...]) * HEAD_DIM ** -0.5).astype(BF16)


def _q_proj(x, ssq, w, layer, seg2, q_gain, *, bm, bn):
    m, k = x.shape
    n = w.shape[2]
    wb = w.dtype.itemsize
    vmem = 2 * bm * k * 2 + k * bn * (2 * wb + 2) + 10 * bm * bn * 4
    return pl.pallas_call(
        _q_proj_kernel,
        grid=(m // bm, n // bn),
        in_specs=[
            _LHS_SPEC,
            pl.BlockSpec((bm, LANES), lambda i, j: (i, 0)),
            _layer_spec(k, bn, layer, lambda i, j: j),
            pl.BlockSpec(seg2.shape, lambda i, j: (0, 0)),
            pl.BlockSpec((1, bn), lambda i, j: (0, 0)),
        ],
        out_specs=pl.BlockSpec((bm, bn), lambda i, j: (i, j)),
        out_shape=jax.ShapeDtypeStruct((m, n), BF16),
        scratch_shapes=_lhs_scratch(bm, k),
        compiler_params=_params(vmem, 2),
        name="q_proj",
    )(x, ssq, w, seg2, q_gain)


def _kv_proj_kernel(x_ref, ssq_ref, w_ref, seg2_ref, kg_ref, kf_ref, vf_ref, kz_ref, vz_ref):
    x = x_ref[...]
    y = jnp.dot(x, w_ref[...], preferred_element_type=F32)
    y = y * _row_factor(ssq_ref, x.shape[1], y.shape[1])
    kn = _head_rms(y[:, :KV_WIDTH], seg2_ref, kg_ref[...])
    v = y[:, KV_WIDTH:]
    kf_ref[...] = kn
    vf_ref[...] = v
    low = lax.broadcasted_iota(jnp.int32, (y.shape[0], LANES), 1) < HEAD_DIM
    for src, dst in ((kn, kz_ref), (v, vz_ref)):
        for p in range(N_KV_HEADS // HEADS_PER_VREG):
            both = src[:, p * LANES:(p + 1) * LANES]
            swapped = pltpu.roll(both, HEAD_DIM, axis=1)
            dst[2 * p, 0] = jnp.where(low, both, 0.0).astype(BF16)
            dst[2 * p, 1] = jnp.where(low, 0.0, swapped).astype(BF16)
            dst[2 * p + 1, 0] = jnp.where(low, swapped, 0.0).astype(BF16)
            dst[2 * p + 1, 1] = jnp.where(low, 0.0, both).astype(BF16)


def _kv_proj(x, ssq, w, seg2, k_gain, *, bm):
    m, k = x.shape
    n = w.shape[1]
    vmem = (2 * bm * k * 2 + 2 * k * n * 2 + 8 * bm * n * 4
            + 2 * 2 * N_KV_HEADS * 2 * bm * LANES * 2)
    z_spec = pl.BlockSpec((N_KV_HEADS, 2, bm, LANES), lambda i: (0, 0, i, 0))
    row_spec = pl.BlockSpec((bm, KV_WIDTH), lambda i: (i, 0))
    z_shape = jax.ShapeDtypeStruct((N_KV_HEADS, 2, m, LANES), BF16)
    return pl.pallas_call(
        _kv_proj_kernel,
        grid=(m // bm,),
        in_specs=[
            pl.BlockSpec((bm, k), lambda i: (i, 0)),
            pl.BlockSpec((bm, LANES), lambda i: (i, 0)),
            pl.BlockSpec((k, n), lambda i: (0, 0)),
            pl.BlockSpec(seg2.shape, lambda i: (0, 0)),
            pl.BlockSpec((1, KV_WIDTH), lambda i: (0, 0)),
        ],
        out_specs=[row_spec, row_spec, z_spec, z_spec],
        out_shape=[
            jax.ShapeDtypeStruct((m, KV_WIDTH), F32),
            jax.ShapeDtypeStruct((m, KV_WIDTH), F32),
            z_shape, z_shape,
        ],
        compiler_params=_params(vmem, 1),
        name="kv_proj",
    )(x, ssq, w, seg2, k_gain)


def _attend(q_ref, o_ref, sink_ref, key_parts, val_parts, *, row0, tq, bias_t=None):
    rows = PAIRS * tq
    lane_pair = lax.broadcasted_iota(jnp.int32, (1, rows), 1) >> (tq.bit_length() - 1)
    for kh in range(N_KV_HEADS):
        base = kh * GQA * HEAD_DIM
        q4 = jnp.concatenate(
            [q_ref[row0:row0 + tq, base + p * LANES:base + (p + 1) * LANES]
             for p in range(PAIRS)], axis=0)
        kk = jnp.concatenate(key_parts(kh, 0) + key_parts(kh, 1), axis=0)
        vv = jnp.concatenate(val_parts(kh, 0) + val_parts(kh, 1), axis=0)
        n_keys = kk.shape[0] // HEADS_PER_VREG
        s_t = lax.dot_general(kk, q4, (((1,), (1,)), ((), ())),
                              preferred_element_type=F32)
        if bias_t is not None:
            s_t = s_t + bias_t
        exps, denoms = [], []
        for parity in range(HEADS_PER_VREG):
            head = kh * GQA + parity
            sink = jnp.full((1, rows), sink_ref[head + HEADS_PER_VREG * (PAIRS - 1)], F32)
            for p in range(PAIRS - 1):
                sink = jnp.where(lane_pair == p, sink_ref[head + HEADS_PER_VREG * p], sink)
            half = s_t[parity * n_keys:(parity + 1) * n_keys]
            mx = jnp.maximum(jnp.max(half, axis=0, keepdims=True), sink)
            e = jnp.exp(half - mx)
            denom = jnp.sum(e, axis=0, keepdims=True) + jnp.exp(sink - mx)
            denoms.append(jnp.broadcast_to(denom, (HEAD_DIM, rows)))
            exps.append(e.astype(BF16))
        o_t = lax.dot_general(vv, jnp.concatenate(exps, axis=0),
                              (((0,), (0,)), ((), ())), preferred_element_type=F32)
        o = (o_t / jnp.concatenate(denoms, axis=0)).T
        for p in range(PAIRS):
            o_ref[row0:row0 + tq, base + p * LANES:base + (p + 1) * LANES] = (
                o[p * tq:(p + 1) * tq].astype(BF16))


def _attend_prompt_block(q_ref, kp_ref, kc_ref, vp_ref, vc_ref, sink_ref, o_ref, first):
    tq = ATT_ROWS
    n_keys = 2 * tq
    chunk_shift = CHUNK.bit_length() - 1
    chunks_per_block = tq // CHUNK
    key = lax.broadcasted_iota(jnp.int32, (2 * n_keys, PAIRS * tq), 0)
    query = lax.broadcasted_iota(jnp.int32, (2 * n_keys, PAIRS * tq), 1)
    q_chunk = (query & (tq - 1)) >> chunk_shift
    k_chunk = ((key & (n_keys - 1)) >> chunk_shift) - chunks_per_block
    oldest = jnp.where(first, 0, -chunks_per_block)
    valid = (k_chunk <= q_chunk) & (k_chunk >= jnp.maximum(q_chunk - 2, oldest))
    bias = jnp.where(valid, 0.0, NEG)
    parts = lambda prev, cur: (lambda kh, parity: [prev[kh, parity], cur[kh, parity]])
    _attend(q_ref, o_ref, sink_ref, parts(kp_ref, kc_ref), parts(vp_ref, vc_ref),
            row0=0, tq=tq, bias_t=bias)


def _attend_sample_block(q_ref, ck_ref, cv_ref, kc_ref, vc_ref, sink_ref, o_ref, *, dec_seq):
    for st in range(ATT_ROWS // dec_seq):
        new = slice(st * dec_seq, (st + 1) * dec_seq)
        parts = lambda cache, cur, st=st, new=new: (
            lambda kh, parity: [cache[st, kh, parity], cur[kh, parity, new, :]])
        _attend(q_ref, o_ref, sink_ref, parts(ck_ref, kc_ref), parts(cv_ref, vc_ref),
                row0=st * dec_seq, tq=dec_seq)


def _attn_kernel(q_ref, kp_ref, kc_ref, vp_ref, vc_ref, ck_ref, cv_ref, sink_ref, o_ref, *,
                 prompt_blocks, blocks_per_seq, dec_seq):
    step = pl.program_id(0)
    is_prompt = step < prompt_blocks
    first = (step & (blocks_per_seq - 1)) == 0
    pl.when(is_prompt)(lambda: _attend_prompt_block(
        q_ref, kp_ref, kc_ref, vp_ref, vc_ref, sink_ref, o_ref, first))
    pl.when(jnp.logical_not(is_prompt))(lambda: _attend_sample_block(
        q_ref, ck_ref, cv_ref, kc_ref, vc_ref, sink_ref, o_ref, dec_seq=dec_seq))


def _attn(q, kz, vz, cache_kz, cache_vz, sinks, *, m_prompt, seq, dec_seq):
    m = q.shape[0]
    tq = ATT_ROWS
    blocks_per_seq = seq // tq
    prompt_blocks = m_prompt // tq
    streams_per_block = tq // dec_seq
    assert blocks_per_seq & (blocks_per_seq - 1) == 0 and (m - m_prompt) % tq == 0
    rows_here = lambda s: (s, 0)
    cur = lambda s: (0, 0, s, 0)
    prev = lambda s: (0, 0, jnp.where((s & (blocks_per_seq - 1)) == 0, s, s - 1), 0)
    cache = lambda s: (jnp.maximum(s - prompt_blocks, 0), 0, 0, 0, 0)
    z_block = (N_KV_HEADS, 2, tq, LANES)
    cache_block = (streams_per_block, N_KV_HEADS, 2, WINDOW, LANES)
    vmem = 4 * tq * D_MODEL * 2 + 8 * N_KV_HEADS * 2 * tq * LANES * 2 \
        + 4 * streams_per_block * N_KV_HEADS * 2 * WINDOW * LANES * 2 \
        + 10 * PAIRS * tq * 4 * tq * 4
    return pl.pallas_call(
        functools.partial(_attn_kernel, prompt_blocks=prompt_blocks,
                          blocks_per_seq=blocks_per_seq, dec_seq=dec_seq),
        grid=(m // tq,),
        in_specs=[
            pl.BlockSpec((tq, D_MODEL), rows_here),
            pl.BlockSpec(z_block, prev),
            pl.BlockSpec(z_block, cur),
            pl.BlockSpec(z_block, prev),
            pl.BlockSpec(z_block, cur),
            pl.BlockSpec(cache_block, cache),
            pl.BlockSpec(cache_block, cache),
            pl.BlockSpec(memory_space=pltpu.SMEM),
        ],
        out_specs=pl.BlockSpec((tq, D_MODEL), rows_here),
        out_shape=jax.ShapeDtypeStruct((m, D_MODEL), BF16),
        compiler_params=_params(vmem, 1),
        name="attn",
    )(q, kz, kz, vz, vz, cache_kz, cache_vz, sinks)


def _lane_pair_copies(cache):
    c = cache.transpose(0, 2, 1, 3).astype(BF16)
    z = jnp.zeros_like(c)
    return jnp.stack([jnp.concatenate([c, z], axis=-1), jnp.concatenate([z, c], axis=-1)], axis=2)


def kernel(x_prompt, x_sample, cache_k, cache_v, norm_a, w_sgu_in, sgu_ln_g, sgu_ln_b, w_sgu_s,
           b_sgu_s, w_sgu_out, norm_kv, w_kv, k_norm, norm_b, w_q, q_norm, sinks, w_o,
           norm_ffn, w_ffn_gate, w_ffn_up, w_ffn_down):
    batch, seq, d = x_prompt.shape
    streams, dec_seq, _ = x_sample.shape
    m_prompt = batch * seq
    m_sample = streams * dec_seq
    m = m_prompt + m_sample
    assert d == D_MODEL and seq % ATT_ROWS == 0
    assert m % WIDE_TILE == 0 and m % KV_TILE == 0
    assert norm_a.shape[0] == 1 and norm_b.shape[0] == 1 and norm_ffn.shape[0] == 2

    row = lambda g: g.reshape(1, -1).astype(F32)
    tiles = dict(bm=WIDE_TILE, bn=MXU_WIDTH)

    def ffn(h, scaled, ssq, layer, **last_part):
        hidden = _ffn_up(scaled, ssq, w_ffn_gate, w_ffn_up, layer, **tiles)
        for p in range(FFN_DOWN_PARTS):
            final = p == FFN_DOWN_PARTS - 1
            outs = _residual(hidden, w_ffn_down, layer, h, part=(p, FFN_DOWN_PARTS),
                             name="ffn_down_%d_%d" % (layer, p), **tiles,
                             **(last_part if final else {}))
            h = outs[0]
        return outs

    x_rows = (x_prompt.reshape(m_prompt, d), x_sample.reshape(m_sample, d))
    xn = _rms_stack(*x_rows, row(norm_a[0]))
    uv = _sgu_in(xn, w_sgu_in, 0, bm=WIDE_TILE, bn=2 * MXU_WIDTH)

    pos_chunk = jnp.arange(SGU_CHUNK) // CHUNK
    ws = jnp.where((pos_chunk[:, None] >= pos_chunk[None, :])[None], w_sgu_s[0], 0.0).astype(BF16)
    bs_t = b_sgu_s[0].T.astype(F32)
    ln_g, ln_b = row(sgu_ln_g[0]), row(sgu_ln_b[0])
    a, v_rows = _sgu_mix(uv, ln_g, ln_b, ws, bs_t, m_prompt=m_prompt, sample_chunk=dec_seq)
    h, scaled, ssq = _residual(a, w_sgu_out, 0, x_rows, name="sgu_out", **tiles,
                               gains=[row(norm_ffn[0])])
    h, scaled_kv, scaled_q, ssq = ffn(h, scaled, ssq, 0, gains=[row(norm_kv), row(norm_b[0])])

    lane_head = jnp.arange(MXU_WIDTH) // HEAD_DIM
    seg = (lane_head[:, None] == lane_head[None, :]).astype(BF16)
    seg2 = jnp.concatenate([seg, seg], axis=0)
    k_rows, v_rows_kv, kz, vz = _kv_proj(
        scaled_kv, ssq, w_kv.astype(BF16), seg2, row(jnp.tile(k_norm, N_KV_HEADS)), bm=KV_TILE)
    q = _q_proj(scaled_q, ssq, w_q, 0, seg2, row(jnp.tile(q_norm[0], MXU_WIDTH // HEAD_DIM)),
                **tiles)
    sink = sinks[0].astype(F32)
    o = _attn(q, kz, vz, _lane_pair_copies(cache_k), _lane_pair_copies(cache_v), sink,
              m_prompt=m_prompt, seq=seq, dec_seq=dec_seq)
    h, scaled, ssq = _residual(o, w_o, 0, h, name="attn_out", **tiles,
                               gains=[row(norm_ffn[1])])
    y_prompt, y_sample = ffn(h, scaled, ssq, 1, split_rows=m_prompt)

    heads = lambda t: t.reshape(t.shape[:-1] + (N_KV_HEADS, HEAD_DIM))
    tail = lambda t: heads(t[:m_prompt].reshape(batch, seq, KV_WIDTH)[:, seq - WINDOW:])
    fresh = lambda t: heads(t[m_prompt:].reshape(streams, dec_seq, KV_WIDTH))
    k_p, v_p = tail(k_rows), tail(v_rows_kv)
    k_s, v_s = fresh(k_rows), fresh(v_rows_kv)
    return (y_prompt.reshape(batch, seq, d), y_sample.reshape(streams, dec_seq, d),
            k_p, v_p, k_s, v_s, v_rows.reshape(1, streams, dec_seq, d))
```

```python
import functools

import jax
import jax.numpy as jnp
from jax import lax
from jax.experimental import pallas as pl
from jax.experimental.pallas import tpu as pltpu

F32 = jnp.float32
BF16 = jnp.bfloat16

D_MODEL = 4096
SGU_CHUNK = 128
SGU_GROUPS = 8
SGU_GROUP_WIDTH = D_MODEL // SGU_GROUPS
CHUNK = 64
HEAD_DIM = 64
N_HEADS = 64
N_KV_HEADS = 8
GQA = N_HEADS // N_KV_HEADS
KV_WIDTH = N_KV_HEADS * HEAD_DIM
WINDOW = 128
NEG = -1e30
RMS_EPS = 1e-6
LN_EPS = 1e-5

V7X_VMEM_BYTES = 64 * 1024 * 1024
VMEM_LIMIT_CAP = V7X_VMEM_BYTES - 6 * 1024 * 1024
LANES = 128
MXU_WIDTH = 256
HEADS_PER_VREG = LANES // HEAD_DIM
PAIRS = GQA // HEADS_PER_VREG

WIDE_TILE = 1408
KV_TILE = WIDE_TILE // 2
NORM_ROWS = 256
ATT_ROWS = 128
RMS_STAT_ROWS = 64
RMS_SCALE_ROWS = 32
FFN_DOWN_PARTS = 2
ROW_GROUPS = 4


def _params(vmem_bytes, n_axes):
    limit = min(int(vmem_bytes) + 8 * 1024 * 1024, VMEM_LIMIT_CAP)
    return pltpu.CompilerParams(
        dimension_semantics=("arbitrary",) * n_axes, vmem_limit_bytes=limit)


def _row_groups(rows):
    size = rows // ROW_GROUPS
    assert size * ROW_GROUPS == rows and size % 16 == 0
    return [slice(g * size, (g + 1) * size) for g in range(ROW_GROUPS)]


_LHS_SPEC = pl.BlockSpec(memory_space=pl.ANY)


def _lhs_scratch(bm, kc):
    return [pltpu.VMEM((2, bm, kc), BF16), pltpu.SemaphoreType.DMA((2,))]


def _lhs_tile(x_hbm, x_buf, sem, col_block=0):
    _, bm, kc = x_buf.shape
    i, n_tiles = pl.program_id(0), pl.num_programs(0)
    slot = lax.rem(i, 2)

    def copy(tile, slot):
        src = x_hbm.at[pl.ds(tile * bm, bm), pl.ds(col_block * kc, kc)]
        return pltpu.make_async_copy(src, x_buf.at[slot], sem.at[slot])

    @pl.when(pl.program_id(1) == 0)
    def _():
        @pl.when(i == 0)
        def _():
            copy(0, 0).start()

        copy(i, slot).wait()

        @pl.when(i + 1 < n_tiles)
        def _():
            copy(i + 1, 1 - slot).start()

    return x_buf.at[slot]


def _layer_spec(k, bn, layer, col_of, row_block=0):
    return pl.BlockSpec((None, k, bn), lambda *idx: (layer, row_block, col_of(*idx)))


def _rms_rows(x_ref, g_ref, xn_ref, r_ref):
    rows, k = x_ref.shape

    def stats(i, carry):
        sl = pl.ds(pl.multiple_of(i * RMS_STAT_ROWS, RMS_STAT_ROWS), RMS_STAT_ROWS)
        x = x_ref[sl, :]
        ms = jnp.mean(x * x, axis=-1, keepdims=True)
        r_ref[sl, :] = jnp.broadcast_to(lax.rsqrt(ms + RMS_EPS), (RMS_STAT_ROWS, LANES))
        return carry

    lax.fori_loop(0, rows // RMS_STAT_ROWS, stats, 0)
    g = g_ref[...]

    def scale(i, carry):
        sl = pl.ds(pl.multiple_of(i * RMS_SCALE_ROWS, RMS_SCALE_ROWS), RMS_SCALE_ROWS)
        r = jnp.tile(r_ref[sl, :], (1, k // LANES))
        xn_ref[sl, :] = (x_ref[sl, :] * r * g).astype(BF16)
        return carry

    lax.fori_loop(0, rows // RMS_SCALE_ROWS, scale, 0)


def _rms_stack_kernel(xp_ref, xs_ref, g_ref, xn_ref, r_ref, *, prompt_blocks):
    from_ref = lambda x_ref: _rms_rows(x_ref, g_ref, xn_ref, r_ref)
    pl.when(pl.program_id(0) < prompt_blocks)(lambda: from_ref(xp_ref))
    pl.when(pl.program_id(0) >= prompt_blocks)(lambda: from_ref(xs_ref))


def _rms_stack(x_prompt, x_sample, gain):
    mp, k = x_prompt.shape
    ms = x_sample.shape[0]
    assert mp % NORM_ROWS == 0 and ms == NORM_ROWS
    prompt_blocks = mp // NORM_ROWS
    m = mp + ms
    vmem = 4 * NORM_ROWS * k * 4 + 2 * NORM_ROWS * k * 2 + 4 * RMS_STAT_ROWS * k * 4
    return pl.pallas_call(
        functools.partial(_rms_stack_kernel, prompt_blocks=prompt_blocks),
        grid=(m // NORM_ROWS,),
        in_specs=[
            pl.BlockSpec((NORM_ROWS, k), lambda i: (jnp.minimum(i, prompt_blocks - 1), 0)),
            pl.BlockSpec((NORM_ROWS, k), lambda i: (0, 0)),
            pl.BlockSpec((1, k), lambda i: (0, 0)),
        ],
        out_specs=pl.BlockSpec((NORM_ROWS, k), lambda i: (i, 0)),
        out_shape=jax.ShapeDtypeStruct((m, k), BF16),
        scratch_shapes=[pltpu.VMEM((NORM_ROWS, LANES), F32)],
        compiler_params=_params(vmem, 1),
        name="rms_stack",
    )(x_prompt, x_sample, gain)


def _head_rms(y, seg2_ref, gain_row):
    y2 = y * y
    hi = y2.astype(BF16)
    lo = (y2 - hi.astype(F32)).astype(BF16)
    seg2 = seg2_ref[...]
    w = seg2.shape[1]
    ssq = jnp.concatenate(
        [jnp.dot(jnp.concatenate([hi[:, c:c + w], lo[:, c:c + w]], axis=1), seg2,
                 preferred_element_type=F32)
         for c in range(0, y.shape[1], w)], axis=-1)
    return y * lax.rsqrt(ssq * (1.0 / HEAD_DIM) + RMS_EPS) * gain_row


def _row_factor(ssq_ref, k, n):
    r = lax.rsqrt(ssq_ref[...] * (1.0 / k) + RMS_EPS)
    return jnp.tile(r, (1, n // LANES))


def _sgu_in_kernel(x_hbm, w_ref, o_ref, x_buf, sem):
    x_ref = _lhs_tile(x_hbm, x_buf, sem)
    w = w_ref[...].astype(BF16)
    for rows in _row_groups(x_ref.shape[0]):
        y = jnp.dot(x_ref[rows, :], w, preferred_element_type=F32)
        o_ref[rows, :] = 0.5 * y * (1.0 + lax.erf(y * (0.5 ** 0.5)))


def _sgu_in(xn, w, layer, *, bm, bn):
    m, k = xn.shape
    n = w.shape[2]
    wb = w.dtype.itemsize
    vmem = 2 * bm * k * 2 + k * bn * (2 * wb + 2) + 5 * bm * bn * 4
    return pl.pallas_call(
        _sgu_in_kernel,
        grid=(m // bm, n // bn),
        in_specs=[_LHS_SPEC, _layer_spec(k, bn, layer, lambda i, j: j)],
        out_specs=pl.BlockSpec((bm, bn), lambda i, j: (i, j)),
        out_shape=jax.ShapeDtypeStruct((m, n), F32),
        scratch_shapes=_lhs_scratch(bm, k),
        compiler_params=_params(vmem, 2),
        name="sgu_in",
    )(xn, w)


def _sgu_mix_rows(u_ref, v_ref, lng_ref, lnb_ref, ws_ref, bs_ref, a_ref, vn_ref, *, chunk):
    ln_g = lng_ref[...]
    ln_b = lnb_ref[...]
    for c in range(u_ref.shape[0] // chunk):
        rows = slice(c * chunk, (c + 1) * chunk)
        v = v_ref[rows, :]
        mu = jnp.mean(v, axis=-1, keepdims=True)
        vc = v - mu
        var = jnp.mean(vc * vc, axis=-1, keepdims=True)
        vn = vc * lax.rsqrt(var + LN_EPS) * ln_g + ln_b
        if vn_ref is not None:
            vn_ref[rows, :] = vn
        vnb = vn.astype(BF16)
        for g in range(SGU_GROUPS):
            cols = slice(g * SGU_GROUP_WIDTH, (g + 1) * SGU_GROUP_WIDTH)
            mixed = jnp.dot(ws_ref[g], vnb[:, cols], preferred_element_type=F32)
            mixed = mixed + bs_ref[:, g:g + 1]
            a_ref[rows, cols] = (u_ref[rows, cols] * mixed).astype(BF16)


def _sgu_mix_kernel(u_ref, v_ref, lng_ref, lnb_ref, ws_ref, bs_ref, wss_ref, bss_ref,
                    a_ref, vn_ref, *, prompt_blocks, sample_chunk):
    common = (u_ref, v_ref, lng_ref, lnb_ref)
    is_prompt = pl.program_id(0) < prompt_blocks
    pl.when(is_prompt)(
        lambda: _sgu_mix_rows(*common, ws_ref, bs_ref, a_ref, None, chunk=SGU_CHUNK))
    pl.when(jnp.logical_not(is_prompt))(
        lambda: _sgu_mix_rows(*common, wss_ref, bss_ref, a_ref, vn_ref, chunk=sample_chunk))


def _sgu_mix(uv, ln_g, ln_b, ws, bs_t, *, m_prompt, sample_chunk):
    m = uv.shape[0]
    d = D_MODEL
    rows = m - m_prompt
    assert m_prompt % rows == 0 and rows % SGU_CHUNK == 0 and rows % sample_chunk == 0
    prompt_blocks = m_prompt // rows
    const = lambda shape: pl.BlockSpec(shape, lambda i: (0,) * len(shape))
    wss, bss = ws[:, :sample_chunk, :sample_chunk], bs_t[:sample_chunk]
    vmem = 2 * 2 * rows * d * 4 + 2 * rows * d * 2 + 2 * rows * d * 4 + 6 * SGU_CHUNK * d * 4
    return pl.pallas_call(
        functools.partial(_sgu_mix_kernel, prompt_blocks=prompt_blocks,
                          sample_chunk=sample_chunk),
        grid=(prompt_blocks + 1,),
        in_specs=[
            pl.BlockSpec((rows, d), lambda i: (i, 0)),
            pl.BlockSpec((rows, d), lambda i: (i, 1)),
            const((1, d)), const((1, d)),
            const(ws.shape), const(bs_t.shape), const(wss.shape), const(bss.shape),
        ],
        out_specs=[pl.BlockSpec((rows, d), lambda i: (i, 0)), const((rows, d))],
        out_shape=[jax.ShapeDtypeStruct((m, d), BF16), jax.ShapeDtypeStruct((rows, d), F32)],
        compiler_params=_params(vmem, 1),
        name="sgu_mix",
    )(uv, uv, ln_g, ln_b, ws, bs_t, wss, bss)


def _residual_kernel(x_hbm, w_ref, *refs, n_res, n_gains, top_rows, tail_rows, col_block):
    x_ref = _lhs_tile(x_hbm, *refs[-2:], col_block=col_block)
    refs = refs[:-2]
    res_refs, refs = refs[:n_res], refs[n_res:]
    gain_refs, refs = refs[:n_gains], refs[n_gains:]
    n_main = 2 if tail_rows else 1
    main_refs, refs = refs[:n_main], refs[n_main:]
    scaled_refs, ssq_refs = refs[:n_gains], refs[n_gains:]
    o_ref = main_refs[0]
    w = w_ref[...].astype(BF16)
    bm = x_ref.shape[0]
    squares = []
    plain = n_res == 1 and n_gains == 0
    for rows in (_row_groups(bm) if plain else [slice(0, bm)]):
        y = jnp.dot(x_ref[rows, :], w, preferred_element_type=F32)
        if n_res == 1:
            h = res_refs[0][rows, :] + y
            o_ref[rows, :] = h
        else:
            top_ref, tail_ref = res_refs
            is_last = pl.program_id(0) == pl.num_programs(0) - 1
            o_ref[:top_rows, :] = top_ref[:top_rows, :] + y[:top_rows]

            @pl.when(jnp.logical_not(is_last))
            def _():
                o_ref[top_rows:, :] = top_ref[top_rows:, :] + y[top_rows:]

            @pl.when(is_last)
            def _():
                o_ref[top_rows:, :] = tail_ref[...] + y[top_rows:]

            h = o_ref[...]
        for g_ref, s_ref in zip(gain_refs, scaled_refs):
            s_ref[rows, :] = (h * g_ref[...]).astype(BF16)
        if n_gains:
            squares.append(jnp.sum(h * h, axis=-1, keepdims=True))
    if tail_rows:
        main_refs[1][...] = o_ref[bm - tail_rows:, :]
    if n_gains:
        (ssq_ref,) = ssq_refs
        part = jnp.broadcast_to(jnp.concatenate(squares, axis=0), ssq_ref.shape)
        first = pl.program_id(1) == 0

        @pl.when(first)
        def _():
            ssq_ref[...] = part

        @pl.when(jnp.logical_not(first))
        def _():
            ssq_ref[...] += part


def _residual(x, w, layer, res, *, bm, bn, name, part=(0, 1), gains=(), split_rows=None):
    m, k = x.shape
    n = w.shape[2]
    p, n_parts = part
    kc = k // n_parts
    assert kc * n_parts == k and kc % LANES == 0
    wb = w.dtype.itemsize
    tile = pl.BlockSpec((bm, bn), lambda i, j: (i, j))
    last = m // bm - 1

    top_rows = 0
    if isinstance(res, tuple):
        top, tail = res
        top_rows = bm - tail.shape[0]
        assert top.shape[0] + tail.shape[0] == m and top_rows > 0 and top_rows % 8 == 0
        res_specs = [tile, pl.BlockSpec((tail.shape[0], bn), lambda i, j: (0, j))]
        res_args = [top, tail]
    else:
        res_specs, res_args = [tile], [res]

    tail_rows = 0
    if split_rows is None:
        out_specs, out_shape = [tile], [jax.ShapeDtypeStruct((m, n), F32)]
    else:
        tail_rows = m - split_rows
        assert last * bm < split_rows and tail_rows <= bm
        out_specs = [tile, pl.BlockSpec((tail_rows, bn),
                                        lambda i, j: (0, jnp.where(i == last, j, 0)))]
        out_shape = [jax.ShapeDtypeStruct((split_rows, n), F32),
                     jax.ShapeDtypeStruct((tail_rows, n), F32)]
    out_specs += [tile] * len(gains)
    out_shape += [jax.ShapeDtypeStruct((m, n), BF16)] * len(gains)
    if gains:
        out_specs.append(pl.BlockSpec((bm, LANES), lambda i, j: (i, 0)))
        out_shape.append(jax.ShapeDtypeStruct((m, LANES), F32))

    vmem = 2 * bm * kc * 2 + kc * bn * (2 * wb + 2) + (8 + 2 * len(gains)) * bm * bn * 4
    return pl.pallas_call(
        functools.partial(_residual_kernel, n_res=len(res_args), n_gains=len(gains),
                          top_rows=top_rows, tail_rows=tail_rows, col_block=p),
        grid=(m // bm, n // bn),
        in_specs=[
            _LHS_SPEC,
            _layer_spec(kc, bn, layer, lambda i, j: j, row_block=p),
        ] + res_specs + [pl.BlockSpec((1, bn), lambda i, j: (0, j))] * len(gains),
        out_specs=out_specs,
        out_shape=out_shape,
        scratch_shapes=_lhs_scratch(bm, kc),
        compiler_params=_params(vmem, 2),
        name=name,
    )(x, w, *res_args, *gains)


def _ffn_up_kernel(x_hbm, ssq_ref, wg_ref, wu_ref, o_ref, x_buf, sem):
    x_ref = _lhs_tile(x_hbm, x_buf, sem)
    wg = wg_ref[...].astype(BF16)
    wu = wu_ref[...].astype(BF16)
    k = x_ref.shape[1]
    for rows in _row_groups(x_ref.shape[0]):
        x = x_ref[rows, :]
        r = _row_factor(ssq_ref.at[rows, :], k, o_ref.shape[1])
        gate = jnp.dot(x, wg, preferred_element_type=F32) * r
        up = jnp.dot(x, wu, preferred_element_type=F32) * r
        o_ref[rows, :] = (gate * jax.nn.sigmoid(gate) * up).astype(BF16)


def _ffn_up(x, ssq, w_gate, w_up, layer, *, bm, bn):
    m, k = x.shape
    n = w_gate.shape[2]
    wb = w_gate.dtype.itemsize
    vmem = 2 * bm * k * 2 + 2 * k * bn * (2 * wb + 2) + 2 * bm * bn * 2 + 5 * bm * bn * 4
    return pl.pallas_call(
        _ffn_up_kernel,
        grid=(m // bm, n // bn),
        in_specs=[
            _LHS_SPEC,
            pl.BlockSpec((bm, LANES), lambda i, j: (i, 0)),
            _layer_spec(k, bn, layer, lambda i, j: j),
            _layer_spec(k, bn, layer, lambda i, j: j),
        ],
        scratch_shapes=_lhs_scratch(bm, k),
        out_specs=pl.BlockSpec((bm, bn), lambda i, j: (i, j)),
        out_shape=jax.ShapeDtypeStruct((m, n), BF16),
        compiler_params=_params(vmem, 2),
        name="ffn_up",
    )(x, ssq, w_gate, w_up)


LOG2_E = 1.4426950408889634
SCORE_SCALE = HEAD_DIM ** -0.5 * LOG2_E


def _q_proj_kernel(x_hbm, ssq_ref, w_ref, seg2_ref, qg_ref, q_ref, x_buf, sem):
    x = _lhs_tile(x_hbm, x_buf, sem)[...]
    y = jnp.dot(x, w_ref[...].astype(BF16), preferred_element_type=F32)
    y = y * _row_factor(ssq_ref, x.shape[1], y.shape[1])
    q_ref[...] = (_head_rms(y, seg2_ref, qg_ref[...]) * SCORE_SCALE).astype(BF16)


def _q_proj(x, ssq, w, layer, seg2, q_gain, *, bm, bn):
    m, k = x.shape
    n = w.shape[2]
    wb = w.dtype.itemsize
    vmem = 2 * bm * k * 2 + k * bn * (2 * wb + 2) + 10 * bm * bn * 4
    return pl.pallas_call(
        _q_proj_kernel,
        grid=(m // bm, n // bn),
        in_specs=[
            _LHS_SPEC,
            pl.BlockSpec((bm, LANES), lambda i, j: (i, 0)),
            _layer_spec(k, bn, layer, lambda i, j: j),
            pl.BlockSpec(seg2.shape, lambda i, j: (0, 0)),
            pl.BlockSpec((1, bn), lambda i, j: (0, 0)),
        ],
        out_specs=pl.BlockSpec((bm, bn), lambda i, j: (i, j)),
        out_shape=jax.ShapeDtypeStruct((m, n), BF16),
        scratch_shapes=_lhs_scratch(bm, k),
        compiler_params=_params(vmem, 2),
        name="q_proj",
    )(x, ssq, w, seg2, q_gain)


def _kv_proj_kernel(x_ref, ssq_ref, w_ref, seg2_ref, kg_ref, kf_ref, vf_ref, kz_ref, vz_ref):
    x = x_ref[...]
    y = jnp.dot(x, w_ref[...], preferred_element_type=F32)
    y = y * _row_factor(ssq_ref, x.shape[1], y.shape[1])
    kn = _head_rms(y[:, :KV_WIDTH], seg2_ref, kg_ref[...])
    v = y[:, KV_WIDTH:]
    kf_ref[...] = kn
    vf_ref[...] = v
    low = lax.broadcasted_iota(jnp.int32, (y.shape[0], LANES), 1) < HEAD_DIM
    for src, dst in ((kn, kz_ref), (v, vz_ref)):
        for p in range(N_KV_HEADS // HEADS_PER_VREG):
            both = src[:, p * LANES:(p + 1) * LANES]
            swapped = pltpu.roll(both, HEAD_DIM, axis=1)
            dst[2 * p, 0] = jnp.where(low, both, 0.0).astype(BF16)
            dst[2 * p, 1] = jnp.where(low, 0.0, swapped).astype(BF16)
            dst[2 * p + 1, 0] = jnp.where(low, swapped, 0.0).astype(BF16)
            dst[2 * p + 1, 1] = jnp.where(low, 0.0, both).astype(BF16)


def _kv_proj(x, ssq, w, seg2, k_gain, *, bm):
    m, k = x.shape
    n = w.shape[1]
    vmem = (2 * bm * k * 2 + 2 * k * n * 2 + 8 * bm * n * 4
            + 2 * 2 * N_KV_HEADS * 2 * bm * LANES * 2)
    z_spec = pl.BlockSpec((N_KV_HEADS, 2, bm, LANES), lambda i: (0, 0, i, 0))
    row_spec = pl.BlockSpec((bm, KV_WIDTH), lambda i: (i, 0))
    z_shape = jax.ShapeDtypeStruct((N_KV_HEADS, 2, m, LANES), BF16)
    return pl.pallas_call(
        _kv_proj_kernel,
        grid=(m // bm,),
        in_specs=[
            pl.BlockSpec((bm, k), lambda i: (i, 0)),
            pl.BlockSpec((bm, LANES), lambda i: (i, 0)),
            pl.BlockSpec((k, n), lambda i: (0, 0)),
            pl.BlockSpec(seg2.shape, lambda i: (0, 0)),
            pl.BlockSpec((1, KV_WIDTH), lambda i: (0, 0)),
        ],
        out_specs=[row_spec, row_spec, z_spec, z_spec],
        out_shape=[
            jax.ShapeDtypeStruct((m, KV_WIDTH), F32),
            jax.ShapeDtypeStruct((m, KV_WIDTH), F32),
            z_shape, z_shape,
        ],
        compiler_params=_params(vmem, 1),
        name="kv_proj",
    )(x, ssq, w, seg2, k_gain)


def _attend(q_ref, o_ref, sink_ref, key_parts, val_parts, *, row0, tq, mask_lanes=None):
    rows = PAIRS * tq
    lane_pair = lax.broadcasted_iota(jnp.int32, (1, rows), 1) >> (tq.bit_length() - 1)
    for kh in range(N_KV_HEADS):
        base = kh * GQA * HEAD_DIM
        q4 = jnp.concatenate(
            [q_ref[row0:row0 + tq, base + p * LANES:base + (p + 1) * LANES]
             for p in range(PAIRS)], axis=0)
        kk = jnp.concatenate(key_parts(kh, 0) + key_parts(kh, 1), axis=0)
        vv = jnp.concatenate(val_parts(kh, 0) + val_parts(kh, 1), axis=0)
        n_keys = kk.shape[0] // HEADS_PER_VREG
        if mask_lanes is not None:
            kk = jnp.concatenate([kk, mask_lanes[0]], axis=1)
            q4 = jnp.concatenate([q4, mask_lanes[1]], axis=1)
        s_t = lax.dot_general(kk, q4, (((1,), (1,)), ((), ())),
                              preferred_element_type=F32)
        exps, denoms = [], []
        for parity in range(HEADS_PER_VREG):
            head = kh * GQA + parity
            sink = jnp.full((1, rows), sink_ref[head + HEADS_PER_VREG * (PAIRS - 1)], F32)
            for p in range(PAIRS - 1):
                sink = jnp.where(lane_pair == p, sink_ref[head + HEADS_PER_VREG * p], sink)
            sink = sink * LOG2_E
            half = s_t[parity * n_keys:(parity + 1) * n_keys]
            mx = jnp.maximum(jnp.max(half, axis=0, keepdims=True), sink)
            e = jnp.exp2(half - mx)
            denom = jnp.sum(e, axis=0, keepdims=True) + jnp.exp2(sink - mx)
            denoms.append(jnp.broadcast_to(denom, (HEAD_DIM, rows)))
            exps.append(e.astype(BF16))
        o_t = lax.dot_general(vv, jnp.concatenate(exps, axis=0),
                              (((0,), (0,)), ((), ())), preferred_element_type=F32)
        o = (o_t / jnp.concatenate(denoms, axis=0)).T
        for p in range(PAIRS):
            o_ref[row0:row0 + tq, base + p * LANES:base + (p + 1) * LANES] = (
                o[p * tq:(p + 1) * tq].astype(BF16))


def _attend_prompt_block(q_ref, kp_ref, kc_ref, vp_ref, vc_ref, sink_ref, o_ref, first):
    tq = ATT_ROWS
    n_keys = 2 * tq
    chunk_shift = CHUNK.bit_length() - 1
    chunks_per_block = tq // CHUNK
    lane = lax.broadcasted_iota(jnp.int32, (2 * n_keys, LANES), 1)
    key = lax.broadcasted_iota(jnp.int32, (2 * n_keys, LANES), 0)
    k_chunk = ((key & (n_keys - 1)) >> chunk_shift) - chunks_per_block
    oldest = jnp.where(first, 0, -chunks_per_block)
    visible = (k_chunk <= lane) & (k_chunk >= jnp.maximum(lane - 2, oldest))
    key_side = jnp.where((lane < chunks_per_block) & jnp.logical_not(visible), NEG, 0.0)
    query = lax.broadcasted_iota(jnp.int32, (PAIRS * tq, LANES), 0)
    q_lane = lax.broadcasted_iota(jnp.int32, (PAIRS * tq, LANES), 1)
    query_side = jnp.where(((query & (tq - 1)) >> chunk_shift) == q_lane, 1.0, 0.0)
    parts = lambda prev, cur: (lambda kh, parity: [prev[kh, parity], cur[kh, parity]])
    _attend(q_ref, o_ref, sink_ref, parts(kp_ref, kc_ref), parts(vp_ref, vc_ref),
            row0=0, tq=tq, mask_lanes=(key_side.astype(BF16), query_side.astype(BF16)))


def _attend_sample_block(q_ref, ck_ref, cv_ref, kc_ref, vc_ref, sink_ref, o_ref, *, dec_seq):
    for st in range(ATT_ROWS // dec_seq):
        new = slice(st * dec_seq, (st + 1) * dec_seq)
        parts = lambda cache, cur, st=st, new=new: (
            lambda kh, parity: [cache[st, kh, parity], cur[kh, parity, new, :]])
        _attend(q_ref, o_ref, sink_ref, parts(ck_ref, kc_ref), parts(cv_ref, vc_ref),
                row0=st * dec_seq, tq=dec_seq)


def _attn_kernel(q_ref, kp_ref, kc_ref, vp_ref, vc_ref, ck_ref, cv_ref, sink_ref, o_ref, *,
                 prompt_blocks, blocks_per_seq, dec_seq):
    step = pl.program_id(0)
    is_prompt = step < prompt_blocks
    first = (step & (blocks_per_seq - 1)) == 0
    pl.when(is_prompt)(lambda: _attend_prompt_block(
        q_ref, kp_ref, kc_ref, vp_ref, vc_ref, sink_ref, o_ref, first))
    pl.when(jnp.logical_not(is_prompt))(lambda: _attend_sample_block(
        q_ref, ck_ref, cv_ref, kc_ref, vc_ref, sink_ref, o_ref, dec_seq=dec_seq))


def _attn(q, kz, vz, cache_kz, cache_vz, sinks, *, m_prompt, seq, dec_seq):
    m = q.shape[0]
    tq = ATT_ROWS
    blocks_per_seq = seq // tq
    prompt_blocks = m_prompt // tq
    streams_per_block = tq // dec_seq
    assert blocks_per_seq & (blocks_per_seq - 1) == 0 and (m - m_prompt) % tq == 0
    rows_here = lambda s: (s, 0)
    cur = lambda s: (0, 0, s, 0)
    prev = lambda s: (0, 0, jnp.where((s & (blocks_per_seq - 1)) == 0, s, s - 1), 0)
    cache = lambda s: (jnp.maximum(s - prompt_blocks, 0), 0, 0, 0, 0)
    z_block = (N_KV_HEADS, 2, tq, LANES)
    cache_block = (streams_per_block, N_KV_HEADS, 2, WINDOW, LANES)
    vmem = 4 * tq * D_MODEL * 2 + 8 * N_KV_HEADS * 2 * tq * LANES * 2 \
        + 4 * streams_per_block * N_KV_HEADS * 2 * WINDOW * LANES * 2 \
        + 10 * PAIRS * tq * 4 * tq * 4
    return pl.pallas_call(
        functools.partial(_attn_kernel, prompt_blocks=prompt_blocks,
                          blocks_per_seq=blocks_per_seq, dec_seq=dec_seq),
        grid=(m // tq,),
        in_specs=[
            pl.BlockSpec((tq, D_MODEL), rows_here),
            pl.BlockSpec(z_block, prev),
            pl.BlockSpec(z_block, cur),
            pl.BlockSpec(z_block, prev),
            pl.BlockSpec(z_block, cur),
            pl.BlockSpec(cache_block, cache),
            pl.BlockSpec(cache_block, cache),
            pl.BlockSpec(memory_space=pltpu.SMEM),
        ],
        out_specs=pl.BlockSpec((tq, D_MODEL), rows_here),
        out_shape=jax.ShapeDtypeStruct((m, D_MODEL), BF16),
        compiler_params=_params(vmem, 1),
        name="attn",
    )(q, kz, kz, vz, vz, cache_kz, cache_vz, sinks)


def _lane_pair_copies(cache):
    c = cache.transpose(0, 2, 1, 3).astype(BF16)
    z = jnp.zeros_like(c)
    return jnp.stack([jnp.concatenate([c, z], axis=-1), jnp.concatenate([z, c], axis=-1)], axis=2)


def kernel(x_prompt, x_sample, cache_k, cache_v, norm_a, w_sgu_in, sgu_ln_g, sgu_ln_b, w_sgu_s,
           b_sgu_s, w_sgu_out, norm_kv, w_kv, k_norm, norm_b, w_q, q_norm, sinks, w_o,
           norm_ffn, w_ffn_gate, w_ffn_up, w_ffn_down):
    batch, seq, d = x_prompt.shape
    streams, dec_seq, _ = x_sample.shape
    m_prompt = batch * seq
    m_sample = streams * dec_seq
    m = m_prompt + m_sample
    assert d == D_MODEL and seq % ATT_ROWS == 0
    assert m % WIDE_TILE == 0 and m % KV_TILE == 0
    assert norm_a.shape[0] == 1 and norm_b.shape[0] == 1 and norm_ffn.shape[0] == 2

    row = lambda g: g.reshape(1, -1).astype(F32)
    tiles = dict(bm=WIDE_TILE, bn=MXU_WIDTH)

    def ffn(h, scaled, ssq, layer, **last_part):
        hidden = _ffn_up(scaled, ssq, w_ffn_gate, w_ffn_up, layer, **tiles)
        for p in range(FFN_DOWN_PARTS):
            final = p == FFN_DOWN_PARTS - 1
            outs = _residual(hidden, w_ffn_down, layer, h, part=(p, FFN_DOWN_PARTS),
                             name="ffn_down_%d_%d" % (layer, p), **tiles,
                             **(last_part if final else {}))
            h = outs[0]
        return outs

    x_rows = (x_prompt.reshape(m_prompt, d), x_sample.reshape(m_sample, d))
    xn = _rms_stack(*x_rows, row(norm_a[0]))
    uv = _sgu_in(xn, w_sgu_in, 0, bm=WIDE_TILE, bn=2 * MXU_WIDTH)

    pos_chunk = jnp.arange(SGU_CHUNK) // CHUNK
    ws = jnp.where((pos_chunk[:, None] >= pos_chunk[None, :])[None], w_sgu_s[0], 0.0).astype(BF16)
    bs_t = b_sgu_s[0].T.astype(F32)
    ln_g, ln_b = row(sgu_ln_g[0]), row(sgu_ln_b[0])
    a, v_rows = _sgu_mix(uv, ln_g, ln_b, ws, bs_t, m_prompt=m_prompt, sample_chunk=dec_seq)
    h, scaled, ssq = _residual(a, w_sgu_out, 0, x_rows, name="sgu_out", **tiles,
                               gains=[row(norm_ffn[0])])
    h, scaled_kv, scaled_q, ssq = ffn(h, scaled, ssq, 0, gains=[row(norm_kv), row(norm_b[0])])

    lane_head = jnp.arange(MXU_WIDTH) // HEAD_DIM
    seg = (lane_head[:, None] == lane_head[None, :]).astype(BF16)
    seg2 = jnp.concatenate([seg, seg], axis=0)
    k_rows, v_rows_kv, kz, vz = _kv_proj(
        scaled_kv, ssq, w_kv.astype(BF16), seg2, row(jnp.tile(k_norm, N_KV_HEADS)), bm=KV_TILE)
    q = _q_proj(scaled_q, ssq, w_q, 0, seg2, row(jnp.tile(q_norm[0], MXU_WIDTH // HEAD_DIM)),
                **tiles)
    sink = sinks[0].astype(F32)
    o = _attn(q, kz, vz, _lane_pair_copies(cache_k), _lane_pair_copies(cache_v), sink,
              m_prompt=m_prompt, seq=seq, dec_seq=dec_seq)
    h, scaled, ssq = _residual(o, w_o, 0, h, name="attn_out", **tiles,
                               gains=[row(norm_ffn[1])])
    y_prompt, y_sample = ffn(h, scaled, ssq, 1, split_rows=m_prompt)

    heads = lambda t: t.reshape(t.shape[:-1] + (N_KV_HEADS, HEAD_DIM))
    tail = lambda t: heads(t[:m_prompt].reshape(batch, seq, KV_WIDTH)[:, seq - WINDOW:])
    fresh = lambda t: heads(t[m_prompt:].reshape(streams, dec_seq, KV_WIDTH))
    k_p, v_p = tail(k_rows), tail(v_rows_kv)
    k_s, v_s = fresh(k_rows), fresh(v_rows_kv)
    return (y_prompt.reshape(batch, seq, d), y_sample.reshape(streams, dec_seq, d),
            k_p, v_p, k_s, v_s, v_rows.reshape(1, streams, dec_seq, d))
```

```python
import functools

import jax
import jax.numpy as jnp
from jax import lax
from jax.experimental import pallas as pl
from jax.experimental.pallas import tpu as pltpu

F32 = jnp.float32
BF16 = jnp.bfloat16

D_MODEL = 4096
SGU_CHUNK = 128
SGU_GROUPS = 8
SGU_GROUP_WIDTH = D_MODEL // SGU_GROUPS
CHUNK = 64
HEAD_DIM = 64
N_HEADS = 64
N_KV_HEADS = 8
GQA = N_HEADS // N_KV_HEADS
KV_WIDTH = N_KV_HEADS * HEAD_DIM
WINDOW = 128
NEG = -1e30
RMS_EPS = 1e-6
LN_EPS = 1e-5

V7X_VMEM_BYTES = 64 * 1024 * 1024
VMEM_LIMIT_CAP = V7X_VMEM_BYTES - 6 * 1024 * 1024
LANES = 128
MXU_WIDTH = 256
HEADS_PER_VREG = LANES // HEAD_DIM
PAIRS = GQA // HEADS_PER_VREG

WIDE_TILE = 1408
KV_TILE = WIDE_TILE // 2
NORM_ROWS = 256
ATT_ROWS = 128
RMS_STAT_ROWS = 64
RMS_SCALE_ROWS = 32
FFN_DOWN_PARTS = 2
ROW_GROUPS = 4


def _params(vmem_bytes, n_axes):
    limit = min(int(vmem_bytes) + 8 * 1024 * 1024, VMEM_LIMIT_CAP)
    return pltpu.CompilerParams(
        dimension_semantics=("arbitrary",) * n_axes, vmem_limit_bytes=limit)


def _row_groups(rows):
    size = rows // ROW_GROUPS
    assert size * ROW_GROUPS == rows and size % 16 == 0
    return [slice(g * size, (g + 1) * size) for g in range(ROW_GROUPS)]


_LHS_SPEC = pl.BlockSpec(memory_space=pl.ANY)


def _lhs_scratch(bm, kc):
    return [pltpu.VMEM((2, bm, kc), BF16), pltpu.SemaphoreType.DMA((2,))]


def _lhs_tile(x_hbm, x_buf, sem, col_block=0):
    _, bm, kc = x_buf.shape
    i, n_tiles = pl.program_id(0), pl.num_programs(0)
    slot = lax.rem(i, 2)

    def copy(tile, slot):
        src = x_hbm.at[pl.ds(tile * bm, bm), pl.ds(col_block * kc, kc)]
        return pltpu.make_async_copy(src, x_buf.at[slot], sem.at[slot])

    @pl.when(pl.program_id(1) == 0)
    def _():
        @pl.when(i == 0)
        def _():
            copy(0, 0).start()

        copy(i, slot).wait()

        @pl.when(i + 1 < n_tiles)
        def _():
            copy(i + 1, 1 - slot).start()

    return x_buf.at[slot]


def _layer_spec(k, bn, layer, col_of, row_block=0):
    return pl.BlockSpec((None, k, bn), lambda *idx: (layer, row_block, col_of(*idx)))


def _rms_rows(x_ref, g_ref, xn_ref, r_ref):
    rows, k = x_ref.shape

    def stats(i, carry):
        sl = pl.ds(pl.multiple_of(i * RMS_STAT_ROWS, RMS_STAT_ROWS), RMS_STAT_ROWS)
        x = x_ref[sl, :]
        ms = jnp.mean(x * x, axis=-1, keepdims=True)
        r_ref[sl, :] = jnp.broadcast_to(lax.rsqrt(ms + RMS_EPS), (RMS_STAT_ROWS, LANES))
        return carry

    lax.fori_loop(0, rows // RMS_STAT_ROWS, stats, 0)
    g = g_ref[...]

    def scale(i, carry):
        sl = pl.ds(pl.multiple_of(i * RMS_SCALE_ROWS, RMS_SCALE_ROWS), RMS_SCALE_ROWS)
        r = jnp.tile(r_ref[sl, :], (1, k // LANES))
        xn_ref[sl, :] = (x_ref[sl, :] * r * g).astype(BF16)
        return carry

    lax.fori_loop(0, rows // RMS_SCALE_ROWS, scale, 0)


def _rms_stack_kernel(xp_ref, xs_ref, g_ref, xn_ref, r_ref, *, prompt_blocks):
    from_ref = lambda x_ref: _rms_rows(x_ref, g_ref, xn_ref, r_ref)
    pl.when(pl.program_id(0) < prompt_blocks)(lambda: from_ref(xp_ref))
    pl.when(pl.program_id(0) >= prompt_blocks)(lambda: from_ref(xs_ref))


def _rms_stack(x_prompt, x_sample, gain):
    mp, k = x_prompt.shape
    ms = x_sample.shape[0]
    assert mp % NORM_ROWS == 0 and ms == NORM_ROWS
    prompt_blocks = mp // NORM_ROWS
    m = mp + ms
    vmem = 4 * NORM_ROWS * k * 4 + 2 * NORM_ROWS * k * 2 + 4 * RMS_STAT_ROWS * k * 4
    return pl.pallas_call(
        functools.partial(_rms_stack_kernel, prompt_blocks=prompt_blocks),
        grid=(m // NORM_ROWS,),
        in_specs=[
            pl.BlockSpec((NORM_ROWS, k), lambda i: (jnp.minimum(i, prompt_blocks - 1), 0)),
            pl.BlockSpec((NORM_ROWS, k), lambda i: (0, 0)),
            pl.BlockSpec((1, k), lambda i: (0, 0)),
        ],
        out_specs=pl.BlockSpec((NORM_ROWS, k), lambda i: (i, 0)),
        out_shape=jax.ShapeDtypeStruct((m, k), BF16),
        scratch_shapes=[pltpu.VMEM((NORM_ROWS, LANES), F32)],
        compiler_params=_params(vmem, 1),
        name="rms_stack",
    )(x_prompt, x_sample, gain)


def _head_rms(y, seg2_ref, gain_row):
    y2 = y * y
    hi = y2.astype(BF16)
    lo = (y2 - hi.astype(F32)).astype(BF16)
    seg2 = seg2_ref[...]
    w = seg2.shape[1]
    ssq = jnp.concatenate(
        [jnp.dot(jnp.concatenate([hi[:, c:c + w], lo[:, c:c + w]], axis=1), seg2,
                 preferred_element_type=F32)
         for c in range(0, y.shape[1], w)], axis=-1)
    return y * lax.rsqrt(ssq * (1.0 / HEAD_DIM) + RMS_EPS) * gain_row


def _row_factor(ssq_ref, k, n):
    r = lax.rsqrt(ssq_ref[...] * (1.0 / k) + RMS_EPS)
    return jnp.tile(r, (1, n // LANES))


def _sgu_in_kernel(x_hbm, w_ref, o_ref, x_buf, sem):
    x_ref = _lhs_tile(x_hbm, x_buf, sem)
    w = w_ref[...].astype(BF16)
    for rows in _row_groups(x_ref.shape[0]):
        y = jnp.dot(x_ref[rows, :], w, preferred_element_type=F32)
        o_ref[rows, :] = 0.5 * y * (1.0 + lax.erf(y * (0.5 ** 0.5)))


def _sgu_in(xn, w, layer, *, bm, bn):
    m, k = xn.shape
    n = w.shape[2]
    wb = w.dtype.itemsize
    vmem = 2 * bm * k * 2 + k * bn * (2 * wb + 2) + 5 * bm * bn * 4
    return pl.pallas_call(
        _sgu_in_kernel,
        grid=(m // bm, n // bn),
        in_specs=[_LHS_SPEC, _layer_spec(k, bn, layer, lambda i, j: j)],
        out_specs=pl.BlockSpec((bm, bn), lambda i, j: (i, j)),
        out_shape=jax.ShapeDtypeStruct((m, n), F32),
        scratch_shapes=_lhs_scratch(bm, k),
        compiler_params=_params(vmem, 2),
        name="sgu_in",
    )(xn, w)


def _sgu_mix_rows(u_ref, v_ref, lng_ref, lnb_ref, ws_ref, bs_ref, a_ref, vn_ref, *, chunk):
    ln_g = lng_ref[...]
    ln_b = lnb_ref[...]
    for c in range(u_ref.shape[0] // chunk):
        rows = slice(c * chunk, (c + 1) * chunk)
        v = v_ref[rows, :]
        mu = jnp.mean(v, axis=-1, keepdims=True)
        vc = v - mu
        var = jnp.mean(vc * vc, axis=-1, keepdims=True)
        vn = vc * lax.rsqrt(var + LN_EPS) * ln_g + ln_b
        if vn_ref is not None:
            vn_ref[rows, :] = vn
        vnb = vn.astype(BF16)
        for g in range(SGU_GROUPS):
            cols = slice(g * SGU_GROUP_WIDTH, (g + 1) * SGU_GROUP_WIDTH)
            mixed = jnp.dot(ws_ref[g], vnb[:, cols], preferred_element_type=F32)
            mixed = mixed + bs_ref[:, g:g + 1]
            a_ref[rows, cols] = (u_ref[rows, cols] * mixed).astype(BF16)


def _sgu_mix_kernel(u_ref, v_ref, lng_ref, lnb_ref, ws_ref, bs_ref, wss_ref, bss_ref,
                    a_ref, vn_ref, *, prompt_blocks, sample_chunk):
    common = (u_ref, v_ref, lng_ref, lnb_ref)
    is_prompt = pl.program_id(0) < prompt_blocks
    pl.when(is_prompt)(
        lambda: _sgu_mix_rows(*common, ws_ref, bs_ref, a_ref, None, chunk=SGU_CHUNK))
    pl.when(jnp.logical_not(is_prompt))(
        lambda: _sgu_mix_rows(*common, wss_ref, bss_ref, a_ref, vn_ref, chunk=sample_chunk))


def _sgu_mix(uv, ln_g, ln_b, ws, bs_t, *, m_prompt, sample_chunk):
    m = uv.shape[0]
    d = D_MODEL
    rows = m - m_prompt
    assert m_prompt % rows == 0 and rows % SGU_CHUNK == 0 and rows % sample_chunk == 0
    prompt_blocks = m_prompt // rows
    const = lambda shape: pl.BlockSpec(shape, lambda i: (0,) * len(shape))
    wss, bss = ws[:, :sample_chunk, :sample_chunk], bs_t[:sample_chunk]
    vmem = 2 * 2 * rows * d * 4 + 2 * rows * d * 2 + 2 * rows * d * 4 + 6 * SGU_CHUNK * d * 4
    return pl.pallas_call(
        functools.partial(_sgu_mix_kernel, prompt_blocks=prompt_blocks,
                          sample_chunk=sample_chunk),
        grid=(prompt_blocks + 1,),
        in_specs=[
            pl.BlockSpec((rows, d), lambda i: (i, 0)),
            pl.BlockSpec((rows, d), lambda i: (i, 1)),
            const((1, d)), const((1, d)),
            const(ws.shape), const(bs_t.shape), const(wss.shape), const(bss.shape),
        ],
        out_specs=[pl.BlockSpec((rows, d), lambda i: (i, 0)), const((rows, d))],
        out_shape=[jax.ShapeDtypeStruct((m, d), BF16), jax.ShapeDtypeStruct((rows, d), F32)],
        compiler_params=_params(vmem, 1),
        name="sgu_mix",
    )(uv, uv, ln_g, ln_b, ws, bs_t, wss, bss)


def _residual_kernel(x_hbm, w_ref, *refs, n_res, n_gains, top_rows, tail_rows, col_block):
    x_ref = _lhs_tile(x_hbm, *refs[-2:], col_block=col_block)
    refs = refs[:-2]
    res_refs, refs = refs[:n_res], refs[n_res:]
    gain_refs, refs = refs[:n_gains], refs[n_gains:]
    n_main = 2 if tail_rows else 1
    main_refs, refs = refs[:n_main], refs[n_main:]
    scaled_refs, ssq_refs = refs[:n_gains], refs[n_gains:]
    o_ref = main_refs[0]
    w = w_ref[...].astype(BF16)
    bm = x_ref.shape[0]
    squares = []
    plain = n_res == 1 and n_gains == 0
    for rows in (_row_groups(bm) if plain else [slice(0, bm)]):
        y = jnp.dot(x_ref[rows, :], w, preferred_element_type=F32)
        if n_res == 1:
            h = res_refs[0][rows, :] + y
            o_ref[rows, :] = h
        else:
            top_ref, tail_ref = res_refs
            is_last = pl.program_id(0) == pl.num_programs(0) - 1
            o_ref[:top_rows, :] = top_ref[:top_rows, :] + y[:top_rows]

            @pl.when(jnp.logical_not(is_last))
            def _():
                o_ref[top_rows:, :] = top_ref[top_rows:, :] + y[top_rows:]

            @pl.when(is_last)
            def _():
                o_ref[top_rows:, :] = tail_ref[...] + y[top_rows:]

            h = o_ref[...]
        for g_ref, s_ref in zip(gain_refs, scaled_refs):
            s_ref[rows, :] = (h * g_ref[...]).astype(BF16)
        if n_gains:
            squares.append(jnp.sum(h * h, axis=-1, keepdims=True))
    if tail_rows:
        main_refs[1][...] = o_ref[bm - tail_rows:, :]
    if n_gains:
        (ssq_ref,) = ssq_refs
        part = jnp.broadcast_to(jnp.concatenate(squares, axis=0), ssq_ref.shape)
        first = pl.program_id(1) == 0

        @pl.when(first)
        def _():
            ssq_ref[...] = part

        @pl.when(jnp.logical_not(first))
        def _():
            ssq_ref[...] += part


def _residual(x, w, layer, res, *, bm, bn, name, part=(0, 1), gains=(), split_rows=None):
    m, k = x.shape
    n = w.shape[2]
    p, n_parts = part
    kc = k // n_parts
    assert kc * n_parts == k and kc % LANES == 0
    wb = w.dtype.itemsize
    tile = pl.BlockSpec((bm, bn), lambda i, j: (i, j))
    last = m // bm - 1

    top_rows = 0
    if isinstance(res, tuple):
        top, tail = res
        top_rows = bm - tail.shape[0]
        assert top.shape[0] + tail.shape[0] == m and top_rows > 0 and top_rows % 8 == 0
        res_specs = [tile, pl.BlockSpec((tail.shape[0], bn), lambda i, j: (0, j))]
        res_args = [top, tail]
    else:
        res_specs, res_args = [tile], [res]

    tail_rows = 0
    if split_rows is None:
        out_specs, out_shape = [tile], [jax.ShapeDtypeStruct((m, n), F32)]
    else:
        tail_rows = m - split_rows
        assert last * bm < split_rows and tail_rows <= bm
        out_specs = [tile, pl.BlockSpec((tail_rows, bn),
                                        lambda i, j: (0, jnp.where(i == last, j, 0)))]
        out_shape = [jax.ShapeDtypeStruct((split_rows, n), F32),
                     jax.ShapeDtypeStruct((tail_rows, n), F32)]
    out_specs += [tile] * len(gains)
    out_shape += [jax.ShapeDtypeStruct((m, n), BF16)] * len(gains)
    if gains:
        out_specs.append(pl.BlockSpec((bm, LANES), lambda i, j: (i, 0)))
        out_shape.append(jax.ShapeDtypeStruct((m, LANES), F32))

    vmem = 2 * bm * kc * 2 + kc * bn * (2 * wb + 2) + (8 + 2 * len(gains)) * bm * bn * 4
    return pl.pallas_call(
        functools.partial(_residual_kernel, n_res=len(res_args), n_gains=len(gains),
                          top_rows=top_rows, tail_rows=tail_rows, col_block=p),
        grid=(m // bm, n // bn),
        in_specs=[
            _LHS_SPEC,
            _layer_spec(kc, bn, layer, lambda i, j: j, row_block=p),
        ] + res_specs + [pl.BlockSpec((1, bn), lambda i, j: (0, j))] * len(gains),
        out_specs=out_specs,
        out_shape=out_shape,
        scratch_shapes=_lhs_scratch(bm, kc),
        compiler_params=_params(vmem, 2),
        name=name,
    )(x, w, *res_args, *gains)


def _ffn_up_kernel(x_hbm, ssq_ref, wg_ref, wu_ref, o_ref, x_buf, sem):
    x_ref = _lhs_tile(x_hbm, x_buf, sem)
    wg = wg_ref[...].astype(BF16)
    wu = wu_ref[...].astype(BF16)
    k = x_ref.shape[1]
    for rows in _row_groups(x_ref.shape[0]):
        x = x_ref[rows, :]
        r = _row_factor(ssq_ref.at[rows, :], k, o_ref.shape[1])
        gate = jnp.dot(x, wg, preferred_element_type=F32) * r
        up = jnp.dot(x, wu, preferred_element_type=F32) * r
        o_ref[rows, :] = (0.5 * gate * (1.0 + jnp.tanh(0.5 * gate)) * up).astype(BF16)


def _ffn_up(x, ssq, w_gate, w_up, layer, *, bm, bn):
    m, k = x.shape
    n = w_gate.shape[2]
    wb = w_gate.dtype.itemsize
    vmem = 2 * bm * k * 2 + 2 * k * bn * (2 * wb + 2) + 2 * bm * bn * 2 + 5 * bm * bn * 4
    return pl.pallas_call(
        _ffn_up_kernel,
        grid=(m // bm, n // bn),
        in_specs=[
            _LHS_SPEC,
            pl.BlockSpec((bm, LANES), lambda i, j: (i, 0)),
            _layer_spec(k, bn, layer, lambda i, j: j),
            _layer_spec(k, bn, layer, lambda i, j: j),
        ],
        scratch_shapes=_lhs_scratch(bm, k),
        out_specs=pl.BlockSpec((bm, bn), lambda i, j: (i, j)),
        out_shape=jax.ShapeDtypeStruct((m, n), BF16),
        compiler_params=_params(vmem, 2),
        name="ffn_up",
    )(x, ssq, w_gate, w_up)


LOG2_E = 1.4426950408889634
SCORE_SCALE = HEAD_DIM ** -0.5 * LOG2_E


def _q_proj_kernel(x_hbm, ssq_ref, w_ref, seg2_ref, qg_ref, q_ref, x_buf, sem):
    x = _lhs_tile(x_hbm, x_buf, sem)[...]
    y = jnp.dot(x, w_ref[...].astype(BF16), preferred_element_type=F32)
    y = y * _row_factor(ssq_ref, x.shape[1], y.shape[1])
    q_ref[...] = _head_rms(y, seg2_ref, qg_ref[...]).astype(BF16)


def _q_proj(x, ssq, w, layer, seg2, q_gain, *, bm, bn):
    m, k = x.shape
    n = w.shape[2]
    wb = w.dtype.itemsize
    vmem = 2 * bm * k * 2 + k * bn * (2 * wb + 2) + 10 * bm * bn * 4
    return pl.pallas_call(
        _q_proj_kernel,
        grid=(m // bm, n // bn),
        in_specs=[
            _LHS_SPEC,
            pl.BlockSpec((bm, LANES), lambda i, j: (i, 0)),
            _layer_spec(k, bn, layer, lambda i, j: j),
            pl.BlockSpec(seg2.shape, lambda i, j: (0, 0)),
            pl.BlockSpec((1, bn), lambda i, j: (0, 0)),
        ],
        out_specs=pl.BlockSpec((bm, bn), lambda i, j: (i, j)),
        out_shape=jax.ShapeDtypeStruct((m, n), BF16),
        scratch_shapes=_lhs_scratch(bm, k),
        compiler_params=_params(vmem, 2),
        name="q_proj",
    )(x, ssq, w, seg2, q_gain)


def _kv_proj_kernel(x_ref, ssq_ref, w_ref, seg2_ref, kg_ref, kf_ref, vf_ref, kz_ref, vz_ref):
    x = x_ref[...]
    y = jnp.dot(x, w_ref[...], preferred_element_type=F32)
    y = y * _row_factor(ssq_ref, x.shape[1], y.shape[1])
    kn = _head_rms(y[:, :KV_WIDTH], seg2_ref, kg_ref[...])
    v = y[:, KV_WIDTH:]
    kf_ref[...] = kn
    vf_ref[...] = v
    low = lax.broadcasted_iota(jnp.int32, (y.shape[0], LANES), 1) < HEAD_DIM
    for src, dst in ((kn, kz_ref), (v, vz_ref)):
        for p in range(N_KV_HEADS // HEADS_PER_VREG):
            both = src[:, p * LANES:(p + 1) * LANES]
            swapped = pltpu.roll(both, HEAD_DIM, axis=1)
            dst[2 * p, 0] = jnp.where(low, both, 0.0).astype(BF16)
            dst[2 * p, 1] = jnp.where(low, 0.0, swapped).astype(BF16)
            dst[2 * p + 1, 0] = jnp.where(low, swapped, 0.0).astype(BF16)
            dst[2 * p + 1, 1] = jnp.where(low, 0.0, both).astype(BF16)


def _kv_proj(x, ssq, w, seg2, k_gain, *, bm):
    m, k = x.shape
    n = w.shape[1]
    vmem = (2 * bm * k * 2 + 2 * k * n * 2 + 8 * bm * n * 4
            + 2 * 2 * N_KV_HEADS * 2 * bm * LANES * 2)
    z_spec = pl.BlockSpec((N_KV_HEADS, 2, bm, LANES), lambda i: (0, 0, i, 0))
    row_spec = pl.BlockSpec((bm, KV_WIDTH), lambda i: (i, 0))
    z_shape = jax.ShapeDtypeStruct((N_KV_HEADS, 2, m, LANES), BF16)
    return pl.pallas_call(
        _kv_proj_kernel,
        grid=(m // bm,),
        in_specs=[
            pl.BlockSpec((bm, k), lambda i: (i, 0)),
            pl.BlockSpec((bm, LANES), lambda i: (i, 0)),
            pl.BlockSpec((k, n), lambda i: (0, 0)),
            pl.BlockSpec(seg2.shape, lambda i: (0, 0)),
            pl.BlockSpec((1, KV_WIDTH), lambda i: (0, 0)),
        ],
        out_specs=[row_spec, row_spec, z_spec, z_spec],
        out_shape=[
            jax.ShapeDtypeStruct((m, KV_WIDTH), F32),
            jax.ShapeDtypeStruct((m, KV_WIDTH), F32),
            z_shape, z_shape,
        ],
        compiler_params=_params(vmem, 1),
        name="kv_proj",
    )(x, ssq, w, seg2, k_gain)


def _attend(q_ref, o_ref, sink_ref, key_parts, val_parts, *, row0, tq, mask_lanes=None):
    rows = PAIRS * tq
    lane_pair = lax.broadcasted_iota(jnp.int32, (1, rows), 1) >> (tq.bit_length() - 1)
    for kh in range(N_KV_HEADS):
        base = kh * GQA * HEAD_DIM
        q4 = jnp.concatenate(
            [q_ref[row0:row0 + tq, base + p * LANES:base + (p + 1) * LANES]
             for p in range(PAIRS)], axis=0)
        kk = jnp.concatenate(key_parts(kh, 0) + key_parts(kh, 1), axis=0)
        vv = jnp.concatenate(val_parts(kh, 0) + val_parts(kh, 1), axis=0)
        n_keys = kk.shape[0] // HEADS_PER_VREG
        if mask_lanes is not None:
            kk = jnp.concatenate([kk, mask_lanes[0]], axis=1)
            q4 = jnp.concatenate([q4, mask_lanes[1]], axis=1)
        s_t = lax.dot_general(kk, q4, (((1,), (1,)), ((), ())),
                              preferred_element_type=F32)
        exps, denoms = [], []
        for parity in range(HEADS_PER_VREG):
            head = kh * GQA + parity
            sink = jnp.full((1, rows), sink_ref[head + HEADS_PER_VREG * (PAIRS - 1)], F32)
            for p in range(PAIRS - 1):
                sink = jnp.where(lane_pair == p, sink_ref[head + HEADS_PER_VREG * p], sink)
            sink = sink * LOG2_E
            half = s_t[parity * n_keys:(parity + 1) * n_keys]
            mx = jnp.maximum(jnp.max(half, axis=0, keepdims=True), sink)
            e = jnp.exp2(half - mx)
            denom = jnp.sum(e, axis=0, keepdims=True) + jnp.exp2(sink - mx)
            denoms.append(jnp.broadcast_to(denom, (HEAD_DIM, rows)))
            exps.append(e.astype(BF16))
        o_t = lax.dot_general(vv, jnp.concatenate(exps, axis=0),
                              (((0,), (0,)), ((), ())), preferred_element_type=F32)
        o = (o_t / jnp.concatenate(denoms, axis=0)).T
        for p in range(PAIRS):
            o_ref[row0:row0 + tq, base + p * LANES:base + (p + 1) * LANES] = (
                o[p * tq:(p + 1) * tq].astype(BF16))


def _attend_prompt_block(q_ref, kp_ref, kc_ref, vp_ref, vc_ref, sink_ref, o_ref, first):
    tq = ATT_ROWS
    n_keys = 2 * tq
    chunk_shift = CHUNK.bit_length() - 1
    chunks_per_block = tq // CHUNK
    lane = lax.broadcasted_iota(jnp.int32, (2 * n_keys, LANES), 1)
    key = lax.broadcasted_iota(jnp.int32, (2 * n_keys, LANES), 0)
    k_chunk = ((key & (n_keys - 1)) >> chunk_shift) - chunks_per_block
    oldest = jnp.where(first, 0, -chunks_per_block)
    visible = (k_chunk <= lane) & (k_chunk >= jnp.maximum(lane - 2, oldest))
    key_side = jnp.where((lane < chunks_per_block) & jnp.logical_not(visible), NEG, 0.0)
    query = lax.broadcasted_iota(jnp.int32, (PAIRS * tq, LANES), 0)
    q_lane = lax.broadcasted_iota(jnp.int32, (PAIRS * tq, LANES), 1)
    query_side = jnp.where(((query & (tq - 1)) >> chunk_shift) == q_lane, 1.0, 0.0)
    parts = lambda prev, cur: (lambda kh, parity: [prev[kh, parity], cur[kh, parity]])
    _attend(q_ref, o_ref, sink_ref, parts(kp_ref, kc_ref), parts(vp_ref, vc_ref),
            row0=0, tq=tq, mask_lanes=(key_side.astype(BF16), query_side.astype(BF16)))


def _attend_sample_block(q_ref, ck_ref, cv_ref, kc_ref, vc_ref, sink_ref, o_ref, *, dec_seq):
    for st in range(ATT_ROWS // dec_seq):
        new = slice(st * dec_seq, (st + 1) * dec_seq)
        parts = lambda cache, cur, st=st, new=new: (
            lambda kh, parity: [cache[st, kh, parity], cur[kh, parity, new, :]])
        _attend(q_ref, o_ref, sink_ref, parts(ck_ref, kc_ref), parts(cv_ref, vc_ref),
                row0=st * dec_seq, tq=dec_seq)


def _attn_kernel(q_ref, kp_ref, kc_ref, vp_ref, vc_ref, ck_ref, cv_ref, sink_ref, o_ref, *,
                 prompt_blocks, blocks_per_seq, dec_seq):
    step = pl.program_id(0)
    is_prompt = step < prompt_blocks
    first = (step & (blocks_per_seq - 1)) == 0
    pl.when(is_prompt)(lambda: _attend_prompt_block(
        q_ref, kp_ref, kc_ref, vp_ref, vc_ref, sink_ref, o_ref, first))
    pl.when(jnp.logical_not(is_prompt))(lambda: _attend_sample_block(
        q_ref, ck_ref, cv_ref, kc_ref, vc_ref, sink_ref, o_ref, dec_seq=dec_seq))


def _attn(q, kz, vz, cache_kz, cache_vz, sinks, *, m_prompt, seq, dec_seq):
    m = q.shape[0]
    tq = ATT_ROWS
    blocks_per_seq = seq // tq
    prompt_blocks = m_prompt // tq
    streams_per_block = tq // dec_seq
    assert blocks_per_seq & (blocks_per_seq - 1) == 0 and (m - m_prompt) % tq == 0
    rows_here = lambda s: (s, 0)
    cur = lambda s: (0, 0, s, 0)
    prev = lambda s: (0, 0, jnp.where((s & (blocks_per_seq - 1)) == 0, s, s - 1), 0)
    cache = lambda s: (jnp.maximum(s - prompt_blocks, 0), 0, 0, 0, 0)
    z_block = (N_KV_HEADS, 2, tq, LANES)
    cache_block = (streams_per_block, N_KV_HEADS, 2, WINDOW, LANES)
    vmem = 4 * tq * D_MODEL * 2 + 8 * N_KV_HEADS * 2 * tq * LANES * 2 \
        + 4 * streams_per_block * N_KV_HEADS * 2 * WINDOW * LANES * 2 \
        + 10 * PAIRS * tq * 4 * tq * 4
    return pl.pallas_call(
        functools.partial(_attn_kernel, prompt_blocks=prompt_blocks,
                          blocks_per_seq=blocks_per_seq, dec_seq=dec_seq),
        grid=(m // tq,),
        in_specs=[
            pl.BlockSpec((tq, D_MODEL), rows_here),
            pl.BlockSpec(z_block, prev),
            pl.BlockSpec(z_block, cur),
            pl.BlockSpec(z_block, prev),
            pl.BlockSpec(z_block, cur),
            pl.BlockSpec(cache_block, cache),
            pl.BlockSpec(cache_block, cache),
            pl.BlockSpec(memory_space=pltpu.SMEM),
        ],
        out_specs=pl.BlockSpec((tq, D_MODEL), rows_here),
        out_shape=jax.ShapeDtypeStruct((m, D_MODEL), BF16),
        compiler_params=_params(vmem, 1),
        name="attn",
    )(q, kz, kz, vz, vz, cache_kz, cache_vz, sinks)


def _lane_pair_copies(cache):
    c = cache.transpose(0, 2, 1, 3).astype(BF16)
    z = jnp.zeros_like(c)
    return jnp.stack([jnp.concatenate([c, z], axis=-1), jnp.concatenate([z, c], axis=-1)], axis=2)


def kernel(x_prompt, x_sample, cache_k, cache_v, norm_a, w_sgu_in, sgu_ln_g, sgu_ln_b, w_sgu_s,
           b_sgu_s, w_sgu_out, norm_kv, w_kv, k_norm, norm_b, w_q, q_norm, sinks, w_o,
           norm_ffn, w_ffn_gate, w_ffn_up, w_ffn_down):
    batch, seq, d = x_prompt.shape
    streams, dec_seq, _ = x_sample.shape
    m_prompt = batch * seq
    m_sample = streams * dec_seq
    m = m_prompt + m_sample
    assert d == D_MODEL and seq % ATT_ROWS == 0
    assert m % WIDE_TILE == 0 and m % KV_TILE == 0
    assert norm_a.shape[0] == 1 and norm_b.shape[0] == 1 and norm_ffn.shape[0] == 2

    row = lambda g: g.reshape(1, -1).astype(F32)
    tiles = dict(bm=WIDE_TILE, bn=MXU_WIDTH)

    def ffn(h, scaled, ssq, layer, **last_part):
        hidden = _ffn_up(scaled, ssq, w_ffn_gate, w_ffn_up, layer, **tiles)
        for p in range(FFN_DOWN_PARTS):
            final = p == FFN_DOWN_PARTS - 1
            outs = _residual(hidden, w_ffn_down, layer, h, part=(p, FFN_DOWN_PARTS),
                             name="ffn_down_%d_%d" % (layer, p), **tiles,
                             **(last_part if final else {}))
            h = outs[0]
        return outs

    x_rows = (x_prompt.reshape(m_prompt, d), x_sample.reshape(m_sample, d))
    xn = _rms_stack(*x_rows, row(norm_a[0]))
    uv = _sgu_in(xn, w_sgu_in, 0, bm=WIDE_TILE, bn=2 * MXU_WIDTH)

    pos_chunk = jnp.arange(SGU_CHUNK) // CHUNK
    ws = jnp.where((pos_chunk[:, None] >= pos_chunk[None, :])[None], w_sgu_s[0], 0.0).astype(BF16)
    bs_t = b_sgu_s[0].T.astype(F32)
    ln_g, ln_b = row(sgu_ln_g[0]), row(sgu_ln_b[0])
    a, v_rows = _sgu_mix(uv, ln_g, ln_b, ws, bs_t, m_prompt=m_prompt, sample_chunk=dec_seq)
    h, scaled, ssq = _residual(a, w_sgu_out, 0, x_rows, name="sgu_out", **tiles,
                               gains=[row(norm_ffn[0])])
    h, scaled_kv, scaled_q, ssq = ffn(h, scaled, ssq, 0, gains=[row(norm_kv), row(norm_b[0])])

    lane_head = jnp.arange(MXU_WIDTH) // HEAD_DIM
    seg = (lane_head[:, None] == lane_head[None, :]).astype(BF16)
    seg2 = jnp.concatenate([seg, seg], axis=0)
    k_rows, v_rows_kv, kz, vz = _kv_proj(
        scaled_kv, ssq, w_kv.astype(BF16), seg2, row(jnp.tile(k_norm, N_KV_HEADS)), bm=KV_TILE)
    q_gain = row(jnp.tile(q_norm[0], MXU_WIDTH // HEAD_DIM)) * SCORE_SCALE
    q = _q_proj(scaled_q, ssq, w_q, 0, seg2, q_gain, **tiles)
    sink = sinks[0].astype(F32)
    o = _attn(q, kz, vz, _lane_pair_copies(cache_k), _lane_pair_copies(cache_v), sink,
              m_prompt=m_prompt, seq=seq, dec_seq=dec_seq)
    h, scaled, ssq = _residual(o, w_o, 0, h, name="attn_out", **tiles,
                               gains=[row(norm_ffn[1])])
    y_prompt, y_sample = ffn(h, scaled, ssq, 1, split_rows=m_prompt)

    heads = lambda t: t.reshape(t.shape[:-1] + (N_KV_HEADS, HEAD_DIM))
    tail = lambda t: heads(t[:m_prompt].reshape(batch, seq, KV_WIDTH)[:, seq - WINDOW:])
    fresh = lambda t: heads(t[m_prompt:].reshape(streams, dec_seq, KV_WIDTH))
    k_p, v_p = tail(k_rows), tail(v_rows_kv)
    k_s, v_s = fresh(k_rows), fresh(v_rows_kv)
    return (y_prompt.reshape(batch, seq, d), y_sample.reshape(streams, dec_seq, d),
            k_p, v_p, k_s, v_s, v_rows.reshape(1, streams, dec_seq, d))
```

```python
import functools

import jax
import jax.numpy as jnp
from jax import lax
from jax.experimental import pallas as pl
from jax.experimental.pallas import tpu as pltpu

F32 = jnp.float32
BF16 = jnp.bfloat16

D_MODEL = 4096
SGU_CHUNK = 128
SGU_GROUPS = 8
SGU_GROUP_WIDTH = D_MODEL // SGU_GROUPS
CHUNK = 64
HEAD_DIM = 64
N_HEADS = 64
N_KV_HEADS = 8
GQA = N_HEADS // N_KV_HEADS
KV_WIDTH = N_KV_HEADS * HEAD_DIM
WINDOW = 128
NEG = -1e30
RMS_EPS = 1e-6
LN_EPS = 1e-5

V7X_VMEM_BYTES = 64 * 1024 * 1024
VMEM_LIMIT_CAP = V7X_VMEM_BYTES - 6 * 1024 * 1024
LANES = 128
MXU_WIDTH = 256
HEADS_PER_VREG = LANES // HEAD_DIM
PAIRS = GQA // HEADS_PER_VREG

WIDE_TILE = 1408
KV_TILE = WIDE_TILE // 2
NORM_ROWS = 256
ATT_ROWS = 128
RMS_STAT_ROWS = 64
RMS_SCALE_ROWS = 32
FFN_DOWN_PARTS = 2
ROW_GROUPS = 4


def _params(vmem_bytes, n_axes):
    limit = min(int(vmem_bytes) + 8 * 1024 * 1024, VMEM_LIMIT_CAP)
    return pltpu.CompilerParams(
        dimension_semantics=("arbitrary",) * n_axes, vmem_limit_bytes=limit)


def _row_groups(rows):
    size = rows // ROW_GROUPS
    assert size * ROW_GROUPS == rows and size % 16 == 0
    return [slice(g * size, (g + 1) * size) for g in range(ROW_GROUPS)]


_LHS_SPEC = pl.BlockSpec(memory_space=pl.ANY)


def _lhs_scratch(bm, kc):
    return [pltpu.VMEM((2, bm, kc), BF16), pltpu.SemaphoreType.DMA((2,))]


def _lhs_tile(x_hbm, x_buf, sem, col_block=0):
    _, bm, kc = x_buf.shape
    i, n_tiles = pl.program_id(0), pl.num_programs(0)
    slot = lax.rem(i, 2)

    def copy(tile, slot):
        src = x_hbm.at[pl.ds(tile * bm, bm), pl.ds(col_block * kc, kc)]
        return pltpu.make_async_copy(src, x_buf.at[slot], sem.at[slot])

    @pl.when(pl.program_id(1) == 0)
    def _():
        @pl.when(i == 0)
        def _():
            copy(0, 0).start()

        copy(i, slot).wait()

        @pl.when(i + 1 < n_tiles)
        def _():
            copy(i + 1, 1 - slot).start()

    return x_buf.at[slot]


def _layer_spec(k, bn, layer, col_of, row_block=0):
    return pl.BlockSpec((None, k, bn), lambda *idx: (layer, row_block, col_of(*idx)))


def _rms_rows(x_ref, g_ref, xn_ref, r_ref):
    rows, k = x_ref.shape

    def stats(i, carry):
        sl = pl.ds(pl.multiple_of(i * RMS_STAT_ROWS, RMS_STAT_ROWS), RMS_STAT_ROWS)
        x = x_ref[sl, :]
        ms = jnp.mean(x * x, axis=-1, keepdims=True)
        r_ref[sl, :] = jnp.broadcast_to(lax.rsqrt(ms + RMS_EPS), (RMS_STAT_ROWS, LANES))
        return carry

    lax.fori_loop(0, rows // RMS_STAT_ROWS, stats, 0)
    g = g_ref[...]

    def scale(i, carry):
        sl = pl.ds(pl.multiple_of(i * RMS_SCALE_ROWS, RMS_SCALE_ROWS), RMS_SCALE_ROWS)
        r = jnp.tile(r_ref[sl, :], (1, k // LANES))
        xn_ref[sl, :] = (x_ref[sl, :] * r * g).astype(BF16)
        return carry

    lax.fori_loop(0, rows // RMS_SCALE_ROWS, scale, 0)


def _rms_stack_kernel(xp_ref, xs_ref, g_ref, xn_ref, r_ref, *, prompt_blocks):
    from_ref = lambda x_ref: _rms_rows(x_ref, g_ref, xn_ref, r_ref)
    pl.when(pl.program_id(0) < prompt_blocks)(lambda: from_ref(xp_ref))
    pl.when(pl.program_id(0) >= prompt_blocks)(lambda: from_ref(xs_ref))


def _rms_stack(x_prompt, x_sample, gain):
    mp, k = x_prompt.shape
    ms = x_sample.shape[0]
    assert mp % NORM_ROWS == 0 and ms == NORM_ROWS
    prompt_blocks = mp // NORM_ROWS
    m = mp + ms
    vmem = 4 * NORM_ROWS * k * 4 + 2 * NORM_ROWS * k * 2 + 4 * RMS_STAT_ROWS * k * 4
    return pl.pallas_call(
        functools.partial(_rms_stack_kernel, prompt_blocks=prompt_blocks),
        grid=(m // NORM_ROWS,),
        in_specs=[
            pl.BlockSpec((NORM_ROWS, k), lambda i: (jnp.minimum(i, prompt_blocks - 1), 0)),
            pl.BlockSpec((NORM_ROWS, k), lambda i: (0, 0)),
            pl.BlockSpec((1, k), lambda i: (0, 0)),
        ],
        out_specs=pl.BlockSpec((NORM_ROWS, k), lambda i: (i, 0)),
        out_shape=jax.ShapeDtypeStruct((m, k), BF16),
        scratch_shapes=[pltpu.VMEM((NORM_ROWS, LANES), F32)],
        compiler_params=_params(vmem, 1),
        name="rms_stack",
    )(x_prompt, x_sample, gain)


def _head_rms(y, seg2_ref, gain_row):
    y2 = y * y
    hi = y2.astype(BF16)
    lo = (y2 - hi.astype(F32)).astype(BF16)
    seg2 = seg2_ref[...]
    w = seg2.shape[1]
    ssq = jnp.concatenate(
        [jnp.dot(jnp.concatenate([hi[:, c:c + w], lo[:, c:c + w]], axis=1), seg2,
                 preferred_element_type=F32)
         for c in range(0, y.shape[1], w)], axis=-1)
    return y * lax.rsqrt(ssq * (1.0 / HEAD_DIM) + RMS_EPS) * gain_row


def _row_factor(ssq_ref, k, n):
    r = lax.rsqrt(ssq_ref[...] * (1.0 / k) + RMS_EPS)
    return jnp.tile(r, (1, n // LANES))


def _sgu_in_kernel(x_hbm, w_ref, o_ref, x_buf, sem):
    x_ref = _lhs_tile(x_hbm, x_buf, sem)
    w = w_ref[...].astype(BF16)
    for rows in _row_groups(x_ref.shape[0]):
        y = jnp.dot(x_ref[rows, :], w, preferred_element_type=F32)
        o_ref[rows, :] = 0.5 * y * (1.0 + lax.erf(y * (0.5 ** 0.5)))


def _sgu_in(xn, w, layer, *, bm, bn):
    m, k = xn.shape
    n = w.shape[2]
    wb = w.dtype.itemsize
    vmem = 2 * bm * k * 2 + k * bn * (2 * wb + 2) + 5 * bm * bn * 4
    return pl.pallas_call(
        _sgu_in_kernel,
        grid=(m // bm, n // bn),
        in_specs=[_LHS_SPEC, _layer_spec(k, bn, layer, lambda i, j: j)],
        out_specs=pl.BlockSpec((bm, bn), lambda i, j: (i, j)),
        out_shape=jax.ShapeDtypeStruct((m, n), F32),
        scratch_shapes=_lhs_scratch(bm, k),
        compiler_params=_params(vmem, 2),
        name="sgu_in",
    )(xn, w)


def _sgu_mix_rows(u_ref, v_ref, lng_ref, lnb_ref, ws_ref, bs_ref, a_ref, vn_ref, *, chunk):
    ln_g = lng_ref[...]
    ln_b = lnb_ref[...]
    for c in range(u_ref.shape[0] // chunk):
        rows = slice(c * chunk, (c + 1) * chunk)
        v = v_ref[rows, :]
        mu = jnp.mean(v, axis=-1, keepdims=True)
        vc = v - mu
        var = jnp.mean(vc * vc, axis=-1, keepdims=True)
        vn = vc * lax.rsqrt(var + LN_EPS) * ln_g + ln_b
        if vn_ref is not None:
            vn_ref[rows, :] = vn
        vnb = vn.astype(BF16)
        for g in range(SGU_GROUPS):
            cols = slice(g * SGU_GROUP_WIDTH, (g + 1) * SGU_GROUP_WIDTH)
            mixed = jnp.dot(ws_ref[g], vnb[:, cols], preferred_element_type=F32)
            mixed = mixed + bs_ref[:, g:g + 1]
            a_ref[rows, cols] = (u_ref[rows, cols] * mixed).astype(BF16)


def _sgu_mix_kernel(u_ref, v_ref, lng_ref, lnb_ref, ws_ref, bs_ref, wss_ref, bss_ref,
                    a_ref, vn_ref, *, prompt_blocks, sample_chunk):
    common = (u_ref, v_ref, lng_ref, lnb_ref)
    is_prompt = pl.program_id(0) < prompt_blocks
    pl.when(is_prompt)(
        lambda: _sgu_mix_rows(*common, ws_ref, bs_ref, a_ref, None, chunk=SGU_CHUNK))
    pl.when(jnp.logical_not(is_prompt))(
        lambda: _sgu_mix_rows(*common, wss_ref, bss_ref, a_ref, vn_ref, chunk=sample_chunk))


def _sgu_mix(uv, ln_g, ln_b, ws, bs_t, *, m_prompt, sample_chunk):
    m = uv.shape[0]
    d = D_MODEL
    rows = m - m_prompt
    assert m_prompt % rows == 0 and rows % SGU_CHUNK == 0 and rows % sample_chunk == 0
    prompt_blocks = m_prompt // rows
    const = lambda shape: pl.BlockSpec(shape, lambda i: (0,) * len(shape))
    wss, bss = ws[:, :sample_chunk, :sample_chunk], bs_t[:sample_chunk]
    vmem = 2 * 2 * rows * d * 4 + 2 * rows * d * 2 + 2 * rows * d * 4 + 6 * SGU_CHUNK * d * 4
    return pl.pallas_call(
        functools.partial(_sgu_mix_kernel, prompt_blocks=prompt_blocks,
                          sample_chunk=sample_chunk),
        grid=(prompt_blocks + 1,),
        in_specs=[
            pl.BlockSpec((rows, d), lambda i: (i, 0)),
            pl.BlockSpec((rows, d), lambda i: (i, 1)),
            const((1, d)), const((1, d)),
            const(ws.shape), const(bs_t.shape), const(wss.shape), const(bss.shape),
        ],
        out_specs=[pl.BlockSpec((rows, d), lambda i: (i, 0)), const((rows, d))],
        out_shape=[jax.ShapeDtypeStruct((m, d), BF16), jax.ShapeDtypeStruct((rows, d), F32)],
        compiler_params=_params(vmem, 1),
        name="sgu_mix",
    )(uv, uv, ln_g, ln_b, ws, bs_t, wss, bss)


def _residual_kernel(x_hbm, w_ref, *refs, n_res, n_gains, top_rows, tail_rows, col_block):
    x_ref = _lhs_tile(x_hbm, *refs[-2:], col_block=col_block)
    refs = refs[:-2]
    res_refs, refs = refs[:n_res], refs[n_res:]
    gain_refs, refs = refs[:n_gains], refs[n_gains:]
    n_main = 2 if tail_rows else 1
    main_refs, refs = refs[:n_main], refs[n_main:]
    scaled_refs, ssq_refs = refs[:n_gains], refs[n_gains:]
    o_ref = main_refs[0]
    w = w_ref[...].astype(BF16)
    bm = x_ref.shape[0]
    squares = []
    plain = n_res == 1 and n_gains == 0
    for rows in (_row_groups(bm) if plain else [slice(0, bm)]):
        y = jnp.dot(x_ref[rows, :], w, preferred_element_type=F32)
        if n_res == 1:
            h = res_refs[0][rows, :] + y
            o_ref[rows, :] = h
        else:
            top_ref, tail_ref = res_refs
            is_last = pl.program_id(0) == pl.num_programs(0) - 1
            straddle = jnp.where(is_last, tail_ref[...], top_ref[top_rows:, :])
            h = jnp.concatenate([top_ref[:top_rows, :], straddle], axis=0) + y
            o_ref[...] = h
        for g_ref, s_ref in zip(gain_refs, scaled_refs):
            s_ref[rows, :] = (h * g_ref[...]).astype(BF16)
        if n_gains:
            squares.append(jnp.sum(h * h, axis=-1, keepdims=True))
    if tail_rows:
        main_refs[1][...] = o_ref[bm - tail_rows:, :]
    if n_gains:
        (ssq_ref,) = ssq_refs
        part = jnp.broadcast_to(jnp.concatenate(squares, axis=0), ssq_ref.shape)
        first = pl.program_id(1) == 0

        @pl.when(first)
        def _():
            ssq_ref[...] = part

        @pl.when(jnp.logical_not(first))
        def _():
            ssq_ref[...] += part


def _residual(x, w, layer, res, *, bm, bn, name, part=(0, 1), gains=(), split_rows=None):
    m, k = x.shape
    n = w.shape[2]
    p, n_parts = part
    kc = k // n_parts
    assert kc * n_parts == k and kc % LANES == 0
    wb = w.dtype.itemsize
    tile = pl.BlockSpec((bm, bn), lambda i, j: (i, j))
    last = m // bm - 1

    top_rows = 0
    if isinstance(res, tuple):
        top, tail = res
        top_rows = bm - tail.shape[0]
        assert top.shape[0] + tail.shape[0] == m and top_rows > 0 and top_rows % 8 == 0
        res_specs = [tile, pl.BlockSpec((tail.shape[0], bn), lambda i, j: (0, j))]
        res_args = [top, tail]
    else:
        res_specs, res_args = [tile], [res]

    tail_rows = 0
    if split_rows is None:
        out_specs, out_shape = [tile], [jax.ShapeDtypeStruct((m, n), F32)]
    else:
        tail_rows = m - split_rows
        assert last * bm < split_rows and tail_rows <= bm
        out_specs = [tile, pl.BlockSpec((tail_rows, bn),
                                        lambda i, j: (0, jnp.where(i == last, j, 0)))]
        out_shape = [jax.ShapeDtypeStruct((split_rows, n), F32),
                     jax.ShapeDtypeStruct((tail_rows, n), F32)]
    out_specs += [tile] * len(gains)
    out_shape += [jax.ShapeDtypeStruct((m, n), BF16)] * len(gains)
    if gains:
        out_specs.append(pl.BlockSpec((bm, LANES), lambda i, j: (i, 0)))
        out_shape.append(jax.ShapeDtypeStruct((m, LANES), F32))

    vmem = 2 * bm * kc * 2 + kc * bn * (2 * wb + 2) + (8 + 2 * len(gains)) * bm * bn * 4
    return pl.pallas_call(
        functools.partial(_residual_kernel, n_res=len(res_args), n_gains=len(gains),
                          top_rows=top_rows, tail_rows=tail_rows, col_block=p),
        grid=(m // bm, n // bn),
        in_specs=[
            _LHS_SPEC,
            _layer_spec(kc, bn, layer, lambda i, j: j, row_block=p),
        ] + res_specs + [pl.BlockSpec((1, bn), lambda i, j: (0, j))] * len(gains),
        out_specs=out_specs,
        out_shape=out_shape,
        scratch_shapes=_lhs_scratch(bm, kc),
        compiler_params=_params(vmem, 2),
        name=name,
    )(x, w, *res_args, *gains)


def _ffn_up_kernel(x_hbm, ssq_ref, wg_ref, wu_ref, o_ref, x_buf, sem):
    x_ref = _lhs_tile(x_hbm, x_buf, sem)
    wg = wg_ref[...].astype(BF16)
    wu = wu_ref[...].astype(BF16)
    k = x_ref.shape[1]
    for rows in _row_groups(x_ref.shape[0]):
        x = x_ref[rows, :]
        r = _row_factor(ssq_ref.at[rows, :], k, o_ref.shape[1])
        gate = jnp.dot(x, wg, preferred_element_type=F32) * r
        up = jnp.dot(x, wu, preferred_element_type=F32) * r
        o_ref[rows, :] = (0.5 * gate * (1.0 + jnp.tanh(0.5 * gate)) * up).astype(BF16)


def _ffn_up(x, ssq, w_gate, w_up, layer, *, bm, bn):
    m, k = x.shape
    n = w_gate.shape[2]
    wb = w_gate.dtype.itemsize
    vmem = 2 * bm * k * 2 + 2 * k * bn * (2 * wb + 2) + 2 * bm * bn * 2 + 5 * bm * bn * 4
    return pl.pallas_call(
        _ffn_up_kernel,
        grid=(m // bm, n // bn),
        in_specs=[
            _LHS_SPEC,
            pl.BlockSpec((bm, LANES), lambda i, j: (i, 0)),
            _layer_spec(k, bn, layer, lambda i, j: j),
            _layer_spec(k, bn, layer, lambda i, j: j),
        ],
        scratch_shapes=_lhs_scratch(bm, k),
        out_specs=pl.BlockSpec((bm, bn), lambda i, j: (i, j)),
        out_shape=jax.ShapeDtypeStruct((m, n), BF16),
        compiler_params=_params(vmem, 2),
        name="ffn_up",
    )(x, ssq, w_gate, w_up)


LOG2_E = 1.4426950408889634
SCORE_SCALE = HEAD_DIM ** -0.5 * LOG2_E


def _q_proj_kernel(x_hbm, ssq_ref, w_ref, seg2_ref, qg_ref, q_ref, x_buf, sem):
    x = _lhs_tile(x_hbm, x_buf, sem)[...]
    y = jnp.dot(x, w_ref[...].astype(BF16), preferred_element_type=F32)
    y = y * _row_factor(ssq_ref, x.shape[1], y.shape[1])
    q_ref[...] = _head_rms(y, seg2_ref, qg_ref[...]).astype(BF16)


def _q_proj(x, ssq, w, layer, seg2, q_gain, *, bm, bn):
    m, k = x.shape
    n = w.shape[2]
    wb = w.dtype.itemsize
    vmem = 2 * bm * k * 2 + k * bn * (2 * wb + 2) + 10 * bm * bn * 4
    return pl.pallas_call(
        _q_proj_kernel,
        grid=(m // bm, n // bn),
        in_specs=[
            _LHS_SPEC,
            pl.BlockSpec((bm, LANES), lambda i, j: (i, 0)),
            _layer_spec(k, bn, layer, lambda i, j: j),
            pl.BlockSpec(seg2.shape, lambda i, j: (0, 0)),
            pl.BlockSpec((1, bn), lambda i, j: (0, 0)),
        ],
        out_specs=pl.BlockSpec((bm, bn), lambda i, j: (i, j)),
        out_shape=jax.ShapeDtypeStruct((m, n), BF16),
        scratch_shapes=_lhs_scratch(bm, k),
        compiler_params=_params(vmem, 2),
        name="q_proj",
    )(x, ssq, w, seg2, q_gain)


def _kv_proj_kernel(x_ref, ssq_ref, w_ref, seg2_ref, kg_ref, kf_ref, vf_ref, kz_ref, vz_ref):
    x = x_ref[...]
    y = jnp.dot(x, w_ref[...], preferred_element_type=F32)
    y = y * _row_factor(ssq_ref, x.shape[1], y.shape[1])
    kn = _head_rms(y[:, :KV_WIDTH], seg2_ref, kg_ref[...])
    v = y[:, KV_WIDTH:]
    kf_ref[...] = kn
    vf_ref[...] = v
    low = lax.broadcasted_iota(jnp.int32, (y.shape[0], LANES), 1) < HEAD_DIM
    for src, dst in ((kn, kz_ref), (v, vz_ref)):
        for p in range(N_KV_HEADS // HEADS_PER_VREG):
            both = src[:, p * LANES:(p + 1) * LANES]
            swapped = pltpu.roll(both, HEAD_DIM, axis=1)
            dst[2 * p, 0] = jnp.where(low, both, 0.0).astype(BF16)
            dst[2 * p, 1] = jnp.where(low, 0.0, swapped).astype(BF16)
            dst[2 * p + 1, 0] = jnp.where(low, swapped, 0.0).astype(BF16)
            dst[2 * p + 1, 1] = jnp.where(low, 0.0, both).astype(BF16)


def _kv_proj(x, ssq, w, seg2, k_gain, *, bm):
    m, k = x.shape
    n = w.shape[1]
    vmem = (2 * bm * k * 2 + 2 * k * n * 2 + 8 * bm * n * 4
            + 2 * 2 * N_KV_HEADS * 2 * bm * LANES * 2)
    z_spec = pl.BlockSpec((N_KV_HEADS, 2, bm, LANES), lambda i: (0, 0, i, 0))
    row_spec = pl.BlockSpec((bm, KV_WIDTH), lambda i: (i, 0))
    z_shape = jax.ShapeDtypeStruct((N_KV_HEADS, 2, m, LANES), BF16)
    return pl.pallas_call(
        _kv_proj_kernel,
        grid=(m // bm,),
        in_specs=[
            pl.BlockSpec((bm, k), lambda i: (i, 0)),
            pl.BlockSpec((bm, LANES), lambda i: (i, 0)),
            pl.BlockSpec((k, n), lambda i: (0, 0)),
            pl.BlockSpec(seg2.shape, lambda i: (0, 0)),
            pl.BlockSpec((1, KV_WIDTH), lambda i: (0, 0)),
        ],
        out_specs=[row_spec, row_spec, z_spec, z_spec],
        out_shape=[
            jax.ShapeDtypeStruct((m, KV_WIDTH), F32),
            jax.ShapeDtypeStruct((m, KV_WIDTH), F32),
            z_shape, z_shape,
        ],
        compiler_params=_params(vmem, 1),
        name="kv_proj",
    )(x, ssq, w, seg2, k_gain)


def _attend(q_ref, o_ref, sink_ref, key_parts, val_parts, *, row0, tq, mask_lanes=None):
    rows = PAIRS * tq
    lane_pair = lax.broadcasted_iota(jnp.int32, (1, rows), 1) >> (tq.bit_length() - 1)
    for kh in range(N_KV_HEADS):
        base = kh * GQA * HEAD_DIM
        q4 = jnp.concatenate(
            [q_ref[row0:row0 + tq, base + p * LANES:base + (p + 1) * LANES]
             for p in range(PAIRS)], axis=0)
        kk = jnp.concatenate(key_parts(kh, 0) + key_parts(kh, 1), axis=0)
        vv = jnp.concatenate(val_parts(kh, 0) + val_parts(kh, 1), axis=0)
        n_keys = kk.shape[0] // HEADS_PER_VREG
        if mask_lanes is not None:
            kk = jnp.concatenate([kk, mask_lanes[0]], axis=1)
            q4 = jnp.concatenate([q4, mask_lanes[1]], axis=1)
        s_t = lax.dot_general(kk, q4, (((1,), (1,)), ((), ())),
                              preferred_element_type=F32)
        exps, denoms = [], []
        for parity in range(HEADS_PER_VREG):
            head = kh * GQA + parity
            sink = jnp.full((1, rows), sink_ref[head + HEADS_PER_VREG * (PAIRS - 1)], F32)
            for p in range(PAIRS - 1):
                sink = jnp.where(lane_pair == p, sink_ref[head + HEADS_PER_VREG * p], sink)
            sink = sink * LOG2_E
            half = s_t[parity * n_keys:(parity + 1) * n_keys]
            mx = jnp.maximum(jnp.max(half, axis=0, keepdims=True), sink)
            e = jnp.exp2(half - mx)
            denom = jnp.sum(e, axis=0, keepdims=True) + jnp.exp2(sink - mx)
            denoms.append(jnp.broadcast_to(denom, (HEAD_DIM, rows)))
            exps.append(e.astype(BF16))
        o_t = lax.dot_general(vv, jnp.concatenate(exps, axis=0),
                              (((0,), (0,)), ((), ())), preferred_element_type=F32)
        o = (o_t / jnp.concatenate(denoms, axis=0)).T
        for p in range(PAIRS):
            o_ref[row0:row0 + tq, base + p * LANES:base + (p + 1) * LANES] = (
                o[p * tq:(p + 1) * tq].astype(BF16))


def _attend_prompt_block(q_ref, kp_ref, kc_ref, vp_ref, vc_ref, sink_ref, o_ref, first):
    tq = ATT_ROWS
    n_keys = 2 * tq
    chunk_shift = CHUNK.bit_length() - 1
    chunks_per_block = tq // CHUNK
    lane = lax.broadcasted_iota(jnp.int32, (2 * n_keys, LANES), 1)
    key = lax.broadcasted_iota(jnp.int32, (2 * n_keys, LANES), 0)
    k_chunk = ((key & (n_keys - 1)) >> chunk_shift) - chunks_per_block
    oldest = jnp.where(first, 0, -chunks_per_block)
    visible = (k_chunk <= lane) & (k_chunk >= jnp.maximum(lane - 2, oldest))
    key_side = jnp.where((lane < chunks_per_block) & jnp.logical_not(visible), NEG, 0.0)
    query = lax.broadcasted_iota(jnp.int32, (PAIRS * tq, LANES), 0)
    q_lane = lax.broadcasted_iota(jnp.int32, (PAIRS * tq, LANES), 1)
    query_side = jnp.where(((query & (tq - 1)) >> chunk_shift) == q_lane, 1.0, 0.0)
    parts = lambda prev, cur: (lambda kh, parity: [prev[kh, parity], cur[kh, parity]])
    _attend(q_ref, o_ref, sink_ref, parts(kp_ref, kc_ref), parts(vp_ref, vc_ref),
            row0=0, tq=tq, mask_lanes=(key_side.astype(BF16), query_side.astype(BF16)))


def _attend_sample_block(q_ref, ck_ref, cv_ref, kc_ref, vc_ref, sink_ref, o_ref, *, dec_seq):
    for st in range(ATT_ROWS // dec_seq):
        new = slice(st * dec_seq, (st + 1) * dec_seq)
        parts = lambda cache, cur, st=st, new=new: (
            lambda kh, parity: [cache[st, kh, parity], cur[kh, parity, new, :]])
        _attend(q_ref, o_ref, sink_ref, parts(ck_ref, kc_ref), parts(cv_ref, vc_ref),
                row0=st * dec_seq, tq=dec_seq)


def _attn_kernel(q_ref, kp_ref, kc_ref, vp_ref, vc_ref, ck_ref, cv_ref, sink_ref, o_ref, *,
                 prompt_blocks, blocks_per_seq, dec_seq):
    step = pl.program_id(0)
    is_prompt = step < prompt_blocks
    first = (step & (blocks_per_seq - 1)) == 0
    pl.when(is_prompt)(lambda: _attend_prompt_block(
        q_ref, kp_ref, kc_ref, vp_ref, vc_ref, sink_ref, o_ref, first))
    pl.when(jnp.logical_not(is_prompt))(lambda: _attend_sample_block(
        q_ref, ck_ref, cv_ref, kc_ref, vc_ref, sink_ref, o_ref, dec_seq=dec_seq))


def _attn(q, kz, vz, cache_kz, cache_vz, sinks, *, m_prompt, seq, dec_seq):
    m = q.shape[0]
    tq = ATT_ROWS
    blocks_per_seq = seq // tq
    prompt_blocks = m_prompt // tq
    streams_per_block = tq // dec_seq
    assert blocks_per_seq & (blocks_per_seq - 1) == 0 and (m - m_prompt) % tq == 0
    rows_here = lambda s: (s, 0)
    cur = lambda s: (0, 0, s, 0)
    prev = lambda s: (0, 0, jnp.where((s & (blocks_per_seq - 1)) == 0, s, s - 1), 0)
    cache = lambda s: (jnp.maximum(s - prompt_blocks, 0), 0, 0, 0, 0)
    z_block = (N_KV_HEADS, 2, tq, LANES)
    cache_block = (streams_per_block, N_KV_HEADS, 2, WINDOW, LANES)
    vmem = 4 * tq * D_MODEL * 2 + 8 * N_KV_HEADS * 2 * tq * LANES * 2 \
        + 4 * streams_per_block * N_KV_HEADS * 2 * WINDOW * LANES * 2 \
        + 10 * PAIRS * tq * 4 * tq * 4
    return pl.pallas_call(
        functools.partial(_attn_kernel, prompt_blocks=prompt_blocks,
                          blocks_per_seq=blocks_per_seq, dec_seq=dec_seq),
        grid=(m // tq,),
        in_specs=[
            pl.BlockSpec((tq, D_MODEL), rows_here),
            pl.BlockSpec(z_block, prev),
            pl.BlockSpec(z_block, cur),
            pl.BlockSpec(z_block, prev),
            pl.BlockSpec(z_block, cur),
            pl.BlockSpec(cache_block, cache),
            pl.BlockSpec(cache_block, cache),
            pl.BlockSpec(memory_space=pltpu.SMEM),
        ],
        out_specs=pl.BlockSpec((tq, D_MODEL), rows_here),
        out_shape=jax.ShapeDtypeStruct((m, D_MODEL), BF16),
        compiler_params=_params(vmem, 1),
        name="attn",
    )(q, kz, kz, vz, vz, cache_kz, cache_vz, sinks)


def _lane_pair_copies(cache):
    c = cache.transpose(0, 2, 1, 3).astype(BF16)
    z = jnp.zeros_like(c)
    return jnp.stack([jnp.concatenate([c, z], axis=-1), jnp.concatenate([z, c], axis=-1)], axis=2)


def kernel(x_prompt, x_sample, cache_k, cache_v, norm_a, w_sgu_in, sgu_ln_g, sgu_ln_b, w_sgu_s,
           b_sgu_s, w_sgu_out, norm_kv, w_kv, k_norm, norm_b, w_q, q_norm, sinks, w_o,
           norm_ffn, w_ffn_gate, w_ffn_up, w_ffn_down):
    batch, seq, d = x_prompt.shape
    streams, dec_seq, _ = x_sample.shape
    m_prompt = batch * seq
    m_sample = streams * dec_seq
    m = m_prompt + m_sample
    assert d == D_MODEL and seq % ATT_ROWS == 0
    assert m % WIDE_TILE == 0 and m % KV_TILE == 0
    assert norm_a.shape[0] == 1 and norm_b.shape[0] == 1 and norm_ffn.shape[0] == 2

    row = lambda g: g.reshape(1, -1).astype(F32)
    tiles = dict(bm=WIDE_TILE, bn=MXU_WIDTH)

    def ffn(h, scaled, ssq, layer, **last_part):
        hidden = _ffn_up(scaled, ssq, w_ffn_gate, w_ffn_up, layer, **tiles)
        for p in range(FFN_DOWN_PARTS):
            final = p == FFN_DOWN_PARTS - 1
            outs = _residual(hidden, w_ffn_down, layer, h, part=(p, FFN_DOWN_PARTS),
                             name="ffn_down_%d_%d" % (layer, p), **tiles,
                             **(last_part if final else {}))
            h = outs[0]
        return outs

    x_rows = (x_prompt.reshape(m_prompt, d), x_sample.reshape(m_sample, d))
    xn = _rms_stack(*x_rows, row(norm_a[0]))
    uv = _sgu_in(xn, w_sgu_in, 0, bm=WIDE_TILE, bn=2 * MXU_WIDTH)

    pos_chunk = jnp.arange(SGU_CHUNK) // CHUNK
    ws = jnp.where((pos_chunk[:, None] >= pos_chunk[None, :])[None], w_sgu_s[0], 0.0).astype(BF16)
    bs_t = b_sgu_s[0].T.astype(F32)
    ln_g, ln_b = row(sgu_ln_g[0]), row(sgu_ln_b[0])
    a, v_rows = _sgu_mix(uv, ln_g, ln_b, ws, bs_t, m_prompt=m_prompt, sample_chunk=dec_seq)
    h, scaled, ssq = _residual(a, w_sgu_out, 0, x_rows, name="sgu_out", **tiles,
                               gains=[row(norm_ffn[0])])
    h, scaled_kv, scaled_q, ssq = ffn(h, scaled, ssq, 0, gains=[row(norm_kv), row(norm_b[0])])

    lane_head = jnp.arange(MXU_WIDTH) // HEAD_DIM
    seg = (lane_head[:, None] == lane_head[None, :]).astype(BF16)
    seg2 = jnp.concatenate([seg, seg], axis=0)
    k_rows, v_rows_kv, kz, vz = _kv_proj(
        scaled_kv, ssq, w_kv.astype(BF16), seg2, row(jnp.tile(k_norm, N_KV_HEADS)), bm=KV_TILE)
    q_gain = row(jnp.tile(q_norm[0], MXU_WIDTH // HEAD_DIM)) * SCORE_SCALE
    q = _q_proj(scaled_q, ssq, w_q, 0, seg2, q_gain, **tiles)
    sink = sinks[0].astype(F32)
    o = _attn(q, kz, vz, _lane_pair_copies(cache_k), _lane_pair_copies(cache_v), sink,
              m_prompt=m_prompt, seq=seq, dec_seq=dec_seq)
    h, scaled, ssq = _residual(o, w_o, 0, h, name="attn_out", **tiles,
                               gains=[row(norm_ffn[1])])
    y_prompt, y_sample = ffn(h, scaled, ssq, 1, split_rows=m_prompt)

    heads = lambda t: t.reshape(t.shape[:-1] + (N_KV_HEADS, HEAD_DIM))
    tail = lambda t: heads(t[:m_prompt].reshape(batch, seq, KV_WIDTH)[:, seq - WINDOW:])
    fresh = lambda t: heads(t[m_prompt:].reshape(streams, dec_seq, KV_WIDTH))
    k_p, v_p = tail(k_rows), tail(v_rows_kv)
    k_s, v_s = fresh(k_rows), fresh(v_rows_kv)
    return (y_prompt.reshape(batch, seq, d), y_sample.reshape(streams, dec_seq, d),
            k_p, v_p, k_s, v_s, v_rows.reshape(1, streams, dec_seq, d))
```

```python
import functools

import jax
import jax.numpy as jnp
from jax import lax
from jax.experimental import pallas as pl
from jax.experimental.pallas import tpu as pltpu

F32 = jnp.float32
BF16 = jnp.bfloat16

D_MODEL = 4096
SGU_CHUNK = 128
SGU_GROUPS = 8
SGU_GROUP_WIDTH = D_MODEL // SGU_GROUPS
CHUNK = 64
HEAD_DIM = 64
N_HEADS = 64
N_KV_HEADS = 8
GQA = N_HEADS // N_KV_HEADS
KV_WIDTH = N_KV_HEADS * HEAD_DIM
WINDOW = 128
NEG = -1e30
RMS_EPS = 1e-6
LN_EPS = 1e-5

V7X_VMEM_BYTES = 64 * 1024 * 1024
VMEM_LIMIT_CAP = V7X_VMEM_BYTES - 6 * 1024 * 1024
LANES = 128
MXU_WIDTH = 256
HEADS_PER_VREG = LANES // HEAD_DIM
PAIRS = GQA // HEADS_PER_VREG

WIDE_TILE = 1408
KV_TILE = WIDE_TILE // 2
NORM_ROWS = 256
ATT_ROWS = 128
RMS_STAT_ROWS = 64
RMS_SCALE_ROWS = 32
FFN_DOWN_PARTS = 2
ROW_GROUPS = 4


def _params(vmem_bytes, n_axes):
    limit = min(int(vmem_bytes) + 8 * 1024 * 1024, VMEM_LIMIT_CAP)
    return pltpu.CompilerParams(
        dimension_semantics=("arbitrary",) * n_axes, vmem_limit_bytes=limit)


def _row_groups(rows):
    size = rows // ROW_GROUPS
    assert size * ROW_GROUPS == rows and size % 16 == 0
    return [slice(g * size, (g + 1) * size) for g in range(ROW_GROUPS)]


_LHS_SPEC = pl.BlockSpec(memory_space=pl.ANY)


def _lhs_scratch(bm, kc):
    return [pltpu.VMEM((2, bm, kc), BF16), pltpu.SemaphoreType.DMA((2,))]


def _lhs_tile(x_hbm, x_buf, sem, col_block=0):
    _, bm, kc = x_buf.shape
    i, n_tiles = pl.program_id(0), pl.num_programs(0)
    slot = lax.rem(i, 2)

    def copy(tile, slot):
        src = x_hbm.at[pl.ds(tile * bm, bm), pl.ds(col_block * kc, kc)]
        return pltpu.make_async_copy(src, x_buf.at[slot], sem.at[slot])

    @pl.when(pl.program_id(1) == 0)
    def _():
        @pl.when(i == 0)
        def _():
            copy(0, 0).start()

        copy(i, slot).wait()

        @pl.when(i + 1 < n_tiles)
        def _():
            copy(i + 1, 1 - slot).start()

    return x_buf.at[slot]


def _layer_spec(k, bn, layer, col_of, row_block=0):
    return pl.BlockSpec((None, k, bn), lambda *idx: (layer, row_block, col_of(*idx)))


def _rms_rows(x_ref, g_ref, xn_ref, r_ref):
    rows, k = x_ref.shape

    def stats(i, carry):
        sl = pl.ds(pl.multiple_of(i * RMS_STAT_ROWS, RMS_STAT_ROWS), RMS_STAT_ROWS)
        x = x_ref[sl, :]
        ms = jnp.mean(x * x, axis=-1, keepdims=True)
        r_ref[sl, :] = jnp.broadcast_to(lax.rsqrt(ms + RMS_EPS), (RMS_STAT_ROWS, LANES))
        return carry

    lax.fori_loop(0, rows // RMS_STAT_ROWS, stats, 0)
    g = g_ref[...]

    def scale(i, carry):
        sl = pl.ds(pl.multiple_of(i * RMS_SCALE_ROWS, RMS_SCALE_ROWS), RMS_SCALE_ROWS)
        r = jnp.tile(r_ref[sl, :], (1, k // LANES))
        xn_ref[sl, :] = (x_ref[sl, :] * r * g).astype(BF16)
        return carry

    lax.fori_loop(0, rows // RMS_SCALE_ROWS, scale, 0)


def _rms_stack_kernel(xp_ref, xs_ref, g_ref, xn_ref, r_ref, *, prompt_blocks):
    from_ref = lambda x_ref: _rms_rows(x_ref, g_ref, xn_ref, r_ref)
    pl.when(pl.program_id(0) < prompt_blocks)(lambda: from_ref(xp_ref))
    pl.when(pl.program_id(0) >= prompt_blocks)(lambda: from_ref(xs_ref))


def _rms_stack(x_prompt, x_sample, gain):
    mp, k = x_prompt.shape
    ms = x_sample.shape[0]
    assert mp % NORM_ROWS == 0 and ms == NORM_ROWS
    prompt_blocks = mp // NORM_ROWS
    m = mp + ms
    vmem = 4 * NORM_ROWS * k * 4 + 2 * NORM_ROWS * k * 2 + 4 * RMS_STAT_ROWS * k * 4
    return pl.pallas_call(
        functools.partial(_rms_stack_kernel, prompt_blocks=prompt_blocks),
        grid=(m // NORM_ROWS,),
        in_specs=[
            pl.BlockSpec((NORM_ROWS, k), lambda i: (jnp.minimum(i, prompt_blocks - 1), 0)),
            pl.BlockSpec((NORM_ROWS, k), lambda i: (0, 0)),
            pl.BlockSpec((1, k), lambda i: (0, 0)),
        ],
        out_specs=pl.BlockSpec((NORM_ROWS, k), lambda i: (i, 0)),
        out_shape=jax.ShapeDtypeStruct((m, k), BF16),
        scratch_shapes=[pltpu.VMEM((NORM_ROWS, LANES), F32)],
        compiler_params=_params(vmem, 1),
        name="rms_stack",
    )(x_prompt, x_sample, gain)


def _head_rms(y, seg2_ref, gain_row):
    y2 = y * y
    hi = y2.astype(BF16)
    lo = (y2 - hi.astype(F32)).astype(BF16)
    seg2 = seg2_ref[...]
    w = seg2.shape[1]
    ssq = jnp.concatenate(
        [jnp.dot(jnp.concatenate([hi[:, c:c + w], lo[:, c:c + w]], axis=1), seg2,
                 preferred_element_type=F32)
         for c in range(0, y.shape[1], w)], axis=-1)
    return y * lax.rsqrt(ssq * (1.0 / HEAD_DIM) + RMS_EPS) * gain_row


def _row_factor(ssq_ref, k, n):
    r = lax.rsqrt(ssq_ref[...] * (1.0 / k) + RMS_EPS)
    return jnp.tile(r, (1, n // LANES))


def _sgu_in_kernel(x_hbm, w_ref, o_ref, x_buf, sem):
    x_ref = _lhs_tile(x_hbm, x_buf, sem)
    w = w_ref[...].astype(BF16)
    for rows in _row_groups(x_ref.shape[0]):
        y = jnp.dot(x_ref[rows, :], w, preferred_element_type=F32)
        o_ref[rows, :] = 0.5 * y * (1.0 + lax.erf(y * (0.5 ** 0.5)))


def _sgu_in(xn, w, layer, *, bm, bn):
    m, k = xn.shape
    n = w.shape[2]
    wb = w.dtype.itemsize
    vmem = 2 * bm * k * 2 + k * bn * (2 * wb + 2) + 5 * bm * bn * 4
    return pl.pallas_call(
        _sgu_in_kernel,
        grid=(m // bm, n // bn),
        in_specs=[_LHS_SPEC, _layer_spec(k, bn, layer, lambda i, j: j)],
        out_specs=pl.BlockSpec((bm, bn), lambda i, j: (i, j)),
        out_shape=jax.ShapeDtypeStruct((m, n), F32),
        scratch_shapes=_lhs_scratch(bm, k),
        compiler_params=_params(vmem, 2),
        name="sgu_in",
    )(xn, w)


def _sgu_mix_rows(u_ref, v_ref, lng_ref, lnb_ref, ws_ref, bs_ref, a_ref, vn_ref, *, chunk):
    ln_g = lng_ref[...]
    ln_b = lnb_ref[...]
    for c in range(u_ref.shape[0] // chunk):
        rows = slice(c * chunk, (c + 1) * chunk)
        v = v_ref[rows, :]
        mu = jnp.mean(v, axis=-1, keepdims=True)
        vc = v - mu
        var = jnp.mean(vc * vc, axis=-1, keepdims=True)
        vn = vc * lax.rsqrt(var + LN_EPS) * ln_g + ln_b
        if vn_ref is not None:
            vn_ref[rows, :] = vn
        vnb = vn.astype(BF16)
        for g in range(SGU_GROUPS):
            cols = slice(g * SGU_GROUP_WIDTH, (g + 1) * SGU_GROUP_WIDTH)
            mixed = jnp.dot(ws_ref[g], vnb[:, cols], preferred_element_type=F32)
            mixed = mixed + bs_ref[:, g:g + 1]
            a_ref[rows, cols] = (u_ref[rows, cols] * mixed).astype(BF16)


def _sgu_mix_kernel(u_ref, v_ref, lng_ref, lnb_ref, ws_ref, bs_ref, wss_ref, bss_ref,
                    a_ref, vn_ref, *, prompt_blocks, sample_chunk):
    common = (u_ref, v_ref, lng_ref, lnb_ref)
    is_prompt = pl.program_id(0) < prompt_blocks
    pl.when(is_prompt)(
        lambda: _sgu_mix_rows(*common, ws_ref, bs_ref, a_ref, None, chunk=SGU_CHUNK))
    pl.when(jnp.logical_not(is_prompt))(
        lambda: _sgu_mix_rows(*common, wss_ref, bss_ref, a_ref, vn_ref, chunk=sample_chunk))


def _sgu_mix(uv, ln_g, ln_b, ws, bs_t, *, m_prompt, sample_chunk):
    m = uv.shape[0]
    d = D_MODEL
    rows = m - m_prompt
    assert m_prompt % rows == 0 and rows % SGU_CHUNK == 0 and rows % sample_chunk == 0
    prompt_blocks = m_prompt // rows
    const = lambda shape: pl.BlockSpec(shape, lambda i: (0,) * len(shape))
    wss, bss = ws[:, :sample_chunk, :sample_chunk], bs_t[:sample_chunk]
    vmem = 2 * 2 * rows * d * 4 + 2 * rows * d * 2 + 2 * rows * d * 4 + 6 * SGU_CHUNK * d * 4
    return pl.pallas_call(
        functools.partial(_sgu_mix_kernel, prompt_blocks=prompt_blocks,
                          sample_chunk=sample_chunk),
        grid=(prompt_blocks + 1,),
        in_specs=[
            pl.BlockSpec((rows, d), lambda i: (i, 0)),
            pl.BlockSpec((rows, d), lambda i: (i, 1)),
            const((1, d)), const((1, d)),
            const(ws.shape), const(bs_t.shape), const(wss.shape), const(bss.shape),
        ],
        out_specs=[pl.BlockSpec((rows, d), lambda i: (i, 0)), const((rows, d))],
        out_shape=[jax.ShapeDtypeStruct((m, d), BF16), jax.ShapeDtypeStruct((rows, d), F32)],
        compiler_params=_params(vmem, 1),
        name="sgu_mix",
    )(uv, uv, ln_g, ln_b, ws, bs_t, wss, bss)


def _residual_kernel(x_hbm, w_ref, *refs, n_res, n_gains, top_rows, tail_rows, col_block):
    x_ref = _lhs_tile(x_hbm, *refs[-2:], col_block=col_block)
    refs = refs[:-2]
    res_refs, refs = refs[:n_res], refs[n_res:]
    gain_refs, refs = refs[:n_gains], refs[n_gains:]
    n_main = 2 if tail_rows else 1
    main_refs, refs = refs[:n_main], refs[n_main:]
    scaled_refs, ssq_refs = refs[:n_gains], refs[n_gains:]
    o_ref = main_refs[0]
    w = w_ref[...].astype(BF16)
    bm = x_ref.shape[0]
    squares = []
    plain = n_res == 1 and n_gains == 0
    for rows in (_row_groups(bm) if plain else [slice(0, bm)]):
        y = jnp.dot(x_ref[rows, :], w, preferred_element_type=F32)
        if n_res == 1:
            h = res_refs[0][rows, :] + y
            o_ref[rows, :] = h
        else:
            top_ref, tail_ref = res_refs
            is_last = pl.program_id(0) == pl.num_programs(0) - 1
            straddle = jnp.where(is_last, tail_ref[...], top_ref[top_rows:, :])
            h = jnp.concatenate([top_ref[:top_rows, :], straddle], axis=0) + y
            o_ref[...] = h
        for g_ref, s_ref in zip(gain_refs, scaled_refs):
            s_ref[rows, :] = (h * g_ref[...]).astype(BF16)
        if n_gains:
            squares.append(jnp.sum(h * h, axis=-1, keepdims=True))
    if tail_rows:
        main_refs[1][...] = o_ref[bm - tail_rows:, :]
    if n_gains:
        (ssq_ref,) = ssq_refs
        part = jnp.broadcast_to(jnp.concatenate(squares, axis=0), ssq_ref.shape)
        first = pl.program_id(1) == 0

        @pl.when(first)
        def _():
            ssq_ref[...] = part

        @pl.when(jnp.logical_not(first))
        def _():
            ssq_ref[...] += part


def _residual(x, w, layer, res, *, bm, bn, name, part=(0, 1), gains=(), split_rows=None):
    m, k = x.shape
    n = w.shape[2]
    p, n_parts = part
    kc = k // n_parts
    assert kc * n_parts == k and kc % LANES == 0
    wb = w.dtype.itemsize
    tile = pl.BlockSpec((bm, bn), lambda i, j: (i, j))
    last = m // bm - 1

    top_rows = 0
    if isinstance(res, tuple):
        top, tail = res
        top_rows = bm - tail.shape[0]
        assert top.shape[0] + tail.shape[0] == m and top_rows > 0 and top_rows % 8 == 0
        res_specs = [tile, pl.BlockSpec((tail.shape[0], bn), lambda i, j: (0, j))]
        res_args = [top, tail]
    else:
        res_specs, res_args = [tile], [res]

    tail_rows = 0
    if split_rows is None:
        out_specs, out_shape = [tile], [jax.ShapeDtypeStruct((m, n), F32)]
    else:
        tail_rows = m - split_rows
        assert last * bm < split_rows and tail_rows <= bm
        out_specs = [tile, pl.BlockSpec((tail_rows, bn),
                                        lambda i, j: (0, jnp.where(i == last, j, 0)))]
        out_shape = [jax.ShapeDtypeStruct((split_rows, n), F32),
                     jax.ShapeDtypeStruct((tail_rows, n), F32)]
    out_specs += [tile] * len(gains)
    out_shape += [jax.ShapeDtypeStruct((m, n), BF16)] * len(gains)
    if gains:
        out_specs.append(pl.BlockSpec((bm, LANES), lambda i, j: (i, 0)))
        out_shape.append(jax.ShapeDtypeStruct((m, LANES), F32))

    vmem = 2 * bm * kc * 2 + kc * bn * (2 * wb + 2) + (8 + 2 * len(gains)) * bm * bn * 4
    return pl.pallas_call(
        functools.partial(_residual_kernel, n_res=len(res_args), n_gains=len(gains),
                          top_rows=top_rows, tail_rows=tail_rows, col_block=p),
        grid=(m // bm, n // bn),
        in_specs=[
            _LHS_SPEC,
            _layer_spec(kc, bn, layer, lambda i, j: j, row_block=p),
        ] + res_specs + [pl.BlockSpec((1, bn), lambda i, j: (0, j))] * len(gains),
        out_specs=out_specs,
        out_shape=out_shape,
        scratch_shapes=_lhs_scratch(bm, kc),
        compiler_params=_params(vmem, 2),
        name=name,
    )(x, w, *res_args, *gains)


def _ffn_up_kernel(x_hbm, ssq_ref, wg_ref, wu_ref, o_ref, x_buf, sem):
    x_ref = _lhs_tile(x_hbm, x_buf, sem)
    wg = wg_ref[...].astype(BF16)
    wu = wu_ref[...].astype(BF16)
    k = x_ref.shape[1]
    for rows in _row_groups(x_ref.shape[0]):
        x = x_ref[rows, :]
        r = _row_factor(ssq_ref.at[rows, :], k, o_ref.shape[1])
        gate = jnp.dot(x, wg, preferred_element_type=F32) * r
        up = jnp.dot(x, wu, preferred_element_type=F32) * r
        o_ref[rows, :] = (0.5 * gate * (1.0 + jnp.tanh(0.5 * gate)) * up).astype(BF16)


def _ffn_up(x, ssq, w_gate, w_up, layer, *, bm, bn):
    m, k = x.shape
    n = w_gate.shape[2]
    wb = w_gate.dtype.itemsize
    vmem = 2 * bm * k * 2 + 2 * k * bn * (2 * wb + 2) + 2 * bm * bn * 2 + 5 * bm * bn * 4
    return pl.pallas_call(
        _ffn_up_kernel,
        grid=(m // bm, n // bn),
        in_specs=[
            _LHS_SPEC,
            pl.BlockSpec((bm, LANES), lambda i, j: (i, 0)),
            _layer_spec(k, bn, layer, lambda i, j: j),
            _layer_spec(k, bn, layer, lambda i, j: j),
        ],
        scratch_shapes=_lhs_scratch(bm, k),
        out_specs=pl.BlockSpec((bm, bn), lambda i, j: (i, j)),
        out_shape=jax.ShapeDtypeStruct((m, n), BF16),
        compiler_params=_params(vmem, 2),
        name="ffn_up",
    )(x, ssq, w_gate, w_up)


LOG2_E = 1.4426950408889634
SCORE_SCALE = HEAD_DIM ** -0.5 * LOG2_E


def _q_proj_kernel(x_hbm, ssq_ref, w_ref, seg2_ref, qg_ref, q_ref, x_buf, sem):
    x = _lhs_tile(x_hbm, x_buf, sem)[...]
    y = jnp.dot(x, w_ref[...].astype(BF16), preferred_element_type=F32)
    y = y * _row_factor(ssq_ref, x.shape[1], y.shape[1])
    q_ref[...] = _head_rms(y, seg2_ref, qg_ref[...]).astype(BF16)


def _q_proj(x, ssq, w, layer, seg2, q_gain, *, bm, bn):
    m, k = x.shape
    n = w.shape[2]
    wb = w.dtype.itemsize
    vmem = 2 * bm * k * 2 + k * bn * (2 * wb + 2) + 10 * bm * bn * 4
    return pl.pallas_call(
        _q_proj_kernel,
        grid=(m // bm, n // bn),
        in_specs=[
            _LHS_SPEC,
            pl.BlockSpec((bm, LANES), lambda i, j: (i, 0)),
            _layer_spec(k, bn, layer, lambda i, j: j),
            pl.BlockSpec(seg2.shape, lambda i, j: (0, 0)),
            pl.BlockSpec((1, bn), lambda i, j: (0, 0)),
        ],
        out_specs=pl.BlockSpec((bm, bn), lambda i, j: (i, j)),
        out_shape=jax.ShapeDtypeStruct((m, n), BF16),
        scratch_shapes=_lhs_scratch(bm, k),
        compiler_params=_params(vmem, 2),
        name="q_proj",
    )(x, ssq, w, seg2, q_gain)


def _kv_proj_kernel(x_ref, ssq_ref, w_ref, seg2_ref, kg_ref, kf_ref, vf_ref, kz_ref, vz_ref):
    x = x_ref[...]
    y = jnp.dot(x, w_ref[...], preferred_element_type=F32)
    y = y * _row_factor(ssq_ref, x.shape[1], y.shape[1])
    kn = _head_rms(y[:, :KV_WIDTH], seg2_ref, kg_ref[...])
    v = y[:, KV_WIDTH:]
    kf_ref[...] = kn
    vf_ref[...] = v
    low = lax.broadcasted_iota(jnp.int32, (y.shape[0], LANES), 1) < HEAD_DIM
    for src, dst in ((kn, kz_ref), (v, vz_ref)):
        for p in range(N_KV_HEADS // HEADS_PER_VREG):
            both = src[:, p * LANES:(p + 1) * LANES]
            swapped = pltpu.roll(both, HEAD_DIM, axis=1)
            dst[2 * p, 0] = jnp.where(low, both, 0.0).astype(BF16)
            dst[2 * p, 1] = jnp.where(low, 0.0, swapped).astype(BF16)
            dst[2 * p + 1, 0] = jnp.where(low, swapped, 0.0).astype(BF16)
            dst[2 * p + 1, 1] = jnp.where(low, 0.0, both).astype(BF16)


def _kv_proj(x, ssq, w, seg2, k_gain, *, bm):
    m, k = x.shape
    n = w.shape[1]
    vmem = (2 * bm * k * 2 + 2 * k * n * 2 + 8 * bm * n * 4
            + 2 * 2 * N_KV_HEADS * 2 * bm * LANES * 2)
    z_spec = pl.BlockSpec((N_KV_HEADS, 2, bm, LANES), lambda i: (0, 0, i, 0))
    row_spec = pl.BlockSpec((bm, KV_WIDTH), lambda i: (i, 0))
    z_shape = jax.ShapeDtypeStruct((N_KV_HEADS, 2, m, LANES), BF16)
    return pl.pallas_call(
        _kv_proj_kernel,
        grid=(m // bm,),
        in_specs=[
            pl.BlockSpec((bm, k), lambda i: (i, 0)),
            pl.BlockSpec((bm, LANES), lambda i: (i, 0)),
            pl.BlockSpec((k, n), lambda i: (0, 0)),
            pl.BlockSpec(seg2.shape, lambda i: (0, 0)),
            pl.BlockSpec((1, KV_WIDTH), lambda i: (0, 0)),
        ],
        out_specs=[row_spec, row_spec, z_spec, z_spec],
        out_shape=[
            jax.ShapeDtypeStruct((m, KV_WIDTH), F32),
            jax.ShapeDtypeStruct((m, KV_WIDTH), F32),
            z_shape, z_shape,
        ],
        compiler_params=_params(vmem, 1),
        name="kv_proj",
    )(x, ssq, w, seg2, k_gain)


def _attend(q_ref, o_ref, sink_ref, key_parts, val_parts, *, row0, tq, mask_lanes=None,
            score_bound=None):
    rows = PAIRS * tq
    lane_pair = lax.broadcasted_iota(jnp.int32, (1, rows), 1) >> (tq.bit_length() - 1)
    for kh in range(N_KV_HEADS):
        base = kh * GQA * HEAD_DIM
        q4 = jnp.concatenate(
            [q_ref[row0:row0 + tq, base + p * LANES:base + (p + 1) * LANES]
             for p in range(PAIRS)], axis=0)
        kk = jnp.concatenate(key_parts(kh, 0) + key_parts(kh, 1), axis=0)
        vv = jnp.concatenate(val_parts(kh, 0) + val_parts(kh, 1), axis=0)
        n_keys = kk.shape[0] // HEADS_PER_VREG
        if mask_lanes is not None:
            kk = jnp.concatenate([kk, mask_lanes[0]], axis=1)
            q4 = jnp.concatenate([q4, mask_lanes[1]], axis=1)
        s_t = lax.dot_general(kk, q4, (((1,), (1,)), ((), ())),
                              preferred_element_type=F32)
        exps, denoms = [], []
        for parity in range(HEADS_PER_VREG):
            head = kh * GQA + parity
            sink = jnp.full((1, rows), sink_ref[head + HEADS_PER_VREG * (PAIRS - 1)], F32)
            for p in range(PAIRS - 1):
                sink = jnp.where(lane_pair == p, sink_ref[head + HEADS_PER_VREG * p], sink)
            sink = sink * LOG2_E
            half = s_t[parity * n_keys:(parity + 1) * n_keys]
            if score_bound is None:
                mx = jnp.maximum(jnp.max(half, axis=0, keepdims=True), sink)
            else:
                mx = jnp.maximum(score_bound, sink)
            e = jnp.exp2(half - mx)
            denom = jnp.sum(e, axis=0, keepdims=True) + jnp.exp2(sink - mx)
            denoms.append(jnp.broadcast_to(denom, (HEAD_DIM, rows)))
            exps.append(e.astype(BF16))
        o_t = lax.dot_general(vv, jnp.concatenate(exps, axis=0),
                              (((0,), (0,)), ((), ())), preferred_element_type=F32)
        o = (o_t / jnp.concatenate(denoms, axis=0)).T
        for p in range(PAIRS):
            o_ref[row0:row0 + tq, base + p * LANES:base + (p + 1) * LANES] = (
                o[p * tq:(p + 1) * tq].astype(BF16))


def _attend_prompt_block(q_ref, kp_ref, kc_ref, vp_ref, vc_ref, sink_ref, o_ref, first,
                         score_bound=None):
    tq = ATT_ROWS
    n_keys = 2 * tq
    chunk_shift = CHUNK.bit_length() - 1
    chunks_per_block = tq // CHUNK
    lane = lax.broadcasted_iota(jnp.int32, (2 * n_keys, LANES), 1)
    key = lax.broadcasted_iota(jnp.int32, (2 * n_keys, LANES), 0)
    k_chunk = ((key & (n_keys - 1)) >> chunk_shift) - chunks_per_block
    oldest = jnp.where(first, 0, -chunks_per_block)
    visible = (k_chunk <= lane) & (k_chunk >= jnp.maximum(lane - 2, oldest))
    key_side = jnp.where((lane < chunks_per_block) & jnp.logical_not(visible), NEG, 0.0)
    query = lax.broadcasted_iota(jnp.int32, (PAIRS * tq, LANES), 0)
    q_lane = lax.broadcasted_iota(jnp.int32, (PAIRS * tq, LANES), 1)
    query_side = jnp.where(((query & (tq - 1)) >> chunk_shift) == q_lane, 1.0, 0.0)
    parts = lambda prev, cur: (lambda kh, parity: [prev[kh, parity], cur[kh, parity]])
    _attend(q_ref, o_ref, sink_ref, parts(kp_ref, kc_ref), parts(vp_ref, vc_ref),
            row0=0, tq=tq, mask_lanes=(key_side.astype(BF16), query_side.astype(BF16)),
            score_bound=score_bound)


def _attend_sample_block(q_ref, ck_ref, cv_ref, kc_ref, vc_ref, sink_ref, o_ref, *, dec_seq):
    for st in range(ATT_ROWS // dec_seq):
        new = slice(st * dec_seq, (st + 1) * dec_seq)
        parts = lambda cache, cur, st=st, new=new: (
            lambda kh, parity: [cache[st, kh, parity], cur[kh, parity, new, :]])
        _attend(q_ref, o_ref, sink_ref, parts(ck_ref, kc_ref), parts(cv_ref, vc_ref),
                row0=st * dec_seq, tq=dec_seq)


def _attn_kernel(q_ref, kp_ref, kc_ref, vp_ref, vc_ref, ck_ref, cv_ref, sink_ref, bound_ref,
                 o_ref, *, prompt_blocks, blocks_per_seq, dec_seq):
    step = pl.program_id(0)
    is_prompt = step < prompt_blocks
    first = (step & (blocks_per_seq - 1)) == 0
    prompt_refs = (q_ref, kp_ref, kc_ref, vp_ref, vc_ref, sink_ref, o_ref, first)
    bound = bound_ref[0]
    use_bound = bound < MAX_SCORE_SHIFT
    pl.when(is_prompt & use_bound)(
        lambda: _attend_prompt_block(*prompt_refs, score_bound=bound))
    pl.when(is_prompt & jnp.logical_not(use_bound))(
        lambda: _attend_prompt_block(*prompt_refs))
    pl.when(jnp.logical_not(is_prompt))(lambda: _attend_sample_block(
        q_ref, ck_ref, cv_ref, kc_ref, vc_ref, sink_ref, o_ref, dec_seq=dec_seq))


MAX_SCORE_SHIFT = 60.0


def _score_bound(q_gain, k_gain):
    margin = 1.02
    bound = margin * SCORE_SCALE * HEAD_DIM * jnp.max(jnp.abs(q_gain)) * jnp.max(jnp.abs(k_gain))
    return bound.reshape(1).astype(F32)


def _attn(q, kz, vz, cache_kz, cache_vz, sinks, score_bound, *, m_prompt, seq, dec_seq):
    m = q.shape[0]
    tq = ATT_ROWS
    blocks_per_seq = seq // tq
    prompt_blocks = m_prompt // tq
    streams_per_block = tq // dec_seq
    assert blocks_per_seq & (blocks_per_seq - 1) == 0 and (m - m_prompt) % tq == 0
    rows_here = lambda s: (s, 0)
    cur = lambda s: (0, 0, s, 0)
    prev = lambda s: (0, 0, jnp.where((s & (blocks_per_seq - 1)) == 0, s, s - 1), 0)
    cache = lambda s: (jnp.maximum(s - prompt_blocks, 0), 0, 0, 0, 0)
    z_block = (N_KV_HEADS, 2, tq, LANES)
    cache_block = (streams_per_block, N_KV_HEADS, 2, WINDOW, LANES)
    vmem = 4 * tq * D_MODEL * 2 + 8 * N_KV_HEADS * 2 * tq * LANES * 2 \
        + 4 * streams_per_block * N_KV_HEADS * 2 * WINDOW * LANES * 2 \
        + 10 * PAIRS * tq * 4 * tq * 4
    return pl.pallas_call(
        functools.partial(_attn_kernel, prompt_blocks=prompt_blocks,
                          blocks_per_seq=blocks_per_seq, dec_seq=dec_seq),
        grid=(m // tq,),
        in_specs=[
            pl.BlockSpec((tq, D_MODEL), rows_here),
            pl.BlockSpec(z_block, prev),
            pl.BlockSpec(z_block, cur),
            pl.BlockSpec(z_block, prev),
            pl.BlockSpec(z_block, cur),
            pl.BlockSpec(cache_block, cache),
            pl.BlockSpec(cache_block, cache),
            pl.BlockSpec(memory_space=pltpu.SMEM),
            pl.BlockSpec(memory_space=pltpu.SMEM),
        ],
        out_specs=pl.BlockSpec((tq, D_MODEL), rows_here),
        out_shape=jax.ShapeDtypeStruct((m, D_MODEL), BF16),
        compiler_params=_params(vmem, 1),
        name="attn",
    )(q, kz, kz, vz, vz, cache_kz, cache_vz, sinks, score_bound)


def _lane_pair_copies(cache):
    c = cache.transpose(0, 2, 1, 3).astype(BF16)
    z = jnp.zeros_like(c)
    return jnp.stack([jnp.concatenate([c, z], axis=-1), jnp.concatenate([z, c], axis=-1)], axis=2)


def kernel(x_prompt, x_sample, cache_k, cache_v, norm_a, w_sgu_in, sgu_ln_g, sgu_ln_b, w_sgu_s,
           b_sgu_s, w_sgu_out, norm_kv, w_kv, k_norm, norm_b, w_q, q_norm, sinks, w_o,
           norm_ffn, w_ffn_gate, w_ffn_up, w_ffn_down):
    batch, seq, d = x_prompt.shape
    streams, dec_seq, _ = x_sample.shape
    m_prompt = batch * seq
    m_sample = streams * dec_seq
    m = m_prompt + m_sample
    assert d == D_MODEL and seq % ATT_ROWS == 0
    assert m % WIDE_TILE == 0 and m % KV_TILE == 0
    assert norm_a.shape[0] == 1 and norm_b.shape[0] == 1 and norm_ffn.shape[0] == 2

    row = lambda g: g.reshape(1, -1).astype(F32)
    tiles = dict(bm=WIDE_TILE, bn=MXU_WIDTH)

    def ffn(h, scaled, ssq, layer, **last_part):
        hidden = _ffn_up(scaled, ssq, w_ffn_gate, w_ffn_up, layer, **tiles)
        for p in range(FFN_DOWN_PARTS):
            final = p == FFN_DOWN_PARTS - 1
            outs = _residual(hidden, w_ffn_down, layer, h, part=(p, FFN_DOWN_PARTS),
                             name="ffn_down_%d_%d" % (layer, p), **tiles,
                             **(last_part if final else {}))
            h = outs[0]
        return outs

    x_rows = (x_prompt.reshape(m_prompt, d), x_sample.reshape(m_sample, d))
    xn = _rms_stack(*x_rows, row(norm_a[0]))
    uv = _sgu_in(xn, w_sgu_in, 0, bm=WIDE_TILE, bn=2 * MXU_WIDTH)

    pos_chunk = jnp.arange(SGU_CHUNK) // CHUNK
    ws = jnp.where((pos_chunk[:, None] >= pos_chunk[None, :])[None], w_sgu_s[0], 0.0).astype(BF16)
    bs_t = b_sgu_s[0].T.astype(F32)
    ln_g, ln_b = row(sgu_ln_g[0]), row(sgu_ln_b[0])
    a, v_rows = _sgu_mix(uv, ln_g, ln_b, ws, bs_t, m_prompt=m_prompt, sample_chunk=dec_seq)
    h, scaled, ssq = _residual(a, w_sgu_out, 0, x_rows, name="sgu_out", **tiles,
                               gains=[row(norm_ffn[0])])
    h, scaled_kv, scaled_q, ssq = ffn(h, scaled, ssq, 0, gains=[row(norm_kv), row(norm_b[0])])

    lane_head = jnp.arange(MXU_WIDTH) // HEAD_DIM
    seg = (lane_head[:, None] == lane_head[None, :]).astype(BF16)
    seg2 = jnp.concatenate([seg, seg], axis=0)
    k_rows, v_rows_kv, kz, vz = _kv_proj(
        scaled_kv, ssq, w_kv.astype(BF16), seg2, row(jnp.tile(k_norm, N_KV_HEADS)), bm=KV_TILE)
    q_gain = row(jnp.tile(q_norm[0], MXU_WIDTH // HEAD_DIM)) * SCORE_SCALE
    q = _q_proj(scaled_q, ssq, w_q, 0, seg2, q_gain, **tiles)
    sink = sinks[0].astype(F32)
    o = _attn(q, kz, vz, _lane_pair_copies(cache_k), _lane_pair_copies(cache_v), sink,
              _score_bound(q_norm[0], k_norm), m_prompt=m_prompt, seq=seq, dec_seq=dec_seq)
    h, scaled, ssq = _residual(o, w_o, 0, h, name="attn_out", **tiles,
                               gains=[row(norm_ffn[1])])
    y_prompt, y_sample = ffn(h, scaled, ssq, 1, split_rows=m_prompt)

    heads = lambda t: t.reshape(t.shape[:-1] + (N_KV_HEADS, HEAD_DIM))
    tail = lambda t: heads(t[:m_prompt].reshape(batch, seq, KV_WIDTH)[:, seq - WINDOW:])
    fresh = lambda t: heads(t[m_prompt:].reshape(streams, dec_seq, KV_WIDTH))
    k_p, v_p = tail(k_rows), tail(v_rows_kv)
    k_s, v_s = fresh(k_rows), fresh(v_rows_kv)
    return (y_prompt.reshape(batch, seq, d), y_sample.reshape(streams, dec_seq, d),
            k_p, v_p, k_s, v_s, v_rows.reshape(1, streams, dec_seq, d))
```

```python
import functools

import jax
import jax.numpy as jnp
from jax import lax
from jax.experimental import pallas as pl
from jax.experimental.pallas import tpu as pltpu

F32 = jnp.float32
BF16 = jnp.bfloat16

D_MODEL = 4096
SGU_CHUNK = 128
SGU_GROUPS = 8
SGU_GROUP_WIDTH = D_MODEL // SGU_GROUPS
CHUNK = 64
HEAD_DIM = 64
N_HEADS = 64
N_KV_HEADS = 8
GQA = N_HEADS // N_KV_HEADS
KV_WIDTH = N_KV_HEADS * HEAD_DIM
WINDOW = 128
NEG = -1e30
RMS_EPS = 1e-6
LN_EPS = 1e-5

V7X_VMEM_BYTES = 64 * 1024 * 1024
VMEM_LIMIT_CAP = V7X_VMEM_BYTES - 6 * 1024 * 1024
VMEM_ESTIMATE_MARGIN = 8 * 1024 * 1024
LANES = 128
F32_SUBLANES = 8
BF16_SUBLANES = 16
MXU_WIDTH = 256
HEADS_PER_VREG = LANES // HEAD_DIM
PAIRS = GQA // HEADS_PER_VREG

WIDE_TILE = 1408
KV_TILE = WIDE_TILE // 2
NORM_ROWS = 256
ATT_ROWS = 128
ATT_STEP_ROWS = 256
RMS_STAT_ROWS = 64
RMS_SCALE_ROWS = 32
FFN_DOWN_PARTS = 2
ROW_GROUPS = 4


def _params(vmem_bytes, n_axes):
    limit = min(int(vmem_bytes) + VMEM_ESTIMATE_MARGIN, VMEM_LIMIT_CAP)
    return pltpu.CompilerParams(
        dimension_semantics=("arbitrary",) * n_axes, vmem_limit_bytes=limit)


def _row_groups(rows):
    size = rows // ROW_GROUPS
    assert size * ROW_GROUPS == rows and size % BF16_SUBLANES == 0
    return [slice(g * size, (g + 1) * size) for g in range(ROW_GROUPS)]


_LHS_SPEC = pl.BlockSpec(memory_space=pl.ANY)


def _lhs_scratch(bm, kc):
    return [pltpu.VMEM((2, bm, kc), BF16), pltpu.SemaphoreType.DMA((2,))]


def _lhs_tile(x_hbm, x_buf, sem, col_block=0):
    _, bm, kc = x_buf.shape
    i, n_tiles = pl.program_id(0), pl.num_programs(0)
    slot = lax.rem(i, 2)

    def copy(tile, slot):
        src = x_hbm.at[pl.ds(tile * bm, bm), pl.ds(col_block * kc, kc)]
        return pltpu.make_async_copy(src, x_buf.at[slot], sem.at[slot])

    @pl.when(pl.program_id(1) == 0)
    def _():
        @pl.when(i == 0)
        def _():
            copy(0, 0).start()

        copy(i, slot).wait()

        @pl.when(i + 1 < n_tiles)
        def _():
            copy(i + 1, 1 - slot).start()

    return x_buf.at[slot]


def _layer_spec(k, bn, layer, col_of, row_block=0):
    return pl.BlockSpec((None, k, bn), lambda *idx: (layer, row_block, col_of(*idx)))


def _rms_rows(x_ref, g_ref, xn_ref, r_ref):
    rows, k = x_ref.shape

    def stats(i, carry):
        sl = pl.ds(pl.multiple_of(i * RMS_STAT_ROWS, RMS_STAT_ROWS), RMS_STAT_ROWS)
        x = x_ref[sl, :]
        ms = jnp.mean(x * x, axis=-1, keepdims=True)
        r_ref[sl, :] = jnp.broadcast_to(lax.rsqrt(ms + RMS_EPS), (RMS_STAT_ROWS, LANES))
        return carry

    lax.fori_loop(0, rows // RMS_STAT_ROWS, stats, 0)
    g = g_ref[...]

    def scale(i, carry):
        sl = pl.ds(pl.multiple_of(i * RMS_SCALE_ROWS, RMS_SCALE_ROWS), RMS_SCALE_ROWS)
        r = jnp.tile(r_ref[sl, :], (1, k // LANES))
        xn_ref[sl, :] = (x_ref[sl, :] * r * g).astype(BF16)
        return carry

    lax.fori_loop(0, rows // RMS_SCALE_ROWS, scale, 0)


def _rms_stack_kernel(xp_ref, xs_ref, g_ref, xn_ref, r_ref, *, prompt_blocks):
    from_ref = lambda x_ref: _rms_rows(x_ref, g_ref, xn_ref, r_ref)
    pl.when(pl.program_id(0) < prompt_blocks)(lambda: from_ref(xp_ref))
    pl.when(pl.program_id(0) >= prompt_blocks)(lambda: from_ref(xs_ref))


def _rms_stack(x_prompt, x_sample, gain):
    mp, k = x_prompt.shape
    ms = x_sample.shape[0]
    assert mp % NORM_ROWS == 0 and ms == NORM_ROWS
    prompt_blocks = mp // NORM_ROWS
    m = mp + ms
    vmem = 4 * NORM_ROWS * k * 4 + 2 * NORM_ROWS * k * 2 + 4 * RMS_STAT_ROWS * k * 4
    return pl.pallas_call(
        functools.partial(_rms_stack_kernel, prompt_blocks=prompt_blocks),
        grid=(m // NORM_ROWS,),
        in_specs=[
            pl.BlockSpec((NORM_ROWS, k), lambda i: (jnp.minimum(i, prompt_blocks - 1), 0)),
            pl.BlockSpec((NORM_ROWS, k), lambda i: (0, 0)),
            pl.BlockSpec((1, k), lambda i: (0, 0)),
        ],
        out_specs=pl.BlockSpec((NORM_ROWS, k), lambda i: (i, 0)),
        out_shape=jax.ShapeDtypeStruct((m, k), BF16),
        scratch_shapes=[pltpu.VMEM((NORM_ROWS, LANES), F32)],
        compiler_params=_params(vmem, 1),
        name="rms_stack",
    )(x_prompt, x_sample, gain)


def _head_rms(y, seg2_ref, gain_row):
    y2 = y * y
    hi = y2.astype(BF16)
    lo = (y2 - hi.astype(F32)).astype(BF16)
    seg2 = seg2_ref[...]
    w = seg2.shape[1]
    ssq = jnp.concatenate(
        [jnp.dot(jnp.concatenate([hi[:, c:c + w], lo[:, c:c + w]], axis=1), seg2,
                 preferred_element_type=F32)
         for c in range(0, y.shape[1], w)], axis=-1)
    return y * lax.rsqrt(ssq * (1.0 / HEAD_DIM) + RMS_EPS) * gain_row


def _row_factor(ssq_ref, k, n):
    r = lax.rsqrt(ssq_ref[...] * (1.0 / k) + RMS_EPS)
    return jnp.tile(r, (1, n // LANES))


def _sgu_in_kernel(x_hbm, w_ref, o_ref, x_buf, sem):
    x_ref = _lhs_tile(x_hbm, x_buf, sem)
    w = w_ref[...].astype(BF16)
    for rows in _row_groups(x_ref.shape[0]):
        y = jnp.dot(x_ref[rows, :], w, preferred_element_type=F32)
        o_ref[rows, :] = 0.5 * y * (1.0 + lax.erf(y * (0.5 ** 0.5)))


def _sgu_in(xn, w, layer, *, bm, bn):
    m, k = xn.shape
    n = w.shape[2]
    wb = w.dtype.itemsize
    vmem = 2 * bm * k * 2 + k * bn * (2 * wb + 2) + 5 * bm * bn * 4
    return pl.pallas_call(
        _sgu_in_kernel,
        grid=(m // bm, n // bn),
        in_specs=[_LHS_SPEC, _layer_spec(k, bn, layer, lambda i, j: j)],
        out_specs=pl.BlockSpec((bm, bn), lambda i, j: (i, j)),
        out_shape=jax.ShapeDtypeStruct((m, n), F32),
        scratch_shapes=_lhs_scratch(bm, k),
        compiler_params=_params(vmem, 2),
        name="sgu_in",
    )(xn, w)


def _sgu_mix_rows(u_ref, v_ref, lng_ref, lnb_ref, ws_ref, bs_ref, a_ref, vn_ref, *, chunk):
    ln_g = lng_ref[...]
    ln_b = lnb_ref[...]
    for c in range(u_ref.shape[0] // chunk):
        rows = slice(c * chunk, (c + 1) * chunk)
        v = v_ref[rows, :]
        mu = jnp.mean(v, axis=-1, keepdims=True)
        vc = v - mu
        var = jnp.mean(vc * vc, axis=-1, keepdims=True)
        vn = vc * lax.rsqrt(var + LN_EPS) * ln_g + ln_b
        if vn_ref is not None:
            vn_ref[rows, :] = vn
        vnb = vn.astype(BF16)
        for g in range(SGU_GROUPS):
            cols = slice(g * SGU_GROUP_WIDTH, (g + 1) * SGU_GROUP_WIDTH)
            mixed = jnp.dot(ws_ref[g], vnb[:, cols], preferred_element_type=F32)
            mixed = mixed + bs_ref[:, g:g + 1]
            a_ref[rows, cols] = (u_ref[rows, cols] * mixed).astype(BF16)


def _sgu_mix_kernel(u_ref, v_ref, lng_ref, lnb_ref, ws_ref, bs_ref, wss_ref, bss_ref,
                    a_ref, vn_ref, *, prompt_blocks, sample_chunk):
    common = (u_ref, v_ref, lng_ref, lnb_ref)
    is_prompt = pl.program_id(0) < prompt_blocks
    pl.when(is_prompt)(
        lambda: _sgu_mix_rows(*common, ws_ref, bs_ref, a_ref, None, chunk=SGU_CHUNK))
    pl.when(jnp.logical_not(is_prompt))(
        lambda: _sgu_mix_rows(*common, wss_ref, bss_ref, a_ref, vn_ref, chunk=sample_chunk))


def _sgu_mix(uv, ln_g, ln_b, ws, bs_t, *, m_prompt, sample_chunk):
    m = uv.shape[0]
    d = D_MODEL
    rows = m - m_prompt
    assert m_prompt % rows == 0 and rows % SGU_CHUNK == 0 and rows % sample_chunk == 0
    prompt_blocks = m_prompt // rows
    const = lambda shape: pl.BlockSpec(shape, lambda i: (0,) * len(shape))
    wss, bss = ws[:, :sample_chunk, :sample_chunk], bs_t[:sample_chunk]
    vmem = 2 * 2 * rows * d * 4 + 2 * rows * d * 2 + 2 * rows * d * 4 + 6 * SGU_CHUNK * d * 4
    return pl.pallas_call(
        functools.partial(_sgu_mix_kernel, prompt_blocks=prompt_blocks,
                          sample_chunk=sample_chunk),
        grid=(prompt_blocks + 1,),
        in_specs=[
            pl.BlockSpec((rows, d), lambda i: (i, 0)),
            pl.BlockSpec((rows, d), lambda i: (i, 1)),
            const((1, d)), const((1, d)),
            const(ws.shape), const(bs_t.shape), const(wss.shape), const(bss.shape),
        ],
        out_specs=[pl.BlockSpec((rows, d), lambda i: (i, 0)), const((rows, d))],
        out_shape=[jax.ShapeDtypeStruct((m, d), BF16), jax.ShapeDtypeStruct((rows, d), F32)],
        compiler_params=_params(vmem, 1),
        name="sgu_mix",
    )(uv, uv, ln_g, ln_b, ws, bs_t, wss, bss)


def _residual_kernel(x_hbm, w_ref, *refs, n_res, n_gains, top_rows, tail_rows, col_block):
    x_ref = _lhs_tile(x_hbm, *refs[-2:], col_block=col_block)
    refs = refs[:-2]
    res_refs, refs = refs[:n_res], refs[n_res:]
    gain_refs, refs = refs[:n_gains], refs[n_gains:]
    n_main = 2 if tail_rows else 1
    main_refs, refs = refs[:n_main], refs[n_main:]
    scaled_refs, ssq_refs = refs[:n_gains], refs[n_gains:]
    o_ref = main_refs[0]
    w = w_ref[...].astype(BF16)
    bm = x_ref.shape[0]
    squares = []
    plain = n_res == 1 and n_gains == 0
    for rows in (_row_groups(bm) if plain else [slice(0, bm)]):
        y = jnp.dot(x_ref[rows, :], w, preferred_element_type=F32)
        if n_res == 1:
            h = res_refs[0][rows, :] + y
            o_ref[rows, :] = h
        else:
            top_ref, tail_ref = res_refs
            is_last = pl.program_id(0) == pl.num_programs(0) - 1
            straddle = jnp.where(is_last, tail_ref[...], top_ref[top_rows:, :])
            h = jnp.concatenate([top_ref[:top_rows, :], straddle], axis=0) + y
            o_ref[...] = h
        for g_ref, s_ref in zip(gain_refs, scaled_refs):
            s_ref[rows, :] = (h * g_ref[...]).astype(BF16)
        if n_gains:
            squares.append(jnp.sum(h * h, axis=-1, keepdims=True))
    if tail_rows:
        main_refs[1][...] = o_ref[bm - tail_rows:, :]
    if n_gains:
        (ssq_ref,) = ssq_refs
        part = jnp.broadcast_to(jnp.concatenate(squares, axis=0), ssq_ref.shape)
        first = pl.program_id(1) == 0

        @pl.when(first)
        def _():
            ssq_ref[...] = part

        @pl.when(jnp.logical_not(first))
        def _():
            ssq_ref[...] += part


def _residual(x, w, layer, res, *, bm, bn, name, part=(0, 1), gains=(), split_rows=None):
    m, k = x.shape
    n = w.shape[2]
    p, n_parts = part
    kc = k // n_parts
    assert kc * n_parts == k and kc % LANES == 0
    wb = w.dtype.itemsize
    tile = pl.BlockSpec((bm, bn), lambda i, j: (i, j))
    last = m // bm - 1

    top_rows = 0
    if isinstance(res, tuple):
        top, tail = res
        top_rows = bm - tail.shape[0]
        assert top.shape[0] + tail.shape[0] == m and top_rows > 0
        assert top_rows % F32_SUBLANES == 0
        res_specs = [tile, pl.BlockSpec((tail.shape[0], bn), lambda i, j: (0, j))]
        res_args = [top, tail]
    else:
        res_specs, res_args = [tile], [res]

    tail_rows = 0
    if split_rows is None:
        out_specs, out_shape = [tile], [jax.ShapeDtypeStruct((m, n), F32)]
    else:
        tail_rows = m - split_rows
        assert last * bm < split_rows and tail_rows <= bm
        out_specs = [tile, pl.BlockSpec((tail_rows, bn),
                                        lambda i, j: (0, jnp.where(i == last, j, 0)))]
        out_shape = [jax.ShapeDtypeStruct((split_rows, n), F32),
                     jax.ShapeDtypeStruct((tail_rows, n), F32)]
    out_specs += [tile] * len(gains)
    out_shape += [jax.ShapeDtypeStruct((m, n), BF16)] * len(gains)
    if gains:
        out_specs.append(pl.BlockSpec((bm, LANES), lambda i, j: (i, 0)))
        out_shape.append(jax.ShapeDtypeStruct((m, LANES), F32))

    vmem = 2 * bm * kc * 2 + kc * bn * (2 * wb + 2) + (8 + 2 * len(gains)) * bm * bn * 4
    return pl.pallas_call(
        functools.partial(_residual_kernel, n_res=len(res_args), n_gains=len(gains),
                          top_rows=top_rows, tail_rows=tail_rows, col_block=p),
        grid=(m // bm, n // bn),
        in_specs=[
            _LHS_SPEC,
            _layer_spec(kc, bn, layer, lambda i, j: j, row_block=p),
        ] + res_specs + [pl.BlockSpec((1, bn), lambda i, j: (0, j))] * len(gains),
        out_specs=out_specs,
        out_shape=out_shape,
        scratch_shapes=_lhs_scratch(bm, kc),
        compiler_params=_params(vmem, 2),
        name=name,
    )(x, w, *res_args, *gains)


def _ffn_up_kernel(x_hbm, ssq_ref, wg_ref, wu_ref, o_ref, x_buf, sem):
    x_ref = _lhs_tile(x_hbm, x_buf, sem)
    wg = wg_ref[...].astype(BF16)
    wu = wu_ref[...].astype(BF16)
    k = x_ref.shape[1]
    for rows in _row_groups(x_ref.shape[0]):
        x = x_ref[rows, :]
        r = _row_factor(ssq_ref.at[rows, :], k, o_ref.shape[1])
        gate = jnp.dot(x, wg, preferred_element_type=F32) * r
        up = jnp.dot(x, wu, preferred_element_type=F32) * r
        o_ref[rows, :] = (0.5 * gate * (1.0 + jnp.tanh(0.5 * gate)) * up).astype(BF16)


def _ffn_up(x, ssq, w_gate, w_up, layer, *, bm, bn):
    m, k = x.shape
    n = w_gate.shape[2]
    wb = w_gate.dtype.itemsize
    vmem = 2 * bm * k * 2 + 2 * k * bn * (2 * wb + 2) + 2 * bm * bn * 2 + 5 * bm * bn * 4
    return pl.pallas_call(
        _ffn_up_kernel,
        grid=(m // bm, n // bn),
        in_specs=[
            _LHS_SPEC,
            pl.BlockSpec((bm, LANES), lambda i, j: (i, 0)),
            _layer_spec(k, bn, layer, lambda i, j: j),
            _layer_spec(k, bn, layer, lambda i, j: j),
        ],
        scratch_shapes=_lhs_scratch(bm, k),
        out_specs=pl.BlockSpec((bm, bn), lambda i, j: (i, j)),
        out_shape=jax.ShapeDtypeStruct((m, n), BF16),
        compiler_params=_params(vmem, 2),
        name="ffn_up",
    )(x, ssq, w_gate, w_up)


LOG2_E = 1.4426950408889634
SCORE_SCALE = HEAD_DIM ** -0.5 * LOG2_E


def _q_proj_kernel(x_hbm, ssq_ref, w_ref, seg2_ref, qg_ref, q_ref, x_buf, sem):
    x = _lhs_tile(x_hbm, x_buf, sem)[...]
    y = jnp.dot(x, w_ref[...].astype(BF16), preferred_element_type=F32)
    y = y * _row_factor(ssq_ref, x.shape[1], y.shape[1])
    q_ref[...] = _head_rms(y, seg2_ref, qg_ref[...]).astype(BF16)


def _q_proj(x, ssq, w, layer, seg2, q_gain, *, bm, bn):
    m, k = x.shape
    n = w.shape[2]
    wb = w.dtype.itemsize
    vmem = 2 * bm * k * 2 + k * bn * (2 * wb + 2) + 10 * bm * bn * 4
    return pl.pallas_call(
        _q_proj_kernel,
        grid=(m // bm, n // bn),
        in_specs=[
            _LHS_SPEC,
            pl.BlockSpec((bm, LANES), lambda i, j: (i, 0)),
            _layer_spec(k, bn, layer, lambda i, j: j),
            pl.BlockSpec(seg2.shape, lambda i, j: (0, 0)),
            pl.BlockSpec((1, bn), lambda i, j: (0, 0)),
        ],
        out_specs=pl.BlockSpec((bm, bn), lambda i, j: (i, j)),
        out_shape=jax.ShapeDtypeStruct((m, n), BF16),
        scratch_shapes=_lhs_scratch(bm, k),
        compiler_params=_params(vmem, 2),
        name="q_proj",
    )(x, ssq, w, seg2, q_gain)


def _kv_proj_kernel(x_ref, ssq_ref, w_ref, seg2_ref, kg_ref, kf_ref, vf_ref, kz_ref, vz_ref):
    x = x_ref[...]
    y = jnp.dot(x, w_ref[...], preferred_element_type=F32)
    y = y * _row_factor(ssq_ref, x.shape[1], y.shape[1])
    kn = _head_rms(y[:, :KV_WIDTH], seg2_ref, kg_ref[...])
    v = y[:, KV_WIDTH:]
    kf_ref[...] = kn
    vf_ref[...] = v
    low = lax.broadcasted_iota(jnp.int32, (y.shape[0], LANES), 1) < HEAD_DIM
    for src, dst in ((kn, kz_ref), (v, vz_ref)):
        for p in range(N_KV_HEADS // HEADS_PER_VREG):
            both = src[:, p * LANES:(p + 1) * LANES]
            swapped = pltpu.roll(both, HEAD_DIM, axis=1)
            dst[2 * p, 0] = jnp.where(low, both, 0.0).astype(BF16)
            dst[2 * p, 1] = jnp.where(low, 0.0, swapped).astype(BF16)
            dst[2 * p + 1, 0] = jnp.where(low, swapped, 0.0).astype(BF16)
            dst[2 * p + 1, 1] = jnp.where(low, 0.0, both).astype(BF16)


def _kv_proj(x, ssq, w, seg2, k_gain, *, bm):
    m, k = x.shape
    n = w.shape[1]
    vmem = (2 * bm * k * 2 + 2 * k * n * 2 + 8 * bm * n * 4
            + 2 * 2 * N_KV_HEADS * 2 * bm * LANES * 2)
    z_spec = pl.BlockSpec((N_KV_HEADS, 2, bm, LANES), lambda i: (0, 0, i, 0))
    row_spec = pl.BlockSpec((bm, KV_WIDTH), lambda i: (i, 0))
    z_shape = jax.ShapeDtypeStruct((N_KV_HEADS, 2, m, LANES), BF16)
    return pl.pallas_call(
        _kv_proj_kernel,
        grid=(m // bm,),
        in_specs=[
            pl.BlockSpec((bm, k), lambda i: (i, 0)),
            pl.BlockSpec((bm, LANES), lambda i: (i, 0)),
            pl.BlockSpec((k, n), lambda i: (0, 0)),
            pl.BlockSpec(seg2.shape, lambda i: (0, 0)),
            pl.BlockSpec((1, KV_WIDTH), lambda i: (0, 0)),
        ],
        out_specs=[row_spec, row_spec, z_spec, z_spec],
        out_shape=[
            jax.ShapeDtypeStruct((m, KV_WIDTH), F32),
            jax.ShapeDtypeStruct((m, KV_WIDTH), F32),
            z_shape, z_shape,
        ],
        compiler_params=_params(vmem, 1),
        name="kv_proj",
    )(x, ssq, w, seg2, k_gain)


def _attend(q_ref, o_ref, sink_ref, key_parts, val_parts, *, row0, tq, mask_lanes=None,
            score_bound=None):
    rows = PAIRS * tq
    lane_pair = lax.broadcasted_iota(jnp.int32, (1, rows), 1) >> (tq.bit_length() - 1)
    for kh in range(N_KV_HEADS):
        base = kh * GQA * HEAD_DIM
        q4 = jnp.concatenate(
            [q_ref[row0:row0 + tq, base + p * LANES:base + (p + 1) * LANES]
             for p in range(PAIRS)], axis=0)
        kk = jnp.concatenate(key_parts(kh, 0) + key_parts(kh, 1), axis=0)
        vv = jnp.concatenate(val_parts(kh, 0) + val_parts(kh, 1), axis=0)
        n_keys = kk.shape[0] // HEADS_PER_VREG
        if mask_lanes is not None:
            kk = jnp.concatenate([kk, mask_lanes[0]], axis=1)
            q4 = jnp.concatenate([q4, mask_lanes[1]], axis=1)
        s_t = lax.dot_general(kk, q4, (((1,), (1,)), ((), ())),
                              preferred_element_type=F32)
        exps, denoms = [], []
        for parity in range(HEADS_PER_VREG):
            head = kh * GQA + parity
            sink = jnp.full((1, rows), sink_ref[head + HEADS_PER_VREG * (PAIRS - 1)], F32)
            for p in range(PAIRS - 1):
                sink = jnp.where(lane_pair == p, sink_ref[head + HEADS_PER_VREG * p], sink)
            sink = sink * LOG2_E
            half = s_t[parity * n_keys:(parity + 1) * n_keys]
            if score_bound is None:
                mx = jnp.maximum(jnp.max(half, axis=0, keepdims=True), sink)
            else:
                mx = jnp.maximum(score_bound, sink)
            e = jnp.exp2(half - mx)
            denom = jnp.sum(e, axis=0, keepdims=True) + jnp.exp2(sink - mx)
            denoms.append(jnp.broadcast_to(denom, (HEAD_DIM, rows)))
            exps.append(e.astype(BF16))
        o_t = lax.dot_general(vv, jnp.concatenate(exps, axis=0),
                              (((0,), (0,)), ((), ())), preferred_element_type=F32)
        o = (o_t / jnp.concatenate(denoms, axis=0)).T
        for p in range(PAIRS):
            o_ref[row0:row0 + tq, base + p * LANES:base + (p + 1) * LANES] = (
                o[p * tq:(p + 1) * tq].astype(BF16))


def _attend_prompt_block(q_ref, o_ref, sink_ref, key_parts, val_parts, first, *, row0,
                         score_bound=None):
    tq = ATT_ROWS
    n_keys = 2 * tq
    chunk_shift = CHUNK.bit_length() - 1
    chunks_per_block = tq // CHUNK
    lane = lax.broadcasted_iota(jnp.int32, (2 * n_keys, LANES), 1)
    key = lax.broadcasted_iota(jnp.int32, (2 * n_keys, LANES), 0)
    k_chunk = ((key & (n_keys - 1)) >> chunk_shift) - chunks_per_block
    oldest = jnp.where(first, 0, -chunks_per_block)
    visible = (k_chunk <= lane) & (k_chunk >= jnp.maximum(lane - 2, oldest))
    key_side = jnp.where((lane < chunks_per_block) & jnp.logical_not(visible), NEG, 0.0)
    query = lax.broadcasted_iota(jnp.int32, (PAIRS * tq, LANES), 0)
    q_lane = lax.broadcasted_iota(jnp.int32, (PAIRS * tq, LANES), 1)
    query_side = jnp.where(((query & (tq - 1)) >> chunk_shift) == q_lane, 1.0, 0.0)
    _attend(q_ref, o_ref, sink_ref, key_parts, val_parts,
            row0=row0, tq=tq, mask_lanes=(key_side.astype(BF16), query_side.astype(BF16)),
            score_bound=score_bound)


def _attend_prompt_step(q_ref, kp_ref, kc_ref, vp_ref, vc_ref, sink_ref, o_ref, first,
                        score_bound=None):
    tq = ATT_ROWS
    for b in range(ATT_STEP_ROWS // tq):
        here = slice(b * tq, (b + 1) * tq)
        behind = slice((b - 1) * tq, b * tq)
        if b == 0:
            parts = lambda prev, cur, here=here: (
                lambda kh, parity: [prev[kh, parity], cur[kh, parity, here, :]])
        else:
            parts = lambda prev, cur, here=here, behind=behind: (
                lambda kh, parity: [cur[kh, parity, behind, :], cur[kh, parity, here, :]])
        _attend_prompt_block(q_ref, o_ref, sink_ref, parts(kp_ref, kc_ref), parts(vp_ref, vc_ref),
                             first if b == 0 else False, row0=b * tq, score_bound=score_bound)


def _attend_sample_block(q_ref, ck_ref, cv_ref, kc_ref, vc_ref, sink_ref, o_ref, *, dec_seq):
    for st in range(ATT_STEP_ROWS // dec_seq):
        new = slice(st * dec_seq, (st + 1) * dec_seq)
        parts = lambda cache, cur, st=st, new=new: (
            lambda kh, parity: [cache[st, kh, parity], cur[kh, parity, new, :]])
        _attend(q_ref, o_ref, sink_ref, parts(ck_ref, kc_ref), parts(cv_ref, vc_ref),
                row0=st * dec_seq, tq=dec_seq)


def _attn_kernel(q_ref, kp_ref, kc_ref, vp_ref, vc_ref, ck_ref, cv_ref, sink_ref, bound_ref,
                 o_ref, *, prompt_blocks, blocks_per_seq, dec_seq):
    step = pl.program_id(0)
    is_prompt = step < prompt_blocks
    first = (step & (blocks_per_seq - 1)) == 0
    prompt_refs = (q_ref, kp_ref, kc_ref, vp_ref, vc_ref, sink_ref, o_ref, first)
    bound = bound_ref[0]
    use_bound = bound < MAX_SCORE_SHIFT
    pl.when(is_prompt & use_bound)(
        lambda: _attend_prompt_step(*prompt_refs, score_bound=bound))
    pl.when(is_prompt & jnp.logical_not(use_bound))(
        lambda: _attend_prompt_step(*prompt_refs))
    pl.when(jnp.logical_not(is_prompt))(lambda: _attend_sample_block(
        q_ref, ck_ref, cv_ref, kc_ref, vc_ref, sink_ref, o_ref, dec_seq=dec_seq))


MAX_SCORE_SHIFT = 60.0


def _score_bound(q_gain, k_gain):
    margin = 1.02
    bound = margin * SCORE_SCALE * HEAD_DIM * jnp.max(jnp.abs(q_gain)) * jnp.max(jnp.abs(k_gain))
    return bound.reshape(1).astype(F32)


def _attn(q, kz, vz, cache_kz, cache_vz, sinks, score_bound, *, m_prompt, seq, dec_seq):
    m = q.shape[0]
    tq = ATT_STEP_ROWS
    blocks_per_step = tq // ATT_ROWS
    steps_per_seq = seq // tq
    prompt_blocks = m_prompt // tq
    streams_per_block = tq // dec_seq
    assert steps_per_seq & (steps_per_seq - 1) == 0 and (m - m_prompt) % tq == 0
    blocks_per_seq = steps_per_seq
    rows_here = lambda s: (s, 0)
    cur = lambda s: (0, 0, s, 0)
    prev = lambda s: (0, 0, jnp.where((s & (steps_per_seq - 1)) == 0, s * blocks_per_step,
                                      s * blocks_per_step - 1), 0)
    cache = lambda s: (jnp.maximum(s - prompt_blocks, 0), 0, 0, 0, 0)
    z_block = (N_KV_HEADS, 2, tq, LANES)
    behind_block = (N_KV_HEADS, 2, ATT_ROWS, LANES)
    cache_block = (streams_per_block, N_KV_HEADS, 2, WINDOW, LANES)
    vmem = 4 * tq * D_MODEL * 2 + 6 * N_KV_HEADS * 2 * tq * LANES * 2 \
        + 4 * streams_per_block * N_KV_HEADS * 2 * WINDOW * LANES * 2 \
        + 10 * PAIRS * ATT_ROWS * 4 * ATT_ROWS * 4
    return pl.pallas_call(
        functools.partial(_attn_kernel, prompt_blocks=prompt_blocks,
                          blocks_per_seq=blocks_per_seq, dec_seq=dec_seq),
        grid=(m // tq,),
        in_specs=[
            pl.BlockSpec((tq, D_MODEL), rows_here),
            pl.BlockSpec(behind_block, prev),
            pl.BlockSpec(z_block, cur),
            pl.BlockSpec(behind_block, prev),
            pl.BlockSpec(z_block, cur),
            pl.BlockSpec(cache_block, cache),
            pl.BlockSpec(cache_block, cache),
            pl.BlockSpec(memory_space=pltpu.SMEM),
            pl.BlockSpec(memory_space=pltpu.SMEM),
        ],
        out_specs=pl.BlockSpec((tq, D_MODEL), rows_here),
        out_shape=jax.ShapeDtypeStruct((m, D_MODEL), BF16),
        compiler_params=_params(vmem, 1),
        name="attn",
    )(q, kz, kz, vz, vz, cache_kz, cache_vz, sinks, score_bound)


def _lane_pair_copies(cache):
    c = cache.transpose(0, 2, 1, 3).astype(BF16)
    z = jnp.zeros_like(c)
    return jnp.stack([jnp.concatenate([c, z], axis=-1), jnp.concatenate([z, c], axis=-1)], axis=2)


def kernel(x_prompt, x_sample, cache_k, cache_v, norm_a, w_sgu_in, sgu_ln_g, sgu_ln_b, w_sgu_s,
           b_sgu_s, w_sgu_out, norm_kv, w_kv, k_norm, norm_b, w_q, q_norm, sinks, w_o,
           norm_ffn, w_ffn_gate, w_ffn_up, w_ffn_down):
    batch, seq, d = x_prompt.shape
    streams, dec_seq, _ = x_sample.shape
    m_prompt = batch * seq
    m_sample = streams * dec_seq
    m = m_prompt + m_sample
    assert d == D_MODEL and seq % ATT_STEP_ROWS == 0 and m_sample % ATT_STEP_ROWS == 0
    assert m % WIDE_TILE == 0 and m % KV_TILE == 0
    assert norm_a.shape[0] == 1 and norm_b.shape[0] == 1 and norm_ffn.shape[0] == 2

    row = lambda g: g.reshape(1, -1).astype(F32)
    tiles = dict(bm=WIDE_TILE, bn=MXU_WIDTH)

    def ffn(h, scaled, ssq, layer, **last_part):
        hidden = _ffn_up(scaled, ssq, w_ffn_gate, w_ffn_up, layer, **tiles)
        for p in range(FFN_DOWN_PARTS):
            final = p == FFN_DOWN_PARTS - 1
            outs = _residual(hidden, w_ffn_down, layer, h, part=(p, FFN_DOWN_PARTS),
                             name="ffn_down_%d_%d" % (layer, p), **tiles,
                             **(last_part if final else {}))
            h = outs[0]
        return outs

    x_rows = (x_prompt.reshape(m_prompt, d), x_sample.reshape(m_sample, d))
    xn = _rms_stack(*x_rows, row(norm_a[0]))
    uv = _sgu_in(xn, w_sgu_in, 0, bm=WIDE_TILE, bn=2 * MXU_WIDTH)

    pos_chunk = jnp.arange(SGU_CHUNK) // CHUNK
    ws = jnp.where((pos_chunk[:, None] >= pos_chunk[None, :])[None], w_sgu_s[0], 0.0).astype(BF16)
    bs_t = b_sgu_s[0].T.astype(F32)
    ln_g, ln_b = row(sgu_ln_g[0]), row(sgu_ln_b[0])
    a, v_rows = _sgu_mix(uv, ln_g, ln_b, ws, bs_t, m_prompt=m_prompt, sample_chunk=dec_seq)
    h, scaled, ssq = _residual(a, w_sgu_out, 0, x_rows, name="sgu_out", **tiles,
                               gains=[row(norm_ffn[0])])
    h, scaled_kv, scaled_q, ssq = ffn(h, scaled, ssq, 0, gains=[row(norm_kv), row(norm_b[0])])

    lane_head = jnp.arange(MXU_WIDTH) // HEAD_DIM
    seg = (lane_head[:, None] == lane_head[None, :]).astype(BF16)
    seg2 = jnp.concatenate([seg, seg], axis=0)
    k_rows, v_rows_kv, kz, vz = _kv_proj(
        scaled_kv, ssq, w_kv.astype(BF16), seg2, row(jnp.tile(k_norm, N_KV_HEADS)), bm=KV_TILE)
    q_gain = row(jnp.tile(q_norm[0], MXU_WIDTH // HEAD_DIM)) * SCORE_SCALE
    q = _q_proj(scaled_q, ssq, w_q, 0, seg2, q_gain, **tiles)
    sink = sinks[0].astype(F32)
    o = _attn(q, kz, vz, _lane_pair_copies(cache_k), _lane_pair_copies(cache_v), sink,
              _score_bound(q_norm[0], k_norm), m_prompt=m_prompt, seq=seq, dec_seq=dec_seq)
    h, scaled, ssq = _residual(o, w_o, 0, h, name="attn_out", **tiles,
                               gains=[row(norm_ffn[1])])
    y_prompt, y_sample = ffn(h, scaled, ssq, 1, split_rows=m_prompt)

    heads = lambda t: t.reshape(t.shape[:-1] + (N_KV_HEADS, HEAD_DIM))
    tail = lambda t: heads(t[:m_prompt].reshape(batch, seq, KV_WIDTH)[:, seq - WINDOW:])
    fresh = lambda t: heads(t[m_prompt:].reshape(streams, dec_seq, KV_WIDTH))
    k_p, v_p = tail(k_rows), tail(v_rows_kv)
    k_s, v_s = fresh(k_rows), fresh(v_rows_kv)
    return (y_prompt.reshape(batch, seq, d), y_sample.reshape(streams, dec_seq, d),
            k_p, v_p, k_s, v_s, v_rows.reshape(1, streams, dec_seq, d))
```

```python
import functools

import jax
import jax.numpy as jnp
from jax import lax
from jax.experimental import pallas as pl
from jax.experimental.pallas import tpu as pltpu

F32 = jnp.float32
BF16 = jnp.bfloat16

D_MODEL = 4096
SGU_CHUNK = 128
SGU_GROUPS = 8
SGU_GROUP_WIDTH = D_MODEL // SGU_GROUPS
CHUNK = 64
HEAD_DIM = 64
N_HEADS = 64
N_KV_HEADS = 8
GQA = N_HEADS // N_KV_HEADS
KV_WIDTH = N_KV_HEADS * HEAD_DIM
WINDOW = 128
NEG = -1e30
RMS_EPS = 1e-6
LN_EPS = 1e-5

V7X_VMEM_BYTES = 64 * 1024 * 1024
VMEM_LIMIT_CAP = V7X_VMEM_BYTES - 6 * 1024 * 1024
VMEM_ESTIMATE_MARGIN = 8 * 1024 * 1024
LANES = 128
F32_SUBLANES = 8
BF16_SUBLANES = 16
MXU_WIDTH = 256
HEADS_PER_VREG = LANES // HEAD_DIM
PAIRS = GQA // HEADS_PER_VREG

WIDE_TILE = 1408
KV_TILE = WIDE_TILE // 2
NORM_ROWS = 256
ATT_ROWS = 128
ATT_STEP_ROWS = 256
RMS_STAT_ROWS = 64
RMS_SCALE_ROWS = 32
FFN_DOWN_PARTS = 2
ROW_GROUPS = 4


def _params(vmem_bytes, n_axes):
    limit = min(int(vmem_bytes) + VMEM_ESTIMATE_MARGIN, VMEM_LIMIT_CAP)
    return pltpu.CompilerParams(
        dimension_semantics=("arbitrary",) * n_axes, vmem_limit_bytes=limit)


def _row_groups(rows):
    size = rows // ROW_GROUPS
    assert size * ROW_GROUPS == rows and size % BF16_SUBLANES == 0
    return [slice(g * size, (g + 1) * size) for g in range(ROW_GROUPS)]


_LHS_SPEC = pl.BlockSpec(memory_space=pl.ANY)


def _lhs_scratch(bm, kc):
    return [pltpu.VMEM((2, bm, kc), BF16), pltpu.SemaphoreType.DMA((2,))]


def _lhs_tile(x_hbm, x_buf, sem, col_block=0):
    _, bm, kc = x_buf.shape
    i, n_tiles = pl.program_id(0), pl.num_programs(0)
    slot = lax.rem(i, 2)

    def copy(tile, slot):
        src = x_hbm.at[pl.ds(tile * bm, bm), pl.ds(col_block * kc, kc)]
        return pltpu.make_async_copy(src, x_buf.at[slot], sem.at[slot])

    @pl.when(pl.program_id(1) == 0)
    def _():
        @pl.when(i == 0)
        def _():
            copy(0, 0).start()

        copy(i, slot).wait()

        @pl.when(i + 1 < n_tiles)
        def _():
            copy(i + 1, 1 - slot).start()

    return x_buf.at[slot]


def _layer_spec(k, bn, layer, col_of, row_block=0):
    return pl.BlockSpec((None, k, bn), lambda *idx: (layer, row_block, col_of(*idx)))


def _rms_rows(x_ref, g_ref, xn_ref, r_ref):
    rows, k = x_ref.shape

    def stats(i, carry):
        sl = pl.ds(pl.multiple_of(i * RMS_STAT_ROWS, RMS_STAT_ROWS), RMS_STAT_ROWS)
        x = x_ref[sl, :]
        ms = jnp.mean(x * x, axis=-1, keepdims=True)
        r_ref[sl, :] = jnp.broadcast_to(lax.rsqrt(ms + RMS_EPS), (RMS_STAT_ROWS, LANES))
        return carry

    lax.fori_loop(0, rows // RMS_STAT_ROWS, stats, 0)
    g = g_ref[...]

    def scale(i, carry):
        sl = pl.ds(pl.multiple_of(i * RMS_SCALE_ROWS, RMS_SCALE_ROWS), RMS_SCALE_ROWS)
        r = jnp.tile(r_ref[sl, :], (1, k // LANES))
        xn_ref[sl, :] = (x_ref[sl, :] * r * g).astype(BF16)
        return carry

    lax.fori_loop(0, rows // RMS_SCALE_ROWS, scale, 0)


def _rms_stack_kernel(xp_ref, xs_ref, g_ref, xn_ref, r_ref, *, prompt_blocks):
    from_ref = lambda x_ref: _rms_rows(x_ref, g_ref, xn_ref, r_ref)
    pl.when(pl.program_id(0) < prompt_blocks)(lambda: from_ref(xp_ref))
    pl.when(pl.program_id(0) >= prompt_blocks)(lambda: from_ref(xs_ref))


def _rms_stack(x_prompt, x_sample, gain):
    mp, k = x_prompt.shape
    ms = x_sample.shape[0]
    assert mp % NORM_ROWS == 0 and ms == NORM_ROWS
    prompt_blocks = mp // NORM_ROWS
    m = mp + ms
    vmem = 4 * NORM_ROWS * k * 4 + 2 * NORM_ROWS * k * 2 + 4 * RMS_STAT_ROWS * k * 4
    return pl.pallas_call(
        functools.partial(_rms_stack_kernel, prompt_blocks=prompt_blocks),
        grid=(m // NORM_ROWS,),
        in_specs=[
            pl.BlockSpec((NORM_ROWS, k), lambda i: (jnp.minimum(i, prompt_blocks - 1), 0)),
            pl.BlockSpec((NORM_ROWS, k), lambda i: (0, 0)),
            pl.BlockSpec((1, k), lambda i: (0, 0)),
        ],
        out_specs=pl.BlockSpec((NORM_ROWS, k), lambda i: (i, 0)),
        out_shape=jax.ShapeDtypeStruct((m, k), BF16),
        scratch_shapes=[pltpu.VMEM((NORM_ROWS, LANES), F32)],
        compiler_params=_params(vmem, 1),
        name="rms_stack",
    )(x_prompt, x_sample, gain)


def _head_rms(y, seg2_ref, gain_row):
    y2 = y * y
    hi = y2.astype(BF16)
    lo = (y2 - hi.astype(F32)).astype(BF16)
    seg2 = seg2_ref[...]
    w = seg2.shape[1]
    ssq = jnp.concatenate(
        [jnp.dot(jnp.concatenate([hi[:, c:c + w], lo[:, c:c + w]], axis=1), seg2,
                 preferred_element_type=F32)
         for c in range(0, y.shape[1], w)], axis=-1)
    return y * lax.rsqrt(ssq * (1.0 / HEAD_DIM) + RMS_EPS) * gain_row


def _row_factor(ssq_ref, k, n):
    r = lax.rsqrt(ssq_ref[...] * (1.0 / k) + RMS_EPS)
    return jnp.tile(r, (1, n // LANES))


def _sgu_in_kernel(x_hbm, w_ref, o_ref, x_buf, sem):
    x_ref = _lhs_tile(x_hbm, x_buf, sem)
    w = w_ref[...].astype(BF16)
    for rows in _row_groups(x_ref.shape[0]):
        y = jnp.dot(x_ref[rows, :], w, preferred_element_type=F32)
        o_ref[rows, :] = 0.5 * y * (1.0 + lax.erf(y * (0.5 ** 0.5)))


def _sgu_in(xn, w, layer, *, bm, bn):
    m, k = xn.shape
    n = w.shape[2]
    wb = w.dtype.itemsize
    vmem = 2 * bm * k * 2 + k * bn * (2 * wb + 2) + 5 * bm * bn * 4
    return pl.pallas_call(
        _sgu_in_kernel,
        grid=(m // bm, n // bn),
        in_specs=[_LHS_SPEC, _layer_spec(k, bn, layer, lambda i, j: j)],
        out_specs=pl.BlockSpec((bm, bn), lambda i, j: (i, j)),
        out_shape=jax.ShapeDtypeStruct((m, n), F32),
        scratch_shapes=_lhs_scratch(bm, k),
        compiler_params=_params(vmem, 2),
        name="sgu_in",
    )(xn, w)


def _sgu_mix_rows(u_ref, v_ref, lng_ref, lnb_ref, ws_ref, bs_ref, a_ref, vn_ref, *, chunk):
    ln_g = lng_ref[...]
    ln_b = lnb_ref[...]
    for c in range(u_ref.shape[0] // chunk):
        rows = slice(c * chunk, (c + 1) * chunk)
        v = v_ref[rows, :]
        mu = jnp.mean(v, axis=-1, keepdims=True)
        vc = v - mu
        var = jnp.mean(vc * vc, axis=-1, keepdims=True)
        vn = vc * lax.rsqrt(var + LN_EPS) * ln_g + ln_b
        if vn_ref is not None:
            vn_ref[rows, :] = vn
        vnb = vn.astype(BF16)
        for g in range(SGU_GROUPS):
            cols = slice(g * SGU_GROUP_WIDTH, (g + 1) * SGU_GROUP_WIDTH)
            mixed = jnp.dot(ws_ref[g], vnb[:, cols], preferred_element_type=F32)
            mixed = mixed + bs_ref[:, g:g + 1]
            a_ref[rows, cols] = (u_ref[rows, cols] * mixed).astype(BF16)


def _sgu_mix_kernel(u_ref, v_ref, lng_ref, lnb_ref, ws_ref, bs_ref, wss_ref, bss_ref,
                    a_ref, vn_ref, *, prompt_blocks, sample_chunk):
    common = (u_ref, v_ref, lng_ref, lnb_ref)
    is_prompt = pl.program_id(0) < prompt_blocks
    pl.when(is_prompt)(
        lambda: _sgu_mix_rows(*common, ws_ref, bs_ref, a_ref, None, chunk=SGU_CHUNK))
    pl.when(jnp.logical_not(is_prompt))(
        lambda: _sgu_mix_rows(*common, wss_ref, bss_ref, a_ref, vn_ref, chunk=sample_chunk))


def _sgu_mix(uv, ln_g, ln_b, ws, bs_t, *, m_prompt, sample_chunk):
    m = uv.shape[0]
    d = D_MODEL
    rows = m - m_prompt
    assert m_prompt % rows == 0 and rows % SGU_CHUNK == 0 and rows % sample_chunk == 0
    prompt_blocks = m_prompt // rows
    const = lambda shape: pl.BlockSpec(shape, lambda i: (0,) * len(shape))
    wss, bss = ws[:, :sample_chunk, :sample_chunk], bs_t[:sample_chunk]
    vmem = 2 * 2 * rows * d * 4 + 2 * rows * d * 2 + 2 * rows * d * 4 + 6 * SGU_CHUNK * d * 4
    return pl.pallas_call(
        functools.partial(_sgu_mix_kernel, prompt_blocks=prompt_blocks,
                          sample_chunk=sample_chunk),
        grid=(prompt_blocks + 1,),
        in_specs=[
            pl.BlockSpec((rows, d), lambda i: (i, 0)),
            pl.BlockSpec((rows, d), lambda i: (i, 1)),
            const((1, d)), const((1, d)),
            const(ws.shape), const(bs_t.shape), const(wss.shape), const(bss.shape),
        ],
        out_specs=[pl.BlockSpec((rows, d), lambda i: (i, 0)), const((rows, d))],
        out_shape=[jax.ShapeDtypeStruct((m, d), BF16), jax.ShapeDtypeStruct((rows, d), F32)],
        compiler_params=_params(vmem, 1),
        name="sgu_mix",
    )(uv, uv, ln_g, ln_b, ws, bs_t, wss, bss)


def _residual_kernel(x_hbm, w_ref, *refs, n_res, n_gains, top_rows, tail_rows, col_block):
    x_ref = _lhs_tile(x_hbm, *refs[-2:], col_block=col_block)
    refs = refs[:-2]
    res_refs, refs = refs[:n_res], refs[n_res:]
    gain_refs, refs = refs[:n_gains], refs[n_gains:]
    n_main = 2 if tail_rows else 1
    main_refs, refs = refs[:n_main], refs[n_main:]
    scaled_refs, ssq_refs = refs[:n_gains], refs[n_gains:]
    o_ref = main_refs[0]
    w = w_ref[...].astype(BF16)
    bm = x_ref.shape[0]
    squares = []
    plain = n_res == 1 and n_gains == 0
    for rows in (_row_groups(bm) if plain else [slice(0, bm)]):
        y = jnp.dot(x_ref[rows, :], w, preferred_element_type=F32)
        if n_res == 1:
            h = res_refs[0][rows, :] + y
            o_ref[rows, :] = h
        else:
            top_ref, tail_ref = res_refs
            is_last = pl.program_id(0) == pl.num_programs(0) - 1
            straddle = jnp.where(is_last, tail_ref[...], top_ref[top_rows:, :])
            h = jnp.concatenate([top_ref[:top_rows, :], straddle], axis=0) + y
            o_ref[...] = h
        for g_ref, s_ref in zip(gain_refs, scaled_refs):
            s_ref[rows, :] = (h * g_ref[...]).astype(BF16)
        if n_gains:
            squares.append(jnp.sum(h * h, axis=-1, keepdims=True))
    if tail_rows:
        main_refs[1][...] = o_ref[bm - tail_rows:, :]
    if n_gains:
        (ssq_ref,) = ssq_refs
        part = jnp.broadcast_to(jnp.concatenate(squares, axis=0), ssq_ref.shape)
        first = pl.program_id(1) == 0

        @pl.when(first)
        def _():
            ssq_ref[...] = part

        @pl.when(jnp.logical_not(first))
        def _():
            ssq_ref[...] += part


def _residual(x, w, layer, res, *, bm, bn, name, part=(0, 1), gains=(), split_rows=None):
    m, k = x.shape
    n = w.shape[2]
    p, n_parts = part
    kc = k // n_parts
    assert kc * n_parts == k and kc % LANES == 0
    wb = w.dtype.itemsize
    tile = pl.BlockSpec((bm, bn), lambda i, j: (i, j))
    last = m // bm - 1

    top_rows = 0
    if isinstance(res, tuple):
        top, tail = res
        top_rows = bm - tail.shape[0]
        assert top.shape[0] + tail.shape[0] == m and top_rows > 0
        assert top_rows % F32_SUBLANES == 0
        res_specs = [tile, pl.BlockSpec((tail.shape[0], bn), lambda i, j: (0, j))]
        res_args = [top, tail]
    else:
        res_specs, res_args = [tile], [res]

    tail_rows = 0
    if split_rows is None:
        out_specs, out_shape = [tile], [jax.ShapeDtypeStruct((m, n), F32)]
    else:
        tail_rows = m - split_rows
        assert last * bm < split_rows and tail_rows <= bm
        out_specs = [tile, pl.BlockSpec((tail_rows, bn),
                                        lambda i, j: (0, jnp.where(i == last, j, 0)))]
        out_shape = [jax.ShapeDtypeStruct((split_rows, n), F32),
                     jax.ShapeDtypeStruct((tail_rows, n), F32)]
    out_specs += [tile] * len(gains)
    out_shape += [jax.ShapeDtypeStruct((m, n), BF16)] * len(gains)
    if gains:
        out_specs.append(pl.BlockSpec((bm, LANES), lambda i, j: (i, 0)))
        out_shape.append(jax.ShapeDtypeStruct((m, LANES), F32))

    vmem = 2 * bm * kc * 2 + kc * bn * (2 * wb + 2) + (8 + 2 * len(gains)) * bm * bn * 4
    return pl.pallas_call(
        functools.partial(_residual_kernel, n_res=len(res_args), n_gains=len(gains),
                          top_rows=top_rows, tail_rows=tail_rows, col_block=p),
        grid=(m // bm, n // bn),
        in_specs=[
            _LHS_SPEC,
            _layer_spec(kc, bn, layer, lambda i, j: j, row_block=p),
        ] + res_specs + [pl.BlockSpec((1, bn), lambda i, j: (0, j))] * len(gains),
        out_specs=out_specs,
        out_shape=out_shape,
        scratch_shapes=_lhs_scratch(bm, kc),
        compiler_params=_params(vmem, 2),
        name=name,
    )(x, w, *res_args, *gains)


def _ffn_up_kernel(x_ref, ssq_ref, wg_hbm, wu_hbm, o_hbm, *, layer, bn):
    bm, k = x_ref.shape
    n = o_hbm.shape[1]

    def columns(wg_ref, wu_ref, o_ref):
        wg = wg_ref[...].astype(BF16)
        wu = wu_ref[...].astype(BF16)
        for rows in _row_groups(bm):
            x = x_ref[rows, :]
            r = _row_factor(ssq_ref.at[rows, :], k, bn)
            gate = jnp.dot(x, wg, preferred_element_type=F32) * r
            up = jnp.dot(x, wu, preferred_element_type=F32) * r
            o_ref[rows, :] = (0.5 * gate * (1.0 + jnp.tanh(0.5 * gate)) * up).astype(BF16)

    row0 = pl.multiple_of(pl.program_id(0) * bm, BF16_SUBLANES)
    w_spec = pl.BlockSpec((k, bn), lambda j: (0, j))
    pltpu.emit_pipeline(
        columns,
        grid=(n // bn,),
        in_specs=[w_spec, w_spec],
        out_specs=[pl.BlockSpec((bm, bn), lambda j: (0, j))],
    )(wg_hbm.at[layer], wu_hbm.at[layer], o_hbm.at[pl.ds(row0, bm), :])


def _ffn_up(x, ssq, w_gate, w_up, layer, *, bm, bn):
    m, k = x.shape
    n = w_gate.shape[2]
    wb = w_gate.dtype.itemsize
    vmem = 2 * bm * k * 2 + 2 * k * bn * (2 * wb + 2) + 2 * bm * bn * 2 + 5 * bm * bn * 4
    return pl.pallas_call(
        functools.partial(_ffn_up_kernel, layer=layer, bn=bn),
        grid=(m // bm,),
        in_specs=[
            pl.BlockSpec((bm, k), lambda i: (i, 0)),
            pl.BlockSpec((bm, LANES), lambda i: (i, 0)),
            pl.BlockSpec(memory_space=pl.ANY),
            pl.BlockSpec(memory_space=pl.ANY),
        ],
        out_specs=pl.BlockSpec(memory_space=pl.ANY),
        out_shape=jax.ShapeDtypeStruct((m, n), BF16),
        compiler_params=_params(vmem, 1),
        name="ffn_up",
    )(x, ssq, w_gate, w_up)


LOG2_E = 1.4426950408889634
SCORE_SCALE = HEAD_DIM ** -0.5 * LOG2_E


def _q_proj_kernel(x_hbm, ssq_ref, w_ref, seg2_ref, qg_ref, q_ref, x_buf, sem):
    x = _lhs_tile(x_hbm, x_buf, sem)[...]
    y = jnp.dot(x, w_ref[...].astype(BF16), preferred_element_type=F32)
    y = y * _row_factor(ssq_ref, x.shape[1], y.shape[1])
    q_ref[...] = _head_rms(y, seg2_ref, qg_ref[...]).astype(BF16)


def _q_proj(x, ssq, w, layer, seg2, q_gain, *, bm, bn):
    m, k = x.shape
    n = w.shape[2]
    wb = w.dtype.itemsize
    vmem = 2 * bm * k * 2 + k * bn * (2 * wb + 2) + 10 * bm * bn * 4
    return pl.pallas_call(
        _q_proj_kernel,
        grid=(m // bm, n // bn),
        in_specs=[
            _LHS_SPEC,
            pl.BlockSpec((bm, LANES), lambda i, j: (i, 0)),
            _layer_spec(k, bn, layer, lambda i, j: j),
            pl.BlockSpec(seg2.shape, lambda i, j: (0, 0)),
            pl.BlockSpec((1, bn), lambda i, j: (0, 0)),
        ],
        out_specs=pl.BlockSpec((bm, bn), lambda i, j: (i, j)),
        out_shape=jax.ShapeDtypeStruct((m, n), BF16),
        scratch_shapes=_lhs_scratch(bm, k),
        compiler_params=_params(vmem, 2),
        name="q_proj",
    )(x, ssq, w, seg2, q_gain)


def _kv_proj_kernel(x_ref, ssq_ref, w_ref, seg2_ref, kg_ref, kf_ref, vf_ref, kz_ref, vz_ref):
    x = x_ref[...]
    y = jnp.dot(x, w_ref[...], preferred_element_type=F32)
    y = y * _row_factor(ssq_ref, x.shape[1], y.shape[1])
    kn = _head_rms(y[:, :KV_WIDTH], seg2_ref, kg_ref[...])
    v = y[:, KV_WIDTH:]
    kf_ref[...] = kn
    vf_ref[...] = v
    low = lax.broadcasted_iota(jnp.int32, (y.shape[0], LANES), 1) < HEAD_DIM
    for src, dst in ((kn, kz_ref), (v, vz_ref)):
        for p in range(N_KV_HEADS // HEADS_PER_VREG):
            both = src[:, p * LANES:(p + 1) * LANES]
            swapped = pltpu.roll(both, HEAD_DIM, axis=1)
            dst[2 * p, 0] = jnp.where(low, both, 0.0).astype(BF16)
            dst[2 * p, 1] = jnp.where(low, 0.0, swapped).astype(BF16)
            dst[2 * p + 1, 0] = jnp.where(low, swapped, 0.0).astype(BF16)
            dst[2 * p + 1, 1] = jnp.where(low, 0.0, both).astype(BF16)


def _kv_proj(x, ssq, w, seg2, k_gain, *, bm):
    m, k = x.shape
    n = w.shape[1]
    vmem = (2 * bm * k * 2 + 2 * k * n * 2 + 8 * bm * n * 4
            + 2 * 2 * N_KV_HEADS * 2 * bm * LANES * 2)
    z_spec = pl.BlockSpec((N_KV_HEADS, 2, bm, LANES), lambda i: (0, 0, i, 0))
    row_spec = pl.BlockSpec((bm, KV_WIDTH), lambda i: (i, 0))
    z_shape = jax.ShapeDtypeStruct((N_KV_HEADS, 2, m, LANES), BF16)
    return pl.pallas_call(
        _kv_proj_kernel,
        grid=(m // bm,),
        in_specs=[
            pl.BlockSpec((bm, k), lambda i: (i, 0)),
            pl.BlockSpec((bm, LANES), lambda i: (i, 0)),
            pl.BlockSpec((k, n), lambda i: (0, 0)),
            pl.BlockSpec(seg2.shape, lambda i: (0, 0)),
            pl.BlockSpec((1, KV_WIDTH), lambda i: (0, 0)),
        ],
        out_specs=[row_spec, row_spec, z_spec, z_spec],
        out_shape=[
            jax.ShapeDtypeStruct((m, KV_WIDTH), F32),
            jax.ShapeDtypeStruct((m, KV_WIDTH), F32),
            z_shape, z_shape,
        ],
        compiler_params=_params(vmem, 1),
        name="kv_proj",
    )(x, ssq, w, seg2, k_gain)


def _attend(q_ref, o_ref, sink_ref, key_parts, val_parts, *, row0, tq, mask_lanes=None,
            score_bound=None):
    rows = PAIRS * tq
    lane_pair = lax.broadcasted_iota(jnp.int32, (1, rows), 1) >> (tq.bit_length() - 1)
    for kh in range(N_KV_HEADS):
        base = kh * GQA * HEAD_DIM
        q4 = jnp.concatenate(
            [q_ref[row0:row0 + tq, base + p * LANES:base + (p + 1) * LANES]
             for p in range(PAIRS)], axis=0)
        kk = jnp.concatenate(key_parts(kh, 0) + key_parts(kh, 1), axis=0)
        vv = jnp.concatenate(val_parts(kh, 0) + val_parts(kh, 1), axis=0)
        n_keys = kk.shape[0] // HEADS_PER_VREG
        if mask_lanes is not None:
            kk = jnp.concatenate([kk, mask_lanes[0]], axis=1)
            q4 = jnp.concatenate([q4, mask_lanes[1]], axis=1)
        s_t = lax.dot_general(kk, q4, (((1,), (1,)), ((), ())),
                              preferred_element_type=F32)
        exps, denoms = [], []
        for parity in range(HEADS_PER_VREG):
            head = kh * GQA + parity
            sink = jnp.full((1, rows), sink_ref[head + HEADS_PER_VREG * (PAIRS - 1)], F32)
            for p in range(PAIRS - 1):
                sink = jnp.where(lane_pair == p, sink_ref[head + HEADS_PER_VREG * p], sink)
            sink = sink * LOG2_E
            half = s_t[parity * n_keys:(parity + 1) * n_keys]
            if score_bound is None:
                mx = jnp.maximum(jnp.max(half, axis=0, keepdims=True), sink)
            else:
                mx = jnp.maximum(score_bound, sink)
            e = jnp.exp2(half - mx)
            denom = jnp.sum(e, axis=0, keepdims=True) + jnp.exp2(sink - mx)
            denoms.append(jnp.broadcast_to(denom, (HEAD_DIM, rows)))
            exps.append(e.astype(BF16))
        o_t = lax.dot_general(vv, jnp.concatenate(exps, axis=0),
                              (((0,), (0,)), ((), ())), preferred_element_type=F32)
        o = (o_t / jnp.concatenate(denoms, axis=0)).T
        for p in range(PAIRS):
            o_ref[row0:row0 + tq, base + p * LANES:base + (p + 1) * LANES] = (
                o[p * tq:(p + 1) * tq].astype(BF16))


def _attend_prompt_block(q_ref, o_ref, sink_ref, key_parts, val_parts, first, *, row0,
                         score_bound=None):
    tq = ATT_ROWS
    n_keys = 2 * tq
    chunk_shift = CHUNK.bit_length() - 1
    chunks_per_block = tq // CHUNK
    lane = lax.broadcasted_iota(jnp.int32, (2 * n_keys, LANES), 1)
    key = lax.broadcasted_iota(jnp.int32, (2 * n_keys, LANES), 0)
    k_chunk = ((key & (n_keys - 1)) >> chunk_shift) - chunks_per_block
    oldest = jnp.where(first, 0, -chunks_per_block)
    visible = (k_chunk <= lane) & (k_chunk >= jnp.maximum(lane - 2, oldest))
    key_side = jnp.where((lane < chunks_per_block) & jnp.logical_not(visible), NEG, 0.0)
    query = lax.broadcasted_iota(jnp.int32, (PAIRS * tq, LANES), 0)
    q_lane = lax.broadcasted_iota(jnp.int32, (PAIRS * tq, LANES), 1)
    query_side = jnp.where(((query & (tq - 1)) >> chunk_shift) == q_lane, 1.0, 0.0)
    _attend(q_ref, o_ref, sink_ref, key_parts, val_parts,
            row0=row0, tq=tq, mask_lanes=(key_side.astype(BF16), query_side.astype(BF16)),
            score_bound=score_bound)


def _attend_prompt_step(q_ref, kp_ref, kc_ref, vp_ref, vc_ref, sink_ref, o_ref, first,
                        score_bound=None):
    tq = ATT_ROWS
    for b in range(ATT_STEP_ROWS // tq):
        here = slice(b * tq, (b + 1) * tq)
        behind = slice((b - 1) * tq, b * tq)
        if b == 0:
            parts = lambda prev, cur, here=here: (
                lambda kh, parity: [prev[kh, parity], cur[kh, parity, here, :]])
        else:
            parts = lambda prev, cur, here=here, behind=behind: (
                lambda kh, parity: [cur[kh, parity, behind, :], cur[kh, parity, here, :]])
        _attend_prompt_block(q_ref, o_ref, sink_ref, parts(kp_ref, kc_ref), parts(vp_ref, vc_ref),
                             first if b == 0 else False, row0=b * tq, score_bound=score_bound)


def _attend_sample_block(q_ref, ck_ref, cv_ref, kc_ref, vc_ref, sink_ref, o_ref, *, dec_seq):
    for st in range(ATT_STEP_ROWS // dec_seq):
        new = slice(st * dec_seq, (st + 1) * dec_seq)
        parts = lambda cache, cur, st=st, new=new: (
            lambda kh, parity: [cache[st, kh, parity], cur[kh, parity, new, :]])
        _attend(q_ref, o_ref, sink_ref, parts(ck_ref, kc_ref), parts(cv_ref, vc_ref),
                row0=st * dec_seq, tq=dec_seq)


def _attn_kernel(q_ref, kp_ref, kc_ref, vp_ref, vc_ref, ck_ref, cv_ref, sink_ref, bound_ref,
                 o_ref, *, prompt_blocks, blocks_per_seq, dec_seq):
    step = pl.program_id(0)
    is_prompt = step < prompt_blocks
    first = (step & (blocks_per_seq - 1)) == 0
    prompt_refs = (q_ref, kp_ref, kc_ref, vp_ref, vc_ref, sink_ref, o_ref, first)
    bound = bound_ref[0]
    use_bound = bound < MAX_SCORE_SHIFT
    pl.when(is_prompt & use_bound)(
        lambda: _attend_prompt_step(*prompt_refs, score_bound=bound))
    pl.when(is_prompt & jnp.logical_not(use_bound))(
        lambda: _attend_prompt_step(*prompt_refs))
    pl.when(jnp.logical_not(is_prompt))(lambda: _attend_sample_block(
        q_ref, ck_ref, cv_ref, kc_ref, vc_ref, sink_ref, o_ref, dec_seq=dec_seq))


MAX_SCORE_SHIFT = 60.0


def _score_bound(q_gain, k_gain):
    margin = 1.02
    bound = margin * SCORE_SCALE * HEAD_DIM * jnp.max(jnp.abs(q_gain)) * jnp.max(jnp.abs(k_gain))
    return bound.reshape(1).astype(F32)


def _attn(q, kz, vz, cache_kz, cache_vz, sinks, score_bound, *, m_prompt, seq, dec_seq):
    m = q.shape[0]
    tq = ATT_STEP_ROWS
    blocks_per_step = tq // ATT_ROWS
    steps_per_seq = seq // tq
    prompt_blocks = m_prompt // tq
    streams_per_block = tq // dec_seq
    assert steps_per_seq & (steps_per_seq - 1) == 0 and (m - m_prompt) % tq == 0
    blocks_per_seq = steps_per_seq
    rows_here = lambda s: (s, 0)
    cur = lambda s: (0, 0, s, 0)
    prev = lambda s: (0, 0, jnp.where((s & (steps_per_seq - 1)) == 0, s * blocks_per_step,
                                      s * blocks_per_step - 1), 0)
    cache = lambda s: (jnp.maximum(s - prompt_blocks, 0), 0, 0, 0, 0)
    z_block = (N_KV_HEADS, 2, tq, LANES)
    behind_block = (N_KV_HEADS, 2, ATT_ROWS, LANES)
    cache_block = (streams_per_block, N_KV_HEADS, 2, WINDOW, LANES)
    vmem = 4 * tq * D_MODEL * 2 + 6 * N_KV_HEADS * 2 * tq * LANES * 2 \
        + 4 * streams_per_block * N_KV_HEADS * 2 * WINDOW * LANES * 2 \
        + 10 * PAIRS * ATT_ROWS * 4 * ATT_ROWS * 4
    return pl.pallas_call(
        functools.partial(_attn_kernel, prompt_blocks=prompt_blocks,
                          blocks_per_seq=blocks_per_seq, dec_seq=dec_seq),
        grid=(m // tq,),
        in_specs=[
            pl.BlockSpec((tq, D_MODEL), rows_here),
            pl.BlockSpec(behind_block, prev),
            pl.BlockSpec(z_block, cur),
            pl.BlockSpec(behind_block, prev),
            pl.BlockSpec(z_block, cur),
            pl.BlockSpec(cache_block, cache),
            pl.BlockSpec(cache_block, cache),
            pl.BlockSpec(memory_space=pltpu.SMEM),
            pl.BlockSpec(memory_space=pltpu.SMEM),
        ],
        out_specs=pl.BlockSpec((tq, D_MODEL), rows_here),
        out_shape=jax.ShapeDtypeStruct((m, D_MODEL), BF16),
        compiler_params=_params(vmem, 1),
        name="attn",
    )(q, kz, kz, vz, vz, cache_kz, cache_vz, sinks, score_bound)


def _lane_pair_copies(cache):
    c = cache.transpose(0, 2, 1, 3).astype(BF16)
    z = jnp.zeros_like(c)
    return jnp.stack([jnp.concatenate([c, z], axis=-1), jnp.concatenate([z, c], axis=-1)], axis=2)


def kernel(x_prompt, x_sample, cache_k, cache_v, norm_a, w_sgu_in, sgu_ln_g, sgu_ln_b, w_sgu_s,
           b_sgu_s, w_sgu_out, norm_kv, w_kv, k_norm, norm_b, w_q, q_norm, sinks, w_o,
           norm_ffn, w_ffn_gate, w_ffn_up, w_ffn_down):
    batch, seq, d = x_prompt.shape
    streams, dec_seq, _ = x_sample.shape
    m_prompt = batch * seq
    m_sample = streams * dec_seq
    m = m_prompt + m_sample
    assert d == D_MODEL and seq % ATT_STEP_ROWS == 0 and m_sample % ATT_STEP_ROWS == 0
    assert m % WIDE_TILE == 0 and m % KV_TILE == 0
    assert norm_a.shape[0] == 1 and norm_b.shape[0] == 1 and norm_ffn.shape[0] == 2

    row = lambda g: g.reshape(1, -1).astype(F32)
    tiles = dict(bm=WIDE_TILE, bn=MXU_WIDTH)

    def ffn(h, scaled, ssq, layer, **last_part):
        hidden = _ffn_up(scaled, ssq, w_ffn_gate, w_ffn_up, layer, **tiles)
        for p in range(FFN_DOWN_PARTS):
            final = p == FFN_DOWN_PARTS - 1
            outs = _residual(hidden, w_ffn_down, layer, h, part=(p, FFN_DOWN_PARTS),
                             name="ffn_down_%d_%d" % (layer, p), **tiles,
                             **(last_part if final else {}))
            h = outs[0]
        return outs

    x_rows = (x_prompt.reshape(m_prompt, d), x_sample.reshape(m_sample, d))
    xn = _rms_stack(*x_rows, row(norm_a[0]))
    uv = _sgu_in(xn, w_sgu_in, 0, bm=WIDE_TILE, bn=2 * MXU_WIDTH)

    pos_chunk = jnp.arange(SGU_CHUNK) // CHUNK
    ws = jnp.where((pos_chunk[:, None] >= pos_chunk[None, :])[None], w_sgu_s[0], 0.0).astype(BF16)
    bs_t = b_sgu_s[0].T.astype(F32)
    ln_g, ln_b = row(sgu_ln_g[0]), row(sgu_ln_b[0])
    a, v_rows = _sgu_mix(uv, ln_g, ln_b, ws, bs_t, m_prompt=m_prompt, sample_chunk=dec_seq)
    h, scaled, ssq = _residual(a, w_sgu_out, 0, x_rows, name="sgu_out", **tiles,
                               gains=[row(norm_ffn[0])])
    h, scaled_kv, scaled_q, ssq = ffn(h, scaled, ssq, 0, gains=[row(norm_kv), row(norm_b[0])])

    lane_head = jnp.arange(MXU_WIDTH) // HEAD_DIM
    seg = (lane_head[:, None] == lane_head[None, :]).astype(BF16)
    seg2 = jnp.concatenate([seg, seg], axis=0)
    k_rows, v_rows_kv, kz, vz = _kv_proj(
        scaled_kv, ssq, w_kv.astype(BF16), seg2, row(jnp.tile(k_norm, N_KV_HEADS)), bm=KV_TILE)
    q_gain = row(jnp.tile(q_norm[0], MXU_WIDTH // HEAD_DIM)) * SCORE_SCALE
    q = _q_proj(scaled_q, ssq, w_q, 0, seg2, q_gain, **tiles)
    sink = sinks[0].astype(F32)
    o = _attn(q, kz, vz, _lane_pair_copies(cache_k), _lane_pair_copies(cache_v), sink,
              _score_bound(q_norm[0], k_norm), m_prompt=m_prompt, seq=seq, dec_seq=dec_seq)
    h, scaled, ssq = _residual(o, w_o, 0, h, name="attn_out", **tiles,
                               gains=[row(norm_ffn[1])])
    y_prompt, y_sample = ffn(h, scaled, ssq, 1, split_rows=m_prompt)

    heads = lambda t: t.reshape(t.shape[:-1] + (N_KV_HEADS, HEAD_DIM))
    tail = lambda t: heads(t[:m_prompt].reshape(batch, seq, KV_WIDTH)[:, seq - WINDOW:])
    fresh = lambda t: heads(t[m_prompt:].reshape(streams, dec_seq, KV_WIDTH))
    k_p, v_p = tail(k_rows), tail(v_rows_kv)
    k_s, v_s = fresh(k_rows), fresh(v_rows_kv)
    return (y_prompt.reshape(batch, seq, d), y_sample.reshape(streams, dec_seq, d),
            k_p, v_p, k_s, v_s, v_rows.reshape(1, streams, dec_seq, d))
```

```python
import functools

import jax
import jax.numpy as jnp
from jax import lax
from jax.experimental import pallas as pl
from jax.experimental.pallas import tpu as pltpu

F32 = jnp.float32
BF16 = jnp.bfloat16

D_MODEL = 4096
SGU_CHUNK = 128
SGU_GROUPS = 8
SGU_GROUP_WIDTH = D_MODEL // SGU_GROUPS
CHUNK = 64
HEAD_DIM = 64
N_HEADS = 64
N_KV_HEADS = 8
GQA = N_HEADS // N_KV_HEADS
KV_WIDTH = N_KV_HEADS * HEAD_DIM
WINDOW = 128
NEG = -1e30
RMS_EPS = 1e-6
LN_EPS = 1e-5

V7X_VMEM_BYTES = 64 * 1024 * 1024
VMEM_LIMIT_CAP = V7X_VMEM_BYTES - 6 * 1024 * 1024
VMEM_ESTIMATE_MARGIN = 8 * 1024 * 1024
LANES = 128
F32_SUBLANES = 8
BF16_SUBLANES = 16
MXU_WIDTH = 256
HEADS_PER_VREG = LANES // HEAD_DIM
PAIRS = GQA // HEADS_PER_VREG

WIDE_TILE = 1408
KV_TILE = WIDE_TILE // 2
NORM_ROWS = 256
ATT_ROWS = 128
ATT_STEP_ROWS = 256
RMS_STAT_ROWS = 64
RMS_SCALE_ROWS = 32
FFN_DOWN_PARTS = 2
ROW_GROUPS = 4


def _params(vmem_bytes, n_axes):
    limit = min(int(vmem_bytes) + VMEM_ESTIMATE_MARGIN, VMEM_LIMIT_CAP)
    return pltpu.CompilerParams(
        dimension_semantics=("arbitrary",) * n_axes, vmem_limit_bytes=limit)


def _row_groups(rows):
    size = rows // ROW_GROUPS
    assert size * ROW_GROUPS == rows and size % BF16_SUBLANES == 0
    return [slice(g * size, (g + 1) * size) for g in range(ROW_GROUPS)]


_LHS_SPEC = pl.BlockSpec(memory_space=pl.ANY)


def _lhs_scratch(bm, kc):
    return [pltpu.VMEM((2, bm, kc), BF16), pltpu.SemaphoreType.DMA((2,))]


def _lhs_tile(x_hbm, x_buf, sem, col_block=0):
    _, bm, kc = x_buf.shape
    i, n_tiles = pl.program_id(0), pl.num_programs(0)
    slot = lax.rem(i, 2)

    def copy(tile, slot):
        src = x_hbm.at[pl.ds(tile * bm, bm), pl.ds(col_block * kc, kc)]
        return pltpu.make_async_copy(src, x_buf.at[slot], sem.at[slot])

    @pl.when(pl.program_id(1) == 0)
    def _():
        @pl.when(i == 0)
        def _():
            copy(0, 0).start()

        copy(i, slot).wait()

        @pl.when(i + 1 < n_tiles)
        def _():
            copy(i + 1, 1 - slot).start()

    return x_buf.at[slot]


def _layer_spec(k, bn, layer, col_of, row_block=0):
    return pl.BlockSpec((None, k, bn), lambda *idx: (layer, row_block, col_of(*idx)))


def _rms_rows(x_ref, g_ref, xn_ref, r_ref):
    rows, k = x_ref.shape

    def stats(i, carry):
        sl = pl.ds(pl.multiple_of(i * RMS_STAT_ROWS, RMS_STAT_ROWS), RMS_STAT_ROWS)
        x = x_ref[sl, :]
        ms = jnp.mean(x * x, axis=-1, keepdims=True)
        r_ref[sl, :] = jnp.broadcast_to(lax.rsqrt(ms + RMS_EPS), (RMS_STAT_ROWS, LANES))
        return carry

    lax.fori_loop(0, rows // RMS_STAT_ROWS, stats, 0)
    g = g_ref[...]

    def scale(i, carry):
        sl = pl.ds(pl.multiple_of(i * RMS_SCALE_ROWS, RMS_SCALE_ROWS), RMS_SCALE_ROWS)
        r = jnp.tile(r_ref[sl, :], (1, k // LANES))
        xn_ref[sl, :] = (x_ref[sl, :] * r * g).astype(BF16)
        return carry

    lax.fori_loop(0, rows // RMS_SCALE_ROWS, scale, 0)


def _rms_stack_kernel(xp_ref, xs_ref, g_ref, xn_ref, r_ref, *, prompt_blocks):
    from_ref = lambda x_ref: _rms_rows(x_ref, g_ref, xn_ref, r_ref)
    pl.when(pl.program_id(0) < prompt_blocks)(lambda: from_ref(xp_ref))
    pl.when(pl.program_id(0) >= prompt_blocks)(lambda: from_ref(xs_ref))


def _rms_stack(x_prompt, x_sample, gain):
    mp, k = x_prompt.shape
    ms = x_sample.shape[0]
    assert mp % NORM_ROWS == 0 and ms == NORM_ROWS
    prompt_blocks = mp // NORM_ROWS
    m = mp + ms
    vmem = 4 * NORM_ROWS * k * 4 + 2 * NORM_ROWS * k * 2 + 4 * RMS_STAT_ROWS * k * 4
    return pl.pallas_call(
        functools.partial(_rms_stack_kernel, prompt_blocks=prompt_blocks),
        grid=(m // NORM_ROWS,),
        in_specs=[
            pl.BlockSpec((NORM_ROWS, k), lambda i: (jnp.minimum(i, prompt_blocks - 1), 0)),
            pl.BlockSpec((NORM_ROWS, k), lambda i: (0, 0)),
            pl.BlockSpec((1, k), lambda i: (0, 0)),
        ],
        out_specs=pl.BlockSpec((NORM_ROWS, k), lambda i: (i, 0)),
        out_shape=jax.ShapeDtypeStruct((m, k), BF16),
        scratch_shapes=[pltpu.VMEM((NORM_ROWS, LANES), F32)],
        compiler_params=_params(vmem, 1),
        name="rms_stack",
    )(x_prompt, x_sample, gain)


def _head_rms(y, seg2_ref, gain_row):
    y2 = y * y
    hi = y2.astype(BF16)
    lo = (y2 - hi.astype(F32)).astype(BF16)
    seg2 = seg2_ref[...]
    w = seg2.shape[1]
    ssq = jnp.concatenate(
        [jnp.dot(jnp.concatenate([hi[:, c:c + w], lo[:, c:c + w]], axis=1), seg2,
                 preferred_element_type=F32)
         for c in range(0, y.shape[1], w)], axis=-1)
    return y * lax.rsqrt(ssq * (1.0 / HEAD_DIM) + RMS_EPS) * gain_row


def _row_factor(ssq_ref, k, n):
    r = lax.rsqrt(ssq_ref[...] * (1.0 / k) + RMS_EPS)
    return jnp.tile(r, (1, n // LANES))


def _sgu_in_kernel(x_hbm, w_ref, o_ref, x_buf, sem):
    x_ref = _lhs_tile(x_hbm, x_buf, sem)
    w = w_ref[...].astype(BF16)
    for rows in _row_groups(x_ref.shape[0]):
        y = jnp.dot(x_ref[rows, :], w, preferred_element_type=F32)
        o_ref[rows, :] = 0.5 * y * (1.0 + lax.erf(y * (0.5 ** 0.5)))


def _sgu_in(xn, w, layer, *, bm, bn):
    m, k = xn.shape
    n = w.shape[2]
    wb = w.dtype.itemsize
    vmem = 2 * bm * k * 2 + k * bn * (2 * wb + 2) + 5 * bm * bn * 4
    return pl.pallas_call(
        _sgu_in_kernel,
        grid=(m // bm, n // bn),
        in_specs=[_LHS_SPEC, _layer_spec(k, bn, layer, lambda i, j: j)],
        out_specs=pl.BlockSpec((bm, bn), lambda i, j: (i, j)),
        out_shape=jax.ShapeDtypeStruct((m, n), F32),
        scratch_shapes=_lhs_scratch(bm, k),
        compiler_params=_params(vmem, 2),
        name="sgu_in",
    )(xn, w)


def _sgu_mix_rows(u_ref, v_ref, lng_ref, lnb_ref, ws_ref, bs_ref, a_ref, vn_ref, *, chunk):
    ln_g = lng_ref[...]
    ln_b = lnb_ref[...]
    for c in range(u_ref.shape[0] // chunk):
        rows = slice(c * chunk, (c + 1) * chunk)
        v = v_ref[rows, :]
        mu = jnp.mean(v, axis=-1, keepdims=True)
        vc = v - mu
        var = jnp.mean(vc * vc, axis=-1, keepdims=True)
        vn = vc * lax.rsqrt(var + LN_EPS) * ln_g + ln_b
        if vn_ref is not None:
            vn_ref[rows, :] = vn
        vnb = vn.astype(BF16)
        for g in range(SGU_GROUPS):
            cols = slice(g * SGU_GROUP_WIDTH, (g + 1) * SGU_GROUP_WIDTH)
            mixed = jnp.dot(ws_ref[g], vnb[:, cols], preferred_element_type=F32)
            mixed = mixed + bs_ref[:, g:g + 1]
            a_ref[rows, cols] = (u_ref[rows, cols] * mixed).astype(BF16)


def _sgu_mix_kernel(u_ref, v_ref, lng_ref, lnb_ref, ws_ref, bs_ref, wss_ref, bss_ref,
                    a_ref, vn_ref, *, prompt_blocks, sample_chunk):
    common = (u_ref, v_ref, lng_ref, lnb_ref)
    is_prompt = pl.program_id(0) < prompt_blocks
    pl.when(is_prompt)(
        lambda: _sgu_mix_rows(*common, ws_ref, bs_ref, a_ref, None, chunk=SGU_CHUNK))
    pl.when(jnp.logical_not(is_prompt))(
        lambda: _sgu_mix_rows(*common, wss_ref, bss_ref, a_ref, vn_ref, chunk=sample_chunk))


def _sgu_mix(uv, ln_g, ln_b, ws, bs_t, *, m_prompt, sample_chunk):
    m = uv.shape[0]
    d = D_MODEL
    rows = m - m_prompt
    assert m_prompt % rows == 0 and rows % SGU_CHUNK == 0 and rows % sample_chunk == 0
    prompt_blocks = m_prompt // rows
    const = lambda shape: pl.BlockSpec(shape, lambda i: (0,) * len(shape))
    wss, bss = ws[:, :sample_chunk, :sample_chunk], bs_t[:sample_chunk]
    vmem = 2 * 2 * rows * d * 4 + 2 * rows * d * 2 + 2 * rows * d * 4 + 6 * SGU_CHUNK * d * 4
    return pl.pallas_call(
        functools.partial(_sgu_mix_kernel, prompt_blocks=prompt_blocks,
                          sample_chunk=sample_chunk),
        grid=(prompt_blocks + 1,),
        in_specs=[
            pl.BlockSpec((rows, d), lambda i: (i, 0)),
            pl.BlockSpec((rows, d), lambda i: (i, 1)),
            const((1, d)), const((1, d)),
            const(ws.shape), const(bs_t.shape), const(wss.shape), const(bss.shape),
        ],
        out_specs=[pl.BlockSpec((rows, d), lambda i: (i, 0)), const((rows, d))],
        out_shape=[jax.ShapeDtypeStruct((m, d), BF16), jax.ShapeDtypeStruct((rows, d), F32)],
        compiler_params=_params(vmem, 1),
        name="sgu_mix",
    )(uv, uv, ln_g, ln_b, ws, bs_t, wss, bss)


def _residual_kernel(x_hbm, w_ref, *refs, n_res, n_gains, top_rows, tail_rows, col_block):
    x_ref = _lhs_tile(x_hbm, *refs[-2:], col_block=col_block)
    refs = refs[:-2]
    res_refs, refs = refs[:n_res], refs[n_res:]
    gain_refs, refs = refs[:n_gains], refs[n_gains:]
    n_main = 2 if tail_rows else 1
    main_refs, refs = refs[:n_main], refs[n_main:]
    scaled_refs, ssq_refs = refs[:n_gains], refs[n_gains:]
    o_ref = main_refs[0]
    w = w_ref[...].astype(BF16)
    bm = x_ref.shape[0]
    squares = []
    plain = n_res == 1 and n_gains == 0
    for rows in (_row_groups(bm) if plain else [slice(0, bm)]):
        y = jnp.dot(x_ref[rows, :], w, preferred_element_type=F32)
        if n_res == 1:
            h = res_refs[0][rows, :] + y
            o_ref[rows, :] = h
        else:
            top_ref, tail_ref = res_refs
            is_last = pl.program_id(0) == pl.num_programs(0) - 1
            straddle = jnp.where(is_last, tail_ref[...], top_ref[top_rows:, :])
            h = jnp.concatenate([top_ref[:top_rows, :], straddle], axis=0) + y
            o_ref[...] = h
        for g_ref, s_ref in zip(gain_refs, scaled_refs):
            s_ref[rows, :] = (h * g_ref[...]).astype(BF16)
        if n_gains:
            squares.append(jnp.sum(h * h, axis=-1, keepdims=True))
    if tail_rows:
        main_refs[1][...] = o_ref[bm - tail_rows:, :]
    if n_gains:
        (ssq_ref,) = ssq_refs
        part = jnp.broadcast_to(jnp.concatenate(squares, axis=0), ssq_ref.shape)
        first = pl.program_id(1) == 0

        @pl.when(first)
        def _():
            ssq_ref[...] = part

        @pl.when(jnp.logical_not(first))
        def _():
            ssq_ref[...] += part


def _residual(x, w, layer, res, *, bm, bn, name, part=(0, 1), gains=(), split_rows=None):
    m, k = x.shape
    n = w.shape[2]
    p, n_parts = part
    kc = k // n_parts
    assert kc * n_parts == k and kc % LANES == 0
    wb = w.dtype.itemsize
    tile = pl.BlockSpec((bm, bn), lambda i, j: (i, j))
    last = m // bm - 1

    top_rows = 0
    if isinstance(res, tuple):
        top, tail = res
        top_rows = bm - tail.shape[0]
        assert top.shape[0] + tail.shape[0] == m and top_rows > 0
        assert top_rows % F32_SUBLANES == 0
        res_specs = [tile, pl.BlockSpec((tail.shape[0], bn), lambda i, j: (0, j))]
        res_args = [top, tail]
    else:
        res_specs, res_args = [tile], [res]

    tail_rows = 0
    if split_rows is None:
        out_specs, out_shape = [tile], [jax.ShapeDtypeStruct((m, n), F32)]
    else:
        tail_rows = m - split_rows
        assert last * bm < split_rows and tail_rows <= bm
        out_specs = [tile, pl.BlockSpec((tail_rows, bn),
                                        lambda i, j: (0, jnp.where(i == last, j, 0)))]
        out_shape = [jax.ShapeDtypeStruct((split_rows, n), F32),
                     jax.ShapeDtypeStruct((tail_rows, n), F32)]
    out_specs += [tile] * len(gains)
    out_shape += [jax.ShapeDtypeStruct((m, n), BF16)] * len(gains)
    if gains:
        out_specs.append(pl.BlockSpec((bm, LANES), lambda i, j: (i, 0)))
        out_shape.append(jax.ShapeDtypeStruct((m, LANES), F32))

    vmem = 2 * bm * kc * 2 + kc * bn * (2 * wb + 2) + (8 + 2 * len(gains)) * bm * bn * 4
    return pl.pallas_call(
        functools.partial(_residual_kernel, n_res=len(res_args), n_gains=len(gains),
                          top_rows=top_rows, tail_rows=tail_rows, col_block=p),
        grid=(m // bm, n // bn),
        in_specs=[
            _LHS_SPEC,
            _layer_spec(kc, bn, layer, lambda i, j: j, row_block=p),
        ] + res_specs + [pl.BlockSpec((1, bn), lambda i, j: (0, j))] * len(gains),
        out_specs=out_specs,
        out_shape=out_shape,
        scratch_shapes=_lhs_scratch(bm, kc),
        compiler_params=_params(vmem, 2),
        name=name,
    )(x, w, *res_args, *gains)


def _ffn_up_kernel(x_hbm, ssq_hbm, wg_hbm, wu_hbm, o_hbm, *, layer, bm, bn):
    m, k = x_hbm.shape
    n = o_hbm.shape[1]

    def block(x_ref, ssq_ref, wg_ref, wu_ref, o_ref):
        wg = wg_ref[...].astype(BF16)
        wu = wu_ref[...].astype(BF16)
        for rows in _row_groups(bm):
            x = x_ref[rows, :]
            r = _row_factor(ssq_ref.at[rows, :], k, bn)
            gate = jnp.dot(x, wg, preferred_element_type=F32) * r
            up = jnp.dot(x, wu, preferred_element_type=F32) * r
            o_ref[rows, :] = (0.5 * gate * (1.0 + jnp.tanh(0.5 * gate)) * up).astype(BF16)

    w_spec = pl.BlockSpec((k, bn), lambda i, j: (0, j))
    pltpu.emit_pipeline(
        block,
        grid=(m // bm, n // bn),
        in_specs=[
            pl.BlockSpec((bm, k), lambda i, j: (i, 0),
                         pipeline_mode=pl.Buffered(2, use_lookahead=True)),
            pl.BlockSpec((bm, LANES), lambda i, j: (i, 0)),
            w_spec, w_spec,
        ],
        out_specs=[pl.BlockSpec((bm, bn), lambda i, j: (i, j))],
    )(x_hbm, ssq_hbm, wg_hbm.at[layer], wu_hbm.at[layer], o_hbm)


def _ffn_up(x, ssq, w_gate, w_up, layer, *, bm, bn):
    m, k = x.shape
    n = w_gate.shape[2]
    wb = w_gate.dtype.itemsize
    vmem = 2 * bm * k * 2 + 2 * k * bn * (2 * wb + 2) + 2 * bm * bn * 2 + 5 * bm * bn * 4
    any_spec = pl.BlockSpec(memory_space=pl.ANY)
    return pl.pallas_call(
        functools.partial(_ffn_up_kernel, layer=layer, bm=bm, bn=bn),
        in_specs=[any_spec] * 4,
        out_specs=any_spec,
        out_shape=jax.ShapeDtypeStruct((m, n), BF16),
        compiler_params=pltpu.CompilerParams(
            vmem_limit_bytes=min(int(vmem) + VMEM_ESTIMATE_MARGIN, VMEM_LIMIT_CAP)),
        name="ffn_up",
    )(x, ssq, w_gate, w_up)


LOG2_E = 1.4426950408889634
SCORE_SCALE = HEAD_DIM ** -0.5 * LOG2_E


def _q_proj_kernel(x_hbm, ssq_ref, w_ref, seg2_ref, qg_ref, q_ref, x_buf, sem):
    x = _lhs_tile(x_hbm, x_buf, sem)[...]
    y = jnp.dot(x, w_ref[...].astype(BF16), preferred_element_type=F32)
    y = y * _row_factor(ssq_ref, x.shape[1], y.shape[1])
    q_ref[...] = _head_rms(y, seg2_ref, qg_ref[...]).astype(BF16)


def _q_proj(x, ssq, w, layer, seg2, q_gain, *, bm, bn):
    m, k = x.shape
    n = w.shape[2]
    wb = w.dtype.itemsize
    vmem = 2 * bm * k * 2 + k * bn * (2 * wb + 2) + 10 * bm * bn * 4
    return pl.pallas_call(
        _q_proj_kernel,
        grid=(m // bm, n // bn),
        in_specs=[
            _LHS_SPEC,
            pl.BlockSpec((bm, LANES), lambda i, j: (i, 0)),
            _layer_spec(k, bn, layer, lambda i, j: j),
            pl.BlockSpec(seg2.shape, lambda i, j: (0, 0)),
            pl.BlockSpec((1, bn), lambda i, j: (0, 0)),
        ],
        out_specs=pl.BlockSpec((bm, bn), lambda i, j: (i, j)),
        out_shape=jax.ShapeDtypeStruct((m, n), BF16),
        scratch_shapes=_lhs_scratch(bm, k),
        compiler_params=_params(vmem, 2),
        name="q_proj",
    )(x, ssq, w, seg2, q_gain)


def _kv_proj_kernel(x_ref, ssq_ref, w_ref, seg2_ref, kg_ref, kf_ref, vf_ref, kz_ref, vz_ref):
    x = x_ref[...]
    y = jnp.dot(x, w_ref[...], preferred_element_type=F32)
    y = y * _row_factor(ssq_ref, x.shape[1], y.shape[1])
    kn = _head_rms(y[:, :KV_WIDTH], seg2_ref, kg_ref[...])
    v = y[:, KV_WIDTH:]
    kf_ref[...] = kn
    vf_ref[...] = v
    low = lax.broadcasted_iota(jnp.int32, (y.shape[0], LANES), 1) < HEAD_DIM
    for src, dst in ((kn, kz_ref), (v, vz_ref)):
        for p in range(N_KV_HEADS // HEADS_PER_VREG):
            both = src[:, p * LANES:(p + 1) * LANES]
            swapped = pltpu.roll(both, HEAD_DIM, axis=1)
            dst[2 * p, 0] = jnp.where(low, both, 0.0).astype(BF16)
            dst[2 * p, 1] = jnp.where(low, 0.0, swapped).astype(BF16)
            dst[2 * p + 1, 0] = jnp.where(low, swapped, 0.0).astype(BF16)
            dst[2 * p + 1, 1] = jnp.where(low, 0.0, both).astype(BF16)


def _kv_proj(x, ssq, w, seg2, k_gain, *, bm):
    m, k = x.shape
    n = w.shape[1]
    vmem = (2 * bm * k * 2 + 2 * k * n * 2 + 8 * bm * n * 4
            + 2 * 2 * N_KV_HEADS * 2 * bm * LANES * 2)
    z_spec = pl.BlockSpec((N_KV_HEADS, 2, bm, LANES), lambda i: (0, 0, i, 0))
    row_spec = pl.BlockSpec((bm, KV_WIDTH), lambda i: (i, 0))
    z_shape = jax.ShapeDtypeStruct((N_KV_HEADS, 2, m, LANES), BF16)
    return pl.pallas_call(
        _kv_proj_kernel,
        grid=(m // bm,),
        in_specs=[
            pl.BlockSpec((bm, k), lambda i: (i, 0)),
            pl.BlockSpec((bm, LANES), lambda i: (i, 0)),
            pl.BlockSpec((k, n), lambda i: (0, 0)),
            pl.BlockSpec(seg2.shape, lambda i: (0, 0)),
            pl.BlockSpec((1, KV_WIDTH), lambda i: (0, 0)),
        ],
        out_specs=[row_spec, row_spec, z_spec, z_spec],
        out_shape=[
            jax.ShapeDtypeStruct((m, KV_WIDTH), F32),
            jax.ShapeDtypeStruct((m, KV_WIDTH), F32),
            z_shape, z_shape,
        ],
        compiler_params=_params(vmem, 1),
        name="kv_proj",
    )(x, ssq, w, seg2, k_gain)


def _attend(q_ref, o_ref, sink_ref, key_parts, val_parts, *, row0, tq, mask_lanes=None,
            score_bound=None):
    rows = PAIRS * tq
    lane_pair = lax.broadcasted_iota(jnp.int32, (1, rows), 1) >> (tq.bit_length() - 1)
    for kh in range(N_KV_HEADS):
        base = kh * GQA * HEAD_DIM
        q4 = jnp.concatenate(
            [q_ref[row0:row0 + tq, base + p * LANES:base + (p + 1) * LANES]
             for p in range(PAIRS)], axis=0)
        kk = jnp.concatenate(key_parts(kh, 0) + key_parts(kh, 1), axis=0)
        vv = jnp.concatenate(val_parts(kh, 0) + val_parts(kh, 1), axis=0)
        n_keys = kk.shape[0] // HEADS_PER_VREG
        if mask_lanes is not None:
            kk = jnp.concatenate([kk, mask_lanes[0]], axis=1)
            q4 = jnp.concatenate([q4, mask_lanes[1]], axis=1)
        s_t = lax.dot_general(kk, q4, (((1,), (1,)), ((), ())),
                              preferred_element_type=F32)
        exps, denoms = [], []
        for parity in range(HEADS_PER_VREG):
            head = kh * GQA + parity
            sink = jnp.full((1, rows), sink_ref[head + HEADS_PER_VREG * (PAIRS - 1)], F32)
            for p in range(PAIRS - 1):
                sink = jnp.where(lane_pair == p, sink_ref[head + HEADS_PER_VREG * p], sink)
            sink = sink * LOG2_E
            half = s_t[parity * n_keys:(parity + 1) * n_keys]
            if score_bound is None:
                mx = jnp.maximum(jnp.max(half, axis=0, keepdims=True), sink)
            else:
                mx = jnp.maximum(score_bound, sink)
            e = jnp.exp2(half - mx)
            denom = jnp.sum(e, axis=0, keepdims=True) + jnp.exp2(sink - mx)
            denoms.append(jnp.broadcast_to(denom, (HEAD_DIM, rows)))
            exps.append(e.astype(BF16))
        o_t = lax.dot_general(vv, jnp.concatenate(exps, axis=0),
                              (((0,), (0,)), ((), ())), preferred_element_type=F32)
        o = (o_t / jnp.concatenate(denoms, axis=0)).T
        for p in range(PAIRS):
            o_ref[row0:row0 + tq, base + p * LANES:base + (p + 1) * LANES] = (
                o[p * tq:(p + 1) * tq].astype(BF16))


def _attend_prompt_block(q_ref, o_ref, sink_ref, key_parts, val_parts, first, *, row0,
                         score_bound=None):
    tq = ATT_ROWS
    n_keys = 2 * tq
    chunk_shift = CHUNK.bit_length() - 1
    chunks_per_block = tq // CHUNK
    lane = lax.broadcasted_iota(jnp.int32, (2 * n_keys, LANES), 1)
    key = lax.broadcasted_iota(jnp.int32, (2 * n_keys, LANES), 0)
    k_chunk = ((key & (n_keys - 1)) >> chunk_shift) - chunks_per_block
    oldest = jnp.where(first, 0, -chunks_per_block)
    visible = (k_chunk <= lane) & (k_chunk >= jnp.maximum(lane - 2, oldest))
    key_side = jnp.where((lane < chunks_per_block) & jnp.logical_not(visible), NEG, 0.0)
    query = lax.broadcasted_iota(jnp.int32, (PAIRS * tq, LANES), 0)
    q_lane = lax.broadcasted_iota(jnp.int32, (PAIRS * tq, LANES), 1)
    query_side = jnp.where(((query & (tq - 1)) >> chunk_shift) == q_lane, 1.0, 0.0)
    _attend(q_ref, o_ref, sink_ref, key_parts, val_parts,
            row0=row0, tq=tq, mask_lanes=(key_side.astype(BF16), query_side.astype(BF16)),
            score_bound=score_bound)


def _attend_prompt_step(q_ref, kp_ref, kc_ref, vp_ref, vc_ref, sink_ref, o_ref, first,
                        score_bound=None):
    tq = ATT_ROWS
    for b in range(ATT_STEP_ROWS // tq):
        here = slice(b * tq, (b + 1) * tq)
        behind = slice((b - 1) * tq, b * tq)
        if b == 0:
            parts = lambda prev, cur, here=here: (
                lambda kh, parity: [prev[kh, parity], cur[kh, parity, here, :]])
        else:
            parts = lambda prev, cur, here=here, behind=behind: (
                lambda kh, parity: [cur[kh, parity, behind, :], cur[kh, parity, here, :]])
        _attend_prompt_block(q_ref, o_ref, sink_ref, parts(kp_ref, kc_ref), parts(vp_ref, vc_ref),
                             first if b == 0 else False, row0=b * tq, score_bound=score_bound)


def _attend_sample_block(q_ref, ck_ref, cv_ref, kc_ref, vc_ref, sink_ref, o_ref, *, dec_seq):
    for st in range(ATT_STEP_ROWS // dec_seq):
        new = slice(st * dec_seq, (st + 1) * dec_seq)
        parts = lambda cache, cur, st=st, new=new: (
            lambda kh, parity: [cache[st, kh, parity], cur[kh, parity, new, :]])
        _attend(q_ref, o_ref, sink_ref, parts(ck_ref, kc_ref), parts(cv_ref, vc_ref),
                row0=st * dec_seq, tq=dec_seq)


def _attn_kernel(q_ref, kp_ref, kc_ref, vp_ref, vc_ref, ck_ref, cv_ref, sink_ref, bound_ref,
                 o_ref, *, prompt_blocks, blocks_per_seq, dec_seq):
    step = pl.program_id(0)
    is_prompt = step < prompt_blocks
    first = (step & (blocks_per_seq - 1)) == 0
    prompt_refs = (q_ref, kp_ref, kc_ref, vp_ref, vc_ref, sink_ref, o_ref, first)
    bound = bound_ref[0]
    use_bound = bound < MAX_SCORE_SHIFT
    pl.when(is_prompt & use_bound)(
        lambda: _attend_prompt_step(*prompt_refs, score_bound=bound))
    pl.when(is_prompt & jnp.logical_not(use_bound))(
        lambda: _attend_prompt_step(*prompt_refs))
    pl.when(jnp.logical_not(is_prompt))(lambda: _attend_sample_block(
        q_ref, ck_ref, cv_ref, kc_ref, vc_ref, sink_ref, o_ref, dec_seq=dec_seq))


MAX_SCORE_SHIFT = 60.0


def _score_bound(q_gain, k_gain):
    margin = 1.02
    bound = margin * SCORE_SCALE * HEAD_DIM * jnp.max(jnp.abs(q_gain)) * jnp.max(jnp.abs(k_gain))
    return bound.reshape(1).astype(F32)


def _attn(q, kz, vz, cache_kz, cache_vz, sinks, score_bound, *, m_prompt, seq, dec_seq):
    m = q.shape[0]
    tq = ATT_STEP_ROWS
    blocks_per_step = tq // ATT_ROWS
    steps_per_seq = seq // tq
    prompt_blocks = m_prompt // tq
    streams_per_block = tq // dec_seq
    assert steps_per_seq & (steps_per_seq - 1) == 0 and (m - m_prompt) % tq == 0
    blocks_per_seq = steps_per_seq
    rows_here = lambda s: (s, 0)
    cur = lambda s: (0, 0, s, 0)
    prev = lambda s: (0, 0, jnp.where((s & (steps_per_seq - 1)) == 0, s * blocks_per_step,
                                      s * blocks_per_step - 1), 0)
    cache = lambda s: (jnp.maximum(s - prompt_blocks, 0), 0, 0, 0, 0)
    z_block = (N_KV_HEADS, 2, tq, LANES)
    behind_block = (N_KV_HEADS, 2, ATT_ROWS, LANES)
    cache_block = (streams_per_block, N_KV_HEADS, 2, WINDOW, LANES)
    vmem = 4 * tq * D_MODEL * 2 + 6 * N_KV_HEADS * 2 * tq * LANES * 2 \
        + 4 * streams_per_block * N_KV_HEADS * 2 * WINDOW * LANES * 2 \
        + 10 * PAIRS * ATT_ROWS * 4 * ATT_ROWS * 4
    return pl.pallas_call(
        functools.partial(_attn_kernel, prompt_blocks=prompt_blocks,
                          blocks_per_seq=blocks_per_seq, dec_seq=dec_seq),
        grid=(m // tq,),
        in_specs=[
            pl.BlockSpec((tq, D_MODEL), rows_here),
            pl.BlockSpec(behind_block, prev),
            pl.BlockSpec(z_block, cur),
            pl.BlockSpec(behind_block, prev),
            pl.BlockSpec(z_block, cur),
            pl.BlockSpec(cache_block, cache),
            pl.BlockSpec(cache_block, cache),
            pl.BlockSpec(memory_space=pltpu.SMEM),
            pl.BlockSpec(memory_space=pltpu.SMEM),
        ],
        out_specs=pl.BlockSpec((tq, D_MODEL), rows_here),
        out_shape=jax.ShapeDtypeStruct((m, D_MODEL), BF16),
        compiler_params=_params(vmem, 1),
        name="attn",
    )(q, kz, kz, vz, vz, cache_kz, cache_vz, sinks, score_bound)


def _lane_pair_copies(cache):
    c = cache.transpose(0, 2, 1, 3).astype(BF16)
    z = jnp.zeros_like(c)
    return jnp.stack([jnp.concatenate([c, z], axis=-1), jnp.concatenate([z, c], axis=-1)], axis=2)


def kernel(x_prompt, x_sample, cache_k, cache_v, norm_a, w_sgu_in, sgu_ln_g, sgu_ln_b, w_sgu_s,
           b_sgu_s, w_sgu_out, norm_kv, w_kv, k_norm, norm_b, w_q, q_norm, sinks, w_o,
           norm_ffn, w_ffn_gate, w_ffn_up, w_ffn_down):
    batch, seq, d = x_prompt.shape
    streams, dec_seq, _ = x_sample.shape
    m_prompt = batch * seq
    m_sample = streams * dec_seq
    m = m_prompt + m_sample
    assert d == D_MODEL and seq % ATT_STEP_ROWS == 0 and m_sample % ATT_STEP_ROWS == 0
    assert m % WIDE_TILE == 0 and m % KV_TILE == 0
    assert norm_a.shape[0] == 1 and norm_b.shape[0] == 1 and norm_ffn.shape[0] == 2

    row = lambda g: g.reshape(1, -1).astype(F32)
    tiles = dict(bm=WIDE_TILE, bn=MXU_WIDTH)

    def ffn(h, scaled, ssq, layer, **last_part):
        hidden = _ffn_up(scaled, ssq, w_ffn_gate, w_ffn_up, layer, **tiles)
        for p in range(FFN_DOWN_PARTS):
            final = p == FFN_DOWN_PARTS - 1
            outs = _residual(hidden, w_ffn_down, layer, h, part=(p, FFN_DOWN_PARTS),
                             name="ffn_down_%d_%d" % (layer, p), **tiles,
                             **(last_part if final else {}))
            h = outs[0]
        return outs

    x_rows = (x_prompt.reshape(m_prompt, d), x_sample.reshape(m_sample, d))
    xn = _rms_stack(*x_rows, row(norm_a[0]))
    uv = _sgu_in(xn, w_sgu_in, 0, bm=WIDE_TILE, bn=2 * MXU_WIDTH)

    pos_chunk = jnp.arange(SGU_CHUNK) // CHUNK
    ws = jnp.where((pos_chunk[:, None] >= pos_chunk[None, :])[None], w_sgu_s[0], 0.0).astype(BF16)
    bs_t = b_sgu_s[0].T.astype(F32)
    ln_g, ln_b = row(sgu_ln_g[0]), row(sgu_ln_b[0])
    a, v_rows = _sgu_mix(uv, ln_g, ln_b, ws, bs_t, m_prompt=m_prompt, sample_chunk=dec_seq)
    h, scaled, ssq = _residual(a, w_sgu_out, 0, x_rows, name="sgu_out", **tiles,
                               gains=[row(norm_ffn[0])])
    h, scaled_kv, scaled_q, ssq = ffn(h, scaled, ssq, 0, gains=[row(norm_kv), row(norm_b[0])])

    lane_head = jnp.arange(MXU_WIDTH) // HEAD_DIM
    seg = (lane_head[:, None] == lane_head[None, :]).astype(BF16)
    seg2 = jnp.concatenate([seg, seg], axis=0)
    k_rows, v_rows_kv, kz, vz = _kv_proj(
        scaled_kv, ssq, w_kv.astype(BF16), seg2, row(jnp.tile(k_norm, N_KV_HEADS)), bm=KV_TILE)
    q_gain = row(jnp.tile(q_norm[0], MXU_WIDTH // HEAD_DIM)) * SCORE_SCALE
    q = _q_proj(scaled_q, ssq, w_q, 0, seg2, q_gain, **tiles)
    sink = sinks[0].astype(F32)
    o = _attn(q, kz, vz, _lane_pair_copies(cache_k), _lane_pair_copies(cache_v), sink,
              _score_bound(q_norm[0], k_norm), m_prompt=m_prompt, seq=seq, dec_seq=dec_seq)
    h, scaled, ssq = _residual(o, w_o, 0, h, name="attn_out", **tiles,
                               gains=[row(norm_ffn[1])])
    y_prompt, y_sample = ffn(h, scaled, ssq, 1, split_rows=m_prompt)

    heads = lambda t: t.reshape(t.shape[:-1] + (N_KV_HEADS, HEAD_DIM))
    tail = lambda t: heads(t[:m_prompt].reshape(batch, seq, KV_WIDTH)[:, seq - WINDOW:])
    fresh = lambda t: heads(t[m_prompt:].reshape(streams, dec_seq, KV_WIDTH))
    k_p, v_p = tail(k_rows), tail(v_rows_kv)
    k_s, v_s = fresh(k_rows), fresh(v_rows_kv)
    return (y_prompt.reshape(batch, seq, d), y_sample.reshape(streams, dec_seq, d),
            k_p, v_p, k_s, v_s, v_rows.reshape(1, streams, dec_seq, d))
```
